```python
import jax
import jax.numpy as jnp
from jax import lax
import numpy as np

D_MODEL = 1024
BATCH = 8
SEQ = 4096
DEPTH = 2

GRID_W = 64
CTX_LEN = 256

NA_HEAD_DIM = 64
NA_HEADS = (D_MODEL // 2) // NA_HEAD_DIM
NA_WIDTH = NA_HEADS * NA_HEAD_DIM
NA_KH = 8
NA_KW = 16
ML_HEAD_DIM = 128
ML_HEADS = (D_MODEL // 2) // ML_HEAD_DIM
ML_WIDTH = ML_HEADS * ML_HEAD_DIM
ML_CHUNK = 128
ML_GATE_CAP = 15.0
AB_IN_WIDTH = 3 * NA_WIDTH + 4 * ML_WIDTH + 4 * ML_HEADS
RW_HEAD_DIM = 64
RW_HEADS = D_MODEL // RW_HEAD_DIM
RW_DECAY_LORA = 64
RW_AAA_LORA = 64
RW_GATE_LORA = 160
RW_GN_EPS = 64e-5
D_FF = 4 * D_MODEL
ROPE_BASE = 10000.0
NORM_EPS = 1e-6

kernel_name = 'hybrid_natten_mlstm_rwkv7_dit'


def rms_norm(x, eps=NORM_EPS):
    xf = x.astype(jnp.float32)
    return (xf * lax.rsqrt(jnp.mean(xf * xf, axis=-1, keepdims=True) + eps)).astype(x.dtype)


def modulate(x, shift, scale):
    return rms_norm(x) * (1.0 + scale) + shift


def split_heads(t, n_heads, head_dim):
    return t.reshape(*t.shape[:-1], n_heads, head_dim)


def adaln(cond, w, b):
    return jnp.split(jax.nn.silu(cond) @ w + b, 6, axis=-1)


def squared_relu_mlp(h, w1, w2):
    return jnp.square(jax.nn.relu(h @ w1)) @ w2


def axial_rope_angles(n_tokens, head_dim):
    pos = jnp.arange(n_tokens)
    rows = (pos // GRID_W).astype(jnp.float32)
    cols = (pos % GRID_W).astype(jnp.float32)
    n_freq = head_dim // 4
    inv_freq = ROPE_BASE ** (-jnp.arange(n_freq, dtype=jnp.float32) / n_freq)
    return rows[:, None] * inv_freq, cols[:, None] * inv_freq


def rope_rotate(x, ang):
    x1, x2 = jnp.split(x, 2, axis=-1)
    cos = jnp.cos(ang)[:, None, :].astype(x.dtype)
    sin = jnp.sin(ang)[:, None, :].astype(x.dtype)
    return jnp.concatenate([x1 * cos - x2 * sin, x1 * sin + x2 * cos], axis=-1)


def axial_rope(x, ang_r, ang_c):
    x_row, x_col = jnp.split(x, 2, axis=-1)
    return jnp.concatenate([rope_rotate(x_row, ang_r), rope_rotate(x_col, ang_c)], axis=-1)


def context_attention(q, k, v):
    s = jnp.einsum('bqhd,bkhd->bhqk', q, k).astype(jnp.float32) * (q.shape[-1] ** -0.5)
    p = jax.nn.softmax(s, axis=-1).astype(v.dtype)
    return jnp.einsum('bhqk,bkhd->bqhd', p, v)


def neighbourhood_attention(q, k, v, k_ctx, v_ctx, rpb):
    n_b, n_tok, n_h, d_h = q.shape
    rows = n_tok // GRID_W
    kh = min(NA_KH, rows)
    n_win = kh * NA_KW
    scale = d_h ** -0.5
    qg = q.reshape(n_b, rows, GRID_W, n_h, d_h)
    kg = k.reshape(n_b, rows, GRID_W, n_h, d_h)
    vg = v.reshape(n_b, rows, GRID_W, n_h, d_h)
    row_idx = jnp.arange(rows)
    row_start = jnp.clip(row_idx - kh // 2, 0, rows - kh)
    col_idx = jnp.arange(GRID_W)
    col_win = jnp.clip(col_idx - NA_KW // 2, 0, GRID_W - NA_KW)[:, None] + jnp.arange(NA_KW)[None, :]
    rpb_c = rpb[:, :, col_win - col_idx[:, None] + NA_KW - 1]

    def one_row(args):
        r, rs, q_row = args
        k_nb = lax.dynamic_slice_in_dim(kg, rs, kh, axis=1)[:, :, col_win]
        v_nb = lax.dynamic_slice_in_dim(vg, rs, kh, axis=1)[:, :, col_win]
        bias = rpb_c[:, rs + jnp.arange(kh) - r + NA_KH - 1]
        s_nb = jnp.einsum('bchd,bicjhd->bhcij', q_row, k_nb).astype(jnp.float32) * scale + jnp.transpose(bias, (0, 2, 1, 3))
        s_cx = jnp.einsum('bchd,bnhd->bhcn', q_row, k_ctx).astype(jnp.float32) * scale
        p = jax.nn.softmax(jnp.concatenate([s_nb.reshape(n_b, n_h, GRID_W, n_win), s_cx], axis=-1), axis=-1).astype(v.dtype)
        p_nb = p[..., :n_win].reshape(n_b, n_h, GRID_W, kh, NA_KW)
        return jnp.einsum('bhcij,bicjhd->bchd', p_nb, v_nb) + jnp.einsum('bhcn,bnhd->bchd', p[..., n_win:], v_ctx)

    out = lax.map(one_row, (row_idx, row_start, jnp.moveaxis(qg, 1, 0)))
    return jnp.moveaxis(out, 0, 1).reshape(n_b, n_tok, n_h, d_h)


def mlstm_chunked(q, k, v, log_i, log_f, state, with_out):
    n_b, n_h, n_t, _ = q.shape
    n_c = n_t // ML_CHUNK

    def chunks(t):
        return jnp.moveaxis(t.reshape(n_b, n_h, n_c, ML_CHUNK, *t.shape[3:]), 2, 0)

    tril = jnp.tril(jnp.ones((ML_CHUNK, ML_CHUNK), dtype=bool))

    def step(carry, inp):
        c_st, n_st, m_st = carry
        qc, kc, vc, ic, fc = inp
        b = jnp.cumsum(fc, axis=-1)
        log_d = jnp.where(tril, b[..., :, None] - b[..., None, :] + ic[..., None, :], -jnp.inf)
        log_inter = b + m_st[..., None]
        m_t = jnp.maximum(log_inter, jnp.max(log_d, axis=-1))
        m_new = m_t[..., -1]
        w_s = jnp.exp(b[..., -1:] - b + ic - m_new[..., None])
        decay = jnp.exp(b[..., -1] + m_st - m_new)
        c_new = decay[..., None, None] * c_st + jnp.einsum('bhs,bhsd,bhse->bhde', w_s, kc, vc)
        n_new = decay[..., None] * n_st + jnp.einsum('bhs,bhsd->bhd', w_s, kc)
        if not with_out:
            return (c_new, n_new, m_new), None
        d_mat = jnp.exp(log_d - m_t[..., None])
        w_inter = jnp.exp(log_inter - m_t)
        s = jnp.einsum('bhtd,bhsd->bhts', qc, kc) * d_mat
        num = jnp.einsum('bhts,bhse->bhte', s, vc) + w_inter[..., None] * jnp.einsum('bhtd,bhde->bhte', qc, c_st)
        den = jnp.sum(s, axis=-1) + w_inter * jnp.einsum('bhtd,bhd->bht', qc, n_st)
        h = num / jnp.maximum(jnp.abs(den), jnp.exp(-m_t))[..., None]
        return (c_new, n_new, m_new), h

    state, hs = lax.scan(step, state, tuple(chunks(t) for t in (q, k, v, log_i, log_f)))
    if not with_out:
        return None, state
    return jnp.moveaxis(hs, 0, 2).reshape(n_b, n_h, n_t, -1), state


def mlstm_bidir(q_l, k_l, v_l, g_l, q_c, k_c, v_c, g_c, need_ctx):
    def heads_first(t):
        return jnp.swapaxes(t, 1, 2).astype(jnp.float32)

    def gates(g):
        g = ML_GATE_CAP * jnp.tanh(g.astype(jnp.float32) / ML_GATE_CAP)
        i_f, f_f, i_b, f_b = jnp.split(jnp.swapaxes(g, 1, 2), 4, axis=1)
        return i_f, jax.nn.log_sigmoid(f_f), i_b, jax.nn.log_sigmoid(f_b)

    def flip(t):
        return jnp.flip(t, axis=2)

    n_b = q_l.shape[0]
    zero = (jnp.zeros((n_b, ML_HEADS, ML_HEAD_DIM, ML_HEAD_DIM), jnp.float32),
            jnp.zeros((n_b, ML_HEADS, ML_HEAD_DIM), jnp.float32),
            jnp.zeros((n_b, ML_HEADS), jnp.float32))
    qc, kc, vc = heads_first(q_c), heads_first(k_c), heads_first(v_c)
    if_c, lf_c, ib_c, lb_c = gates(g_c)
    hc_f, st_f = mlstm_chunked(qc, kc, vc, if_c, lf_c, zero, need_ctx)
    hc_b, st_b = mlstm_chunked(flip(qc), flip(kc), flip(vc), flip(ib_c), flip(lb_c), zero, need_ctx)
    ql, kl, vl = heads_first(q_l), heads_first(k_l), heads_first(v_l)
    if_l, lf_l, ib_l, lb_l = gates(g_l)
    hl_f, _ = mlstm_chunked(ql, kl, vl, if_l, lf_l, st_f, True)
    hl_b, _ = mlstm_chunked(flip(ql), flip(kl), flip(vl), flip(ib_l), flip(lb_l), st_b, True)
    h_lat = jnp.swapaxes(hl_f + flip(hl_b), 1, 2)
    h_ctx = jnp.swapaxes(hc_f + flip(hc_b), 1, 2) if need_ctx else None
    return h_lat, h_ctx


def ab_mixer(h_l, h_c, w_in, gate_b, q_norm, k_norm, rpb, head_norm, w_out, ang_r, ang_c, need_ctx):
    splits = [NA_WIDTH, 2 * NA_WIDTH, 3 * NA_WIDTH, 3 * NA_WIDTH + ML_WIDTH, 3 * NA_WIDTH + 2 * ML_WIDTH,
              3 * NA_WIDTH + 3 * ML_WIDTH, 3 * NA_WIDTH + 4 * ML_WIDTH]

    def project(h):
        qa, ka, va, qb, kb, vb, ob, g = jnp.split(h @ w_in, splits, axis=-1)
        qa = rms_norm(split_heads(qa, NA_HEADS, NA_HEAD_DIM)) * q_norm
        ka = rms_norm(split_heads(ka, NA_HEADS, NA_HEAD_DIM)) * k_norm
        va = split_heads(va, NA_HEADS, NA_HEAD_DIM)
        qb = split_heads(qb, ML_HEADS, ML_HEAD_DIM)
        kb = split_heads(kb, ML_HEADS, ML_HEAD_DIM) * (ML_HEAD_DIM ** -0.5)
        vb = split_heads(vb, ML_HEADS, ML_HEAD_DIM)
        return qa, ka, va, qb, kb, vb, ob, g + gate_b

    qa_l, ka_l, va_l, qb_l, kb_l, vb_l, ob_l, g_l = project(h_l)
    qa_c, ka_c, va_c, qb_c, kb_c, vb_c, ob_c, g_c = project(h_c)
    qb_l = axial_rope(qb_l, ang_r, ang_c)
    kb_l = axial_rope(kb_l, ang_r, ang_c)
    na_l = neighbourhood_attention(qa_l, ka_l, va_l, ka_c, va_c, rpb)
    ml_l, ml_c = mlstm_bidir(qb_l, kb_l, vb_l, g_l, qb_c, kb_c, vb_c, g_c, need_ctx)

    def merge(na, ml, o):
        ml = (rms_norm(ml) * head_norm).astype(o.dtype) * jax.nn.sigmoid(split_heads(o, ML_HEADS, ML_HEAD_DIM))
        cat = jnp.concatenate([na.reshape(*na.shape[:2], NA_WIDTH), ml.reshape(*ml.shape[:2], ML_WIDTH)], axis=-1)
        return cat @ w_out

    out_l = merge(na_l, ml_l, ob_l)
    out_c = merge(context_attention(qa_c, ka_c, va_c), ml_c, ob_c) if need_ctx else None
    return out_l, out_c


def centred_shift(x):
    xp = jnp.pad(x, ((0, 0), (1, 1), (0, 0)))
    return 0.5 * (xp[:, :-2] + xp[:, 2:])


def rwkv_prepare(h, mu, w_rkv, w0, w1, w2, a0, a1, a2, g1, g2, k_k, k_a):
    xx = centred_shift(h) - h
    xr, xw, xk, xv, xa, xg = h[None] + xx[None] * mu[:, None, None, :]
    r, k, v = jnp.einsum('sbtd,sde->sbte', jnp.stack([xr, xk, xv]), w_rkv)
    w_logit = w0[:, None, None, :] + jnp.einsum('zbtr,zrd->zbtd', jnp.tanh(jnp.einsum('btd,zdr->zbtr', xw, w1)), w2)
    decay = jnp.exp(-jnp.exp(-jax.nn.softplus(-w_logit.astype(jnp.float32)) - 0.5))
    a = jax.nn.sigmoid(a0[:, None, None, :] + jnp.einsum('zbtr,zrd->zbtd', jnp.einsum('btd,zdr->zbtr', xa, a1), a2))
    g = jax.nn.sigmoid(xg @ g1) @ g2
    kk = split_heads((k * k_k).astype(jnp.float32), RW_HEADS, RW_HEAD_DIM)
    kk = (kk / jnp.maximum(jnp.linalg.norm(kk, axis=-1, keepdims=True), 1e-12)).reshape(k.shape)
    k_dir = k[None] * (1.0 + (a - 1.0) * k_a)
    return r, k_dir, v, decay, -kk, kk[None] * a, g


def rwkv7_scan(r, w, k, v, a, b, state, reverse, with_out):
    def time_major(t):
        return jnp.moveaxis(split_heads(t.astype(jnp.float32), RW_HEADS, RW_HEAD_DIM), 1, 0)

    def step(s, inp):
        rt, wt, kt, vt, at, bt = inp
        sa = jnp.einsum('bhij,bhj->bhi', s, at)
        s = s * wt[:, :, None, :] + sa[..., None] * bt[:, :, None, :] + vt[..., None] * kt[:, :, None, :]
        y = jnp.einsum('bhij,bhj->bhi', s, rt) if with_out else None
        return s, y

    state, ys = lax.scan(step, state, tuple(time_major(t) for t in (r, w, k, v, a, b)), reverse=reverse)
    return (jnp.moveaxis(ys, 0, 1) if with_out else None), state


def rwkv_mixer(h_l, h_c, mu, w_rkv, w0, w1, w2, a0, a1, a2, g1, g2, k_k, k_a, r_k, lnx_w, lnx_b, w_o, need_ctx):
    lat = rwkv_prepare(h_l, mu, w_rkv, w0, w1, w2, a0, a1, a2, g1, g2, k_k, k_a)
    cxt = rwkv_prepare(h_c, mu, w_rkv, w0, w1, w2, a0, a1, a2, g1, g2, k_k, k_a)
    s0 = jnp.zeros((h_l.shape[0], RW_HEADS, RW_HEAD_DIM, RW_HEAD_DIM), jnp.float32)

    def bidir(p, s_f, s_b, with_out):
        r, k_dir, v, decay, a, b_dir, _ = p
        y_f, s_f = rwkv7_scan(r, decay[0], k_dir[0], v, a, b_dir[0], s_f, False, with_out)
        y_b, s_b = rwkv7_scan(r, decay[1], k_dir[1], v, a, b_dir[1], s_b, True, with_out)
        return y_f, y_b, s_f, s_b

    yc_f, yc_b, st_f, st_b = bidir(cxt, s0, s0, need_ctx)
    yl_f, yl_b, _, _ = bidir(lat, st_f, st_b, True)

    def readout(p, y_f, y_b):
        r, k_dir, v, _, _, _, g = p
        y = y_f + y_b
        mean = jnp.mean(y, axis=-1, keepdims=True)
        var = jnp.mean(jnp.square(y - mean), axis=-1, keepdims=True)
        y = ((y - mean) * lax.rsqrt(var + RW_GN_EPS)).reshape(r.shape) * lnx_w + lnx_b
        coef = jnp.sum(split_heads(r, RW_HEADS, RW_HEAD_DIM)[None] * split_heads(k_dir, RW_HEADS, RW_HEAD_DIM)
                       * split_heads(r_k, RW_HEADS, RW_HEAD_DIM), axis=(0, -1))[..., None]
        bonus = (coef * split_heads(v, RW_HEADS, RW_HEAD_DIM)).reshape(r.shape)
        return ((y + bonus) * g).astype(h_l.dtype) @ w_o

    out_l = readout(lat, yl_f, yl_b)
    out_c = readout(cxt, yc_f, yc_b) if need_ctx else None
    return out_l, out_c


def setup_inputs(seed: int = 0) -> dict:
    key = jax.random.key(seed)
    keys = iter(jax.random.split(key, 48))
    n_even = (DEPTH + 1) // 2
    n_odd = DEPTH // 2
    d = D_MODEL

    def normal(shape, scale):
        return scale * jax.random.normal(next(keys), shape, jnp.float32)

    def uniform(shape, lo, hi):
        return jax.random.uniform(next(keys), shape, jnp.float32, lo, hi)

    ab_gate_b = jnp.concatenate([normal((n_even, ML_HEADS), 0.1), uniform((n_even, ML_HEADS), 3.0, 6.0),
                                 normal((n_even, ML_HEADS), 0.1), uniform((n_even, ML_HEADS), 3.0, 6.0)], axis=-1)
    return {
        'x': normal((BATCH, SEQ, d), 1.0),
        'c': normal((BATCH, d), 1.0),
        'ctx': normal((BATCH, CTX_LEN, d), 1.0),
        'c_ctx': normal((d,), 1.0),
        'ada_w': normal((DEPTH, d, 6 * d), 0.5 * d ** -0.5),
        'ada_b': normal((DEPTH, 6 * d), 0.02),
        'ab_w_in': normal((n_even, d, AB_IN_WIDTH), d ** -0.5),
        'ab_gate_b': ab_gate_b,
        'na_q_norm': 1.0 + normal((n_even, NA_HEAD_DIM), 0.05),
        'na_k_norm': 1.0 + normal((n_even, NA_HEAD_DIM), 0.05),
        'na_rpb': normal((n_even, NA_HEADS, 2 * NA_KH - 1, 2 * NA_KW - 1), 0.1),
        'ml_head_norm': 1.0 + normal((n_even, ML_HEADS, ML_HEAD_DIM), 0.05),
        'ab_w_out': normal((n_even, d, d), d ** -0.5),
        'rw_mu': uniform((n_odd, 6, d), 0.0, 1.0),
        'rw_w_rkv': normal((n_odd, 3, d, d), d ** -0.5),
        'rw_w0': uniform((n_odd, 2, d), -6.0, -1.0),
        'rw_w1': normal((n_odd, 2, d, RW_DECAY_LORA), 0.1 * d ** -0.5),
        'rw_w2': normal((n_odd, 2, RW_DECAY_LORA, d), 0.1 * RW_DECAY_LORA ** -0.5),
        'rw_a0': normal((n_odd, 2, d), 0.1),
        'rw_a1': normal((n_odd, 2, d, RW_AAA_LORA), d ** -0.5),
        'rw_a2': normal((n_odd, 2, RW_AAA_LORA, d), 0.5 * RW_AAA_LORA ** -0.5),
        'rw_g1': normal((n_odd, d, RW_GATE_LORA), d ** -0.5),
        'rw_g2': normal((n_odd, RW_GATE_LORA, d), RW_GATE_LORA ** -0.5),
        'rw_k_k': 0.85 + normal((n_odd, d), 0.05),
        'rw_k_a': 1.0 + normal((n_odd, d), 0.05),
        'rw_r_k': normal((n_odd, d), 0.1),
        'rw_lnx_w': 1.0 + normal((n_odd, d), 0.05),
        'rw_lnx_b': normal((n_odd, d), 0.02),
        'rw_w_o': normal((n_odd, d, d), d ** -0.5),
        'mlp_w1': normal((DEPTH, d, D_FF), d ** -0.5),
        'mlp_w2': normal((DEPTH, D_FF, d), D_FF ** -0.5),
    }


def reference(x, c, ctx, c_ctx, ada_w, ada_b, ab_w_in, ab_gate_b, na_q_norm, na_k_norm, na_rpb, ml_head_norm,
              ab_w_out, rw_mu, rw_w_rkv, rw_w0, rw_w1, rw_w2, rw_a0, rw_a1, rw_a2, rw_g1, rw_g2, rw_k_k, rw_k_a,
              rw_r_k, rw_lnx_w, rw_lnx_b, rw_w_o, mlp_w1, mlp_w2):
    ang_r, ang_c = axial_rope_angles(x.shape[1], ML_HEAD_DIM)
    for layer in range(DEPTH):
        need_ctx = layer < DEPTH - 1
        j = layer // 2
        sh_a, sc_a, gt_a, sh_m, sc_m, gt_m = adaln(c[:, None, :], ada_w[layer], ada_b[layer])
        csh_a, csc_a, cgt_a, csh_m, csc_m, cgt_m = adaln(c_ctx, ada_w[layer], ada_b[layer])
        h_l = modulate(x, sh_a, sc_a)
        h_c = modulate(ctx, csh_a, csc_a)
        if layer % 2 == 0:
            o_l, o_c = ab_mixer(h_l, h_c, ab_w_in[j], ab_gate_b[j], na_q_norm[j], na_k_norm[j], na_rpb[j],
                                ml_head_norm[j], ab_w_out[j], ang_r, ang_c, need_ctx)
        else:
            o_l, o_c = rwkv_mixer(h_l, h_c, rw_mu[j], rw_w_rkv[j], rw_w0[j], rw_w1[j], rw_w2[j], rw_a0[j],
                                  rw_a1[j], rw_a2[j], rw_g1[j], rw_g2[j], rw_k_k[j], rw_k_a[j], rw_r_k[j],
                                  rw_lnx_w[j], rw_lnx_b[j], rw_w_o[j], need_ctx)
        x = x + gt_a * o_l
        x = x + gt_m * squared_relu_mlp(modulate(x, sh_m, sc_m), mlp_w1[layer], mlp_w2[layer])
        if need_ctx:
            ctx = ctx + cgt_a * o_c
            ctx = ctx + cgt_m * squared_relu_mlp(modulate(ctx, csh_m, csc_m), mlp_w1[layer], mlp_w2[layer])
    return x
```

```python
import functools

import numpy as np
import jax
import jax.numpy as jnp
from jax import lax
from jax.experimental import pallas as pl
from jax.experimental.pallas import tpu as pltpu

F32 = jnp.float32
BF16 = jnp.bfloat16

D_MODEL = 1024
GRID_W = 64
NA_HEAD_DIM = 64
NA_HEADS = 8
NA_WIDTH = 512
NA_KH = 8
NA_KW = 16
ML_HEAD_DIM = 128
ML_HEADS = 4
ML_WIDTH = 512
ML_CHUNK = 128
ML_GATE_CAP = 15.0
RW_HEAD_DIM = 64
RW_HEADS = 16
RW_LORA = 64
RW_GATE_LORA = 160
RW_GN_EPS = 64e-5
D_FF = 4 * D_MODEL
ROPE_BASE = 10000.0
NORM_EPS = 1e-6

LANES = 128
VMEM_LIMIT = 56 * 1024 * 1024
TOKEN_TILE = 256
NA_ROWS = 4
NA_KEY_ROWS = NA_ROWS + NA_KH - 1
RW_CHUNK = 64
NEG_BIG = -1e30
RW_DECAY_SCALE = float(np.exp(-0.5))


def _cparams(sem):
    return pltpu.CompilerParams(dimension_semantics=sem, vmem_limit_bytes=VMEM_LIMIT)


def _const_spec(shape):
    nd = len(shape)
    return pl.BlockSpec(shape, lambda *_: (0,) * nd)


def _dot(a, b):
    return jnp.dot(a, b, preferred_element_type=F32)


def _dot_nt(a, b):
    return lax.dot_general(a, b, (((1,), (1,)), ((), ())), preferred_element_type=F32)


def _dot_tn(a, b):
    return lax.dot_general(a, b, (((0,), (0,)), ((), ())), preferred_element_type=F32)


def _split3(x):
    hi = x.astype(BF16)
    r1 = x - hi.astype(F32)
    mid = r1.astype(BF16)
    lo = (r1 - mid.astype(F32)).astype(BF16)
    return hi, mid, lo


def _dot_exact_rhs(a_bf16, x):
    hi, mid, lo = _split3(x)
    return _dot(a_bf16, hi) + _dot(a_bf16, mid) + _dot(a_bf16, lo)


def _dot_exact_lhs(x, a_bf16):
    hi, mid, lo = _split3(x)
    return _dot(hi, a_bf16) + _dot(mid, a_bf16) + _dot(lo, a_bf16)


def _dot_f32(a, b):
    ah, am, al = _split3(a)
    bh, bm, bl = _split3(b)
    return (_dot(ah, bh) + _dot(ah, bm) + _dot(am, bh)
            + _dot(ah, bl) + _dot(am, bm) + _dot(al, bh))


def _sigmoid(x):
    return 1.0 / (1.0 + jnp.exp(-x))


def _log_sigmoid(x):
    return jnp.minimum(x, 0.0) - jnp.log(1.0 + jnp.exp(-jnp.abs(x)))


def _rms(x):
    return x * lax.rsqrt(jnp.mean(x * x, axis=-1, keepdims=True) + NORM_EPS)


def _modulate(x, shift, scale):
    return _rms(x) * (1.0 + scale) + shift


def _iota(shape, dim):
    return lax.broadcasted_iota(jnp.int32, shape, dim)


def _adaln_kernel(c_ref, w_ref, b_ref, o_ref):
    c = c_ref[...]
    s = c * _sigmoid(c)
    o_ref[0] = _dot_f32(s, w_ref[0]) + b_ref[0]


def _adaln(cond, ada_w, ada_b):
    depth = ada_w.shape[0]
    tn = 512
    return pl.pallas_call(
        _adaln_kernel,
        grid=(depth, 6 * D_MODEL // tn),
        in_specs=[
            pl.BlockSpec((16, D_MODEL), lambda l, j: (0, 0)),
            pl.BlockSpec((1, D_MODEL, tn), lambda l, j: (l, 0, j)),
            pl.BlockSpec((1, 1, tn), lambda l, j: (l, 0, j)),
        ],
        out_specs=pl.BlockSpec((1, 16, tn), lambda l, j: (l, 0, j)),
        out_shape=jax.ShapeDtypeStruct((depth, 16, 6 * D_MODEL), F32),
        compiler_params=_cparams(("parallel", "parallel")),
        name="adaln",
    )(cond, ada_w, ada_b.reshape(depth, 1, 6 * D_MODEL))


_C_QA, _C_KA, _C_VA, _C_QB, _C_KB, _C_VB, _C_OB, _C_QBP, _C_KBP, _C_G = (
    0, 512, 1024, 1536, 2048, 2560, 3072, 3584, 4096, 4608)
_AB_COLS = 4608 + LANES


def _ab_proj_kernel(x_ref, sh_ref, sc_ref, w_ref, wgt_ref, bd_ref, qn_ref, kn_ref, cos_ref, sin_ref,
                    gb_ref, gbt_ref,
                    qa_ref, ka_ref, va_ref, qb_ref, kb_ref, vb_ref, ob_ref, g_ref, gt_ref, *, rope):
    h = _modulate(x_ref[0], sh_ref[0], sc_ref[0]).astype(BF16)

    def proj(c0, n=512):
        return _dot(h, w_ref[:, c0:c0 + n])

    bd = bd_ref[...]

    def head_norm(y, gain):
        ms = _dot((y * y).astype(BF16), bd)
        return y * lax.rsqrt(ms + NORM_EPS) * gain

    qa_ref[0] = (head_norm(proj(_C_QA), qn_ref[...]) * (NA_HEAD_DIM ** -0.5)).astype(BF16)
    ka_ref[0] = head_norm(proj(_C_KA), kn_ref[...]).astype(BF16)
    va_ref[0] = proj(_C_VA).astype(BF16)
    qb = proj(_C_QB)
    kb = proj(_C_KB)
    if rope:
        cos = jnp.concatenate([cos_ref[...]] * ML_HEADS, axis=1)
        sin = jnp.concatenate([sin_ref[...]] * ML_HEADS, axis=1)
        qb = qb * cos + proj(_C_QBP) * sin
        kb = kb * cos + proj(_C_KBP) * sin
    qb_ref[0] = qb.astype(BF16)
    kb_ref[0] = (kb * (ML_HEAD_DIM ** -0.5)).astype(BF16)
    vb_ref[0] = proj(_C_VB).astype(BF16)
    ob_ref[0] = proj(_C_OB)
    g_ref[0] = proj(_C_G, LANES)[:, :4 * ML_HEADS] + gb_ref[...]
    gt_ref[0] = _dot_nt(wgt_ref[...], h) + gbt_ref[...]


def _ab_project(x, shift, scale, prm, rope):
    n_b, n_t, _ = x.shape
    tm = TOKEN_TILE
    tok = lambda w, dt: jax.ShapeDtypeStruct((n_b, n_t, w), dt)
    tok_spec = lambda w: pl.BlockSpec((1, tm, w), lambda b, i: (b, i, 0))
    mod_spec = pl.BlockSpec((1, 1, D_MODEL), lambda b, i: (b, 0, 0))
    return pl.pallas_call(
        functools.partial(_ab_proj_kernel, rope=rope),
        grid=(n_b, n_t // tm),
        in_specs=[
            tok_spec(D_MODEL), mod_spec, mod_spec,
            _const_spec((D_MODEL, _AB_COLS)), _const_spec((4 * ML_HEADS, D_MODEL)),
            _const_spec((NA_WIDTH, NA_WIDTH)), _const_spec((1, NA_WIDTH)), _const_spec((1, NA_WIDTH)),
            pl.BlockSpec((tm, LANES), lambda b, i: (i, 0)), pl.BlockSpec((tm, LANES), lambda b, i: (i, 0)),
            _const_spec((1, 4 * ML_HEADS)), _const_spec((4 * ML_HEADS, 1)),
        ],
        out_specs=[tok_spec(512)] * 7 + [
            tok_spec(4 * ML_HEADS),
            pl.BlockSpec((1, 4 * ML_HEADS, tm), lambda b, i: (b, 0, i)),
        ],
        out_shape=[tok(512, BF16)] * 6 + [tok(512, F32), tok(4 * ML_HEADS, F32),
                                           jax.ShapeDtypeStruct((n_b, 4 * ML_HEADS, n_t), F32)],
        compiler_params=_cparams(("parallel", "parallel")),
        name="ab_proj_rope" if rope else "ab_proj",
    )(x, shift, scale, prm["w_all"], prm["wg_t"], prm["bd512"], prm["q_norm"], prm["k_norm"],
      prm["cos"][:n_t], prm["sin"][:n_t], prm["gate_b"], prm["gate_b_t"])


def _na_kernel(q_ref, k_ref, v_ref, kc_ref, vc_ref, bias_ref, o_ref):
    bi = pl.program_id(2)
    n_rows = k_ref.shape[1] // GRID_W
    kr0 = jnp.clip(bi * NA_ROWS - NA_KH // 2, 0, n_rows - NA_KEY_ROWS)
    k0 = pl.multiple_of(kr0 * GRID_W, GRID_W)
    n_keys = NA_KEY_ROWS * GRID_W
    q = q_ref[0]
    kw = k_ref[0, pl.ds(k0, n_keys), :]
    vw = v_ref[0, pl.ds(k0, n_keys), :]
    kc = kc_ref[0]
    vc = vc_ref[0]
    lane = _iota((1, LANES), 1)
    outs = []
    for hh in range(2):
        in_head = (lane >= hh * NA_HEAD_DIM) & (lane < (hh + 1) * NA_HEAD_DIM)
        qm = jnp.where(in_head, q, jnp.zeros_like(q))
        s_nb = _dot_nt(qm, kw) + bias_ref[0, hh].astype(F32)
        s_cx = _dot_nt(qm, kc)
        m = jnp.maximum(jnp.max(s_nb, axis=-1, keepdims=True), jnp.max(s_cx, axis=-1, keepdims=True))
        p_nb = jnp.exp(s_nb - m)
        p_cx = jnp.exp(s_cx - m)
        den = jnp.sum(p_nb, axis=-1, keepdims=True) + jnp.sum(p_cx, axis=-1, keepdims=True)
        o = _dot(p_nb.astype(BF16), vw) + _dot(p_cx.astype(BF16), vc)
        outs.append(o / den)
    o_ref[0] = jnp.where(lane < NA_HEAD_DIM, outs[0], outs[1]).astype(o_ref.dtype)


def _na_attention(qa, ka, va, ka_c, va_c, bias):
    n_b, n_t, _ = qa.shape
    n_ctx = ka_c.shape[1]
    rq = NA_ROWS * GRID_W
    n_blk = n_t // rq
    nk = NA_KEY_ROWS * GRID_W

    def case(bi):
        return jnp.where(bi == 0, 0, jnp.where(bi == n_blk - 1, 2, 1))

    return pl.pallas_call(
        _na_kernel,
        grid=(n_b, NA_HEADS // 2, n_blk),
        in_specs=[
            pl.BlockSpec((1, rq, LANES), lambda b, hp, bi: (b, bi, hp)),
            pl.BlockSpec((1, n_t, LANES), lambda b, hp, bi: (b, 0, hp)),
            pl.BlockSpec((1, n_t, LANES), lambda b, hp, bi: (b, 0, hp)),
            pl.BlockSpec((1, n_ctx, LANES), lambda b, hp, bi: (b, 0, hp)),
            pl.BlockSpec((1, n_ctx, LANES), lambda b, hp, bi: (b, 0, hp)),
            pl.BlockSpec((1, 2, rq, nk), lambda b, hp, bi: (case(bi), hp, 0, 0)),
        ],
        out_specs=pl.BlockSpec((1, rq, LANES), lambda b, hp, bi: (b, bi, hp)),
        out_shape=jax.ShapeDtypeStruct((n_b, n_t, NA_WIDTH), BF16),
        compiler_params=_cparams(("parallel", "parallel", "arbitrary")),
        name="na_attention",
    )(qa, ka, va, ka_c, va_c, bias)


def _ctx_attn_kernel(q_ref, k_ref, v_ref, o_ref):
    q = q_ref[0]
    k = k_ref[0]
    v = v_ref[0]
    lane = _iota((1, LANES), 1)
    outs = []
    for hh in range(2):
        in_head = (lane >= hh * NA_HEAD_DIM) & (lane < (hh + 1) * NA_HEAD_DIM)
        qm = jnp.where(in_head, q, jnp.zeros_like(q))
        s = _dot_nt(qm, k)
        p = jnp.exp(s - jnp.max(s, axis=-1, keepdims=True))
        outs.append(_dot(p.astype(BF16), v) / jnp.sum(p, axis=-1, keepdims=True))
    o_ref[0] = jnp.where(lane < NA_HEAD_DIM, outs[0], outs[1]).astype(o_ref.dtype)


def _ctx_attention(qa, ka, va):
    n_b, n_ctx, _ = qa.shape
    spec = pl.BlockSpec((1, n_ctx, LANES), lambda b, hp: (b, 0, hp))
    return pl.pallas_call(
        _ctx_attn_kernel,
        grid=(n_b, NA_HEADS // 2),
        in_specs=[spec, spec, spec],
        out_specs=spec,
        out_shape=jax.ShapeDtypeStruct((n_b, n_ctx, NA_WIDTH), BF16),
        compiler_params=_cparams(("parallel", "parallel")),
        name="ctx_attention",
    )(qa, ka, va)


def _na_bias_table(rpb, n_rows):
    n_blk = n_rows // NA_ROWS
    cq = np.arange(GRID_W)[:, None]
    ck = np.arange(GRID_W)[None, :]
    cs = np.clip(cq - NA_KW // 2, 0, GRID_W - NA_KW)
    col_ok = (ck >= cs) & (ck < cs + NA_KW)
    dc = np.clip(ck - cq + NA_KW - 1, 0, 2 * NA_KW - 2)
    toep = jnp.where(col_ok[None, None], rpb[:, :, dc], NEG_BIG)
    toep = jnp.concatenate([toep, jnp.full_like(toep[:, :1], NEG_BIG)], axis=1)
    tables = []
    for bi in (0, 1, n_blk - 1):
        r0 = bi * NA_ROWS
        kr0 = int(np.clip(r0 - NA_KH // 2, 0, n_rows - NA_KEY_ROWS))
        rq = r0 + np.arange(NA_ROWS)[:, None]
        rk = kr0 + np.arange(NA_KEY_ROWS)[None, :]
        rs = np.clip(rq - NA_KH // 2, 0, n_rows - NA_KH)
        row_ok = (rk >= rs) & (rk < rs + NA_KH)
        dr = np.where(row_ok, rk - rq + NA_KH - 1, 2 * NA_KH - 1)
        t = toep[:, dr]
        t = jnp.transpose(t, (0, 1, 3, 2, 4)).reshape(rpb.shape[0], NA_ROWS * GRID_W, NA_KEY_ROWS * GRID_W)
        tables.append(t)
    return jnp.stack(tables).astype(BF16)


def _mlstm_chunk(q, k, v, gcol, grow, state, tri_c, tri_r, rev):
    c_st, n_st, m_st = state
    n_l = q.shape[0]
    cap = lambda g: ML_GATE_CAP * jnp.tanh(g / ML_GATE_CAP)
    gcol = cap(gcol)
    grow = cap(grow)
    i_col = gcol[:, 0:1]
    f_col = _log_sigmoid(gcol[:, 1:2])
    i_row = grow[0:1, :]
    f_row = _log_sigmoid(grow[1:2, :])
    b_col = _dot_exact_rhs(tri_c, jnp.broadcast_to(f_col, (n_l, n_l)))
    b_row = _dot_exact_lhs(jnp.broadcast_to(f_row, (n_l, n_l)), tri_r)
    ti = _iota((n_l, n_l), 0)
    si = _iota((n_l, n_l), 1)
    mask = (si >= ti) if rev else (si <= ti)
    last = 0 if rev else n_l - 1
    log_d = jnp.where(mask, b_col - b_row + i_row, -jnp.inf)
    b_t = b_col[:, 0:1]
    log_inter = b_t + m_st
    m_t = jnp.maximum(log_inter, jnp.max(log_d, axis=-1, keepdims=True))
    is_last = _iota((n_l, 1), 0) == last
    m_new = jnp.sum(jnp.where(is_last, m_t, 0.0), axis=0, keepdims=True)
    b_last = jnp.sum(jnp.where(is_last, b_t, 0.0), axis=0, keepdims=True)
    w_s = jnp.exp(b_last - b_t + i_col - m_new)
    decay = jnp.exp(b_last + m_st - m_new)
    d_mat = jnp.exp(log_d - m_t)
    w_inter = jnp.exp(log_inter - m_t)
    s = _dot_nt(q, k) * d_mat
    num = _dot(s.astype(BF16), v) + w_inter * _dot(q, c_st.astype(BF16))
    den = jnp.sum(s, axis=-1, keepdims=True) + w_inter * jnp.sum(q.astype(F32) * n_st, axis=-1, keepdims=True)
    h = num / jnp.maximum(jnp.abs(den), jnp.exp(-m_t))
    kw = k.astype(F32) * w_s
    c_new = decay * c_st + _dot_tn(kw.astype(BF16), v)
    n_new = decay * n_st + jnp.sum(kw, axis=0, keepdims=True)
    return h, (c_new, n_new, m_new)


def _mlstm_kernel(ql_ref, kl_ref, vl_ref, gl_ref, glt_ref, qc_ref, kc_ref, vc_ref, gc_ref, gct_ref,
                  hlf_ref, hlb_ref, hcf_ref, hcb_ref):
    n_l = ML_CHUNK
    ti = _iota((n_l, n_l), 0)
    si = _iota((n_l, n_l), 1)
    lower = jnp.where(si <= ti, 1.0, 0.0).astype(BF16)
    upper = jnp.where(si >= ti, 1.0, 0.0).astype(BF16)

    def run(refs, outs, c, state, rev):
        q_ref, k_ref, v_ref, g_ref, gt_ref = refs
        t0 = pl.multiple_of(c * n_l, n_l)
        gcol = g_ref[0, 0, pl.ds(t0, n_l), :]
        grow = gt_ref[0, 0, :, pl.ds(t0, n_l)]
        d = 2 if rev else 0
        h, state = _mlstm_chunk(q_ref[0, pl.ds(t0, n_l), :], k_ref[0, pl.ds(t0, n_l), :],
                                v_ref[0, pl.ds(t0, n_l), :], gcol[:, d:d + 2], grow[d:d + 2, :], state,
                                upper if rev else lower, lower if rev else upper, rev)
        outs[1 if rev else 0][0, pl.ds(t0, n_l), :] = h
        return state

    zero = (jnp.zeros((ML_HEAD_DIM, ML_HEAD_DIM), F32), jnp.zeros((1, ML_HEAD_DIM), F32), jnp.zeros((1, 1), F32))
    ctx_refs = (qc_ref, kc_ref, vc_ref, gc_ref, gct_ref)
    lat_refs = (ql_ref, kl_ref, vl_ref, gl_ref, glt_ref)
    n_cc = qc_ref.shape[1] // n_l
    n_lc = ql_ref.shape[1] // n_l

    def ctx_body(j, carry):
        st_f, st_b = carry
        st_f = run(ctx_refs, (hcf_ref, hcb_ref), j, st_f, False)
        st_b = run(ctx_refs, (hcf_ref, hcb_ref), n_cc - 1 - j, st_b, True)
        return st_f, st_b

    def lat_body(j, carry):
        st_f, st_b = carry
        st_f = run(lat_refs, (hlf_ref, hlb_ref), j, st_f, False)
        st_b = run(lat_refs, (hlf_ref, hlb_ref), n_lc - 1 - j, st_b, True)
        return st_f, st_b

    carry = lax.fori_loop(0, n_cc, ctx_body, (zero, zero))
    lax.fori_loop(0, n_lc, lat_body, carry)


def _mlstm(ql, kl, vl, gl, glt, qc, kc, vc, gc, gct):
    n_b, n_t, _ = ql.shape
    n_ctx = qc.shape[1]

    def seq(n):
        return pl.BlockSpec((1, n, LANES), lambda b, h: (b, 0, h))

    def gate(n):
        return pl.BlockSpec((1, 1, n, 4), lambda b, h: (b, h, 0, 0))

    def gate_t(n):
        return pl.BlockSpec((1, 1, 4, n), lambda b, h: (b, h, 0, 0))

    return pl.pallas_call(
        _mlstm_kernel,
        grid=(n_b, ML_HEADS),
        in_specs=[seq(n_t), seq(n_t), seq(n_t), gate(n_t), gate_t(n_t),
                  seq(n_ctx), seq(n_ctx), seq(n_ctx), gate(n_ctx), gate_t(n_ctx)],
        out_specs=[seq(n_t), seq(n_t), seq(n_ctx), seq(n_ctx)],
        out_shape=[jax.ShapeDtypeStruct((n_b, n_t, ML_WIDTH), F32)] * 2
        + [jax.ShapeDtypeStruct((n_b, n_ctx, ML_WIDTH), F32)] * 2,
        compiler_params=_cparams(("parallel", "parallel")),
        name="mlstm",
    )(ql, kl, vl, gl, glt, qc, kc, vc, gc, gct)


def _mlp(x1, sh, sc, gt, w1_ref, w2_ref):
    h = _modulate(x1, sh, sc).astype(BF16)
    acc = jnp.zeros_like(x1)
    n_chunk = 1024
    for c in range(D_FF // n_chunk):
        a = jnp.maximum(_dot(h, w1_ref[:, c * n_chunk:(c + 1) * n_chunk]), 0.0)
        acc = acc + _dot((a * a).astype(BF16), w2_ref[c * n_chunk:(c + 1) * n_chunk, :])
    return x1 + gt * acc


def _ab_out_kernel(x_ref, na_ref, hf_ref, hb_ref, ob_ref, ga_ref, shm_ref, scm_ref, gm_ref, hn_ref,
                   wo_ref, w1_ref, w2_ref, o_ref):
    ml = hf_ref[0] + hb_ref[0]
    parts = []
    for hh in range(ML_HEADS):
        parts.append(_rms(ml[:, hh * ML_HEAD_DIM:(hh + 1) * ML_HEAD_DIM]))
    ml = jnp.concatenate(parts, axis=1) * hn_ref[...] * _sigmoid(ob_ref[0])
    o = _dot(na_ref[0], wo_ref[0:NA_WIDTH, :]) + _dot(ml.astype(BF16), wo_ref[NA_WIDTH:, :])
    x1 = x_ref[0] + ga_ref[0] * o
    o_ref[0] = _mlp(x1, shm_ref[0], scm_ref[0], gm_ref[0], w1_ref, w2_ref)


def _ab_out_mlp(x, na, hf, hb, ob, mods, head_norm, w_out, w1, w2):
    n_b, n_t, _ = x.shape
    tm = TOKEN_TILE
    tok = lambda w: pl.BlockSpec((1, tm, w), lambda b, i: (b, i, 0))
    mod = pl.BlockSpec((1, 1, D_MODEL), lambda b, i: (b, 0, 0))
    return pl.pallas_call(
        _ab_out_kernel,
        grid=(n_b, n_t // tm),
        in_specs=[tok(D_MODEL), tok(512), tok(512), tok(512), tok(512), mod, mod, mod, mod,
                  _const_spec((1, ML_WIDTH)), _const_spec((D_MODEL, D_MODEL)),
                  _const_spec((D_MODEL, D_FF)), _const_spec((D_FF, D_MODEL))],
        out_specs=tok(D_MODEL),
        out_shape=jax.ShapeDtypeStruct(x.shape, F32),
        compiler_params=_cparams(("parallel", "parallel")),
        name="ab_out_mlp",
    )(x, na, hf, hb, ob, mods["gate_a"], mods["shift_m"], mods["scale_m"], mods["gate_m"],
      head_norm, w_out, w1, w2)


def _group_sum(x, bd):
    parts = []
    for j in range(D_MODEL // LANES):
        parts.append(_dot(x[:, j * LANES:(j + 1) * LANES].astype(BF16), bd))
    return jnp.concatenate(parts, axis=1)


def _rw_prep_kernel(x_ref, xp_ref, xn_ref, sh_ref, sc_ref, mu_ref, wrkv_ref, w1_ref, w2_ref, w0_ref,
                    a1_ref, a2_ref, a0_ref, g1_ref, g2_ref, kk_ref, ka_ref, bd_ref,
                    r_ref, v_ref, g_ref, a_ref, k0_ref, k1_ref, b0_ref, b1_ref, lw0_ref, lw1_ref):
    i = pl.program_id(1)
    n_i = pl.num_programs(1)
    sh = sh_ref[0]
    sc = sc_ref[0]
    h = _modulate(x_ref[0], sh, sc)
    tm = h.shape[0]
    h_before = _modulate(xp_ref[0, 7:8, :], sh, sc) * jnp.where(i > 0, 1.0, 0.0)
    h_after = _modulate(xn_ref[0, 0:1, :], sh, sc) * jnp.where(i < n_i - 1, 1.0, 0.0)
    row = _iota((tm, 1), 0)
    h_prev = jnp.where(row == 0, h_before, pltpu.roll(h, 1, 0))
    h_next = jnp.where(row == tm - 1, h_after, pltpu.roll(h, tm - 1, 0))
    xx = 0.5 * (h_prev + h_next) - h

    def mix(s):
        return (h + xx * mu_ref[s:s + 1, :]).astype(BF16)

    r = _dot(mix(0), wrkv_ref[0])
    k = _dot(mix(2), wrkv_ref[1])
    v = _dot(mix(3), wrkv_ref[2])
    lane = _iota((1, 2 * RW_LORA), 1)
    hid_w = jnp.tanh(_dot(mix(1), w1_ref[...]))
    hid_a = _dot(mix(4), a1_ref[...])
    g = _dot(_sigmoid(_dot(mix(5), g1_ref[...])).astype(BF16), g2_ref[...])
    kk = k * kk_ref[...]
    ss = _group_sum(kk * kk, bd_ref[...])
    kk = kk * jnp.minimum(lax.rsqrt(ss), 1e12)
    r_ref[0] = r.astype(r_ref.dtype)
    v_ref[0] = v.astype(v_ref.dtype)
    g_ref[0] = g.astype(g_ref.dtype)
    a_ref[0] = (-kk).astype(a_ref.dtype)
    for z, (k_out, b_out, lw_out) in enumerate(((k0_ref, b0_ref, lw0_ref), (k1_ref, b1_ref, lw1_ref))):
        in_dir = (lane >= z * RW_LORA) & (lane < (z + 1) * RW_LORA)
        w_logit = w0_ref[z:z + 1, :] + _dot(jnp.where(in_dir, hid_w, 0.0).astype(BF16), w2_ref[...])
        a = _sigmoid(a0_ref[z:z + 1, :] + _dot(jnp.where(in_dir, hid_a, 0.0).astype(BF16), a2_ref[...]))
        lw_out[0] = -RW_DECAY_SCALE * _sigmoid(w_logit)
        k_out[0] = (k * (1.0 + (a - 1.0) * ka_ref[...])).astype(k_out.dtype)
        b_out[0] = (kk * a).astype(b_out.dtype)


def _rw_prepare(x, shift, scale, prm):
    n_b, n_t, _ = x.shape
    tm = TOKEN_TILE
    tok = pl.BlockSpec((1, tm, D_MODEL), lambda b, i: (b, i, 0))
    mod = pl.BlockSpec((1, 1, D_MODEL), lambda b, i: (b, 0, 0))
    n8 = n_t // 8
    prev_spec = pl.BlockSpec((1, 8, D_MODEL), lambda b, i: (b, jnp.maximum(i * (tm // 8) - 1, 0), 0))
    next_spec = pl.BlockSpec((1, 8, D_MODEL), lambda b, i: (b, jnp.minimum((i + 1) * (tm // 8), n8 - 1), 0))
    out = lambda dt: jax.ShapeDtypeStruct((n_b, n_t, D_MODEL), dt)
    return pl.pallas_call(
        _rw_prep_kernel,
        grid=(n_b, n_t // tm),
        in_specs=[tok, prev_spec, next_spec, mod, mod, _const_spec((8, D_MODEL)),
                  _const_spec((3, D_MODEL, D_MODEL)),
                  _const_spec((D_MODEL, 2 * RW_LORA)), _const_spec((2 * RW_LORA, D_MODEL)), _const_spec((2, D_MODEL)),
                  _const_spec((D_MODEL, 2 * RW_LORA)), _const_spec((2 * RW_LORA, D_MODEL)), _const_spec((2, D_MODEL)),
                  _const_spec((D_MODEL, 256)), _const_spec((256, D_MODEL)),
                  _const_spec((1, D_MODEL)), _const_spec((1, D_MODEL)), _const_spec((LANES, LANES))],
        out_specs=[tok] * 10,
        out_shape=[out(BF16)] * 8 + [out(F32)] * 2,
        compiler_params=_cparams(("parallel", "parallel")),
        name="rw_prepare",
    )(x, x, x, shift, scale, prm["mu"], prm["w_rkv"], prm["w1"], prm["w2"], prm["w0"],
      prm["a1"], prm["a2"], prm["a0"], prm["g1"], prm["g2"], prm["k_k"], prm["k_a"], prm["bd128"])


def _rw_chunk(refs, h_ref, y_ref, stage, rev):
    r_ref, lw_ref, k_ref, v_ref, a_ref, b_ref = refs
    n_l = RW_CHUNK
    lw = lw_ref[0]
    ti = _iota((n_l, n_l), 0)
    si = _iota((n_l, n_l), 1)
    incl = (si >= ti) if rev else (si <= ti)
    cum = _dot_exact_rhs(jnp.where(incl, 1.0, 0.0).astype(BF16), lw)
    tot = jnp.sum(lw, axis=0, keepdims=True)
    e_pos = jnp.exp(cum)
    e_neg = jnp.exp(-cum)
    e_tot = jnp.exp(tot)
    rt_s, at_s, bt_s, kt_s, bh_s, kh_s, etot_s = stage
    etot_s[...] = jnp.broadcast_to(e_tot, etot_s.shape)
    b = b_ref[0].astype(F32)
    k = k_ref[0].astype(F32)
    rt_s[...] = (r_ref[0].astype(F32) * e_pos).astype(BF16)
    at_s[...] = (a_ref[0].astype(F32) * jnp.exp(cum - lw)).astype(BF16)
    bt = b * e_neg
    kt = k * e_neg
    bt_s[...] = bt.astype(BF16)
    kt_s[...] = kt.astype(BF16)
    bh_s[...] = (bt * e_tot).astype(BF16)
    kh_s[...] = (kt * e_tot).astype(BF16)

    lane = _iota((1, LANES), 1)
    lo = lane < RW_HEAD_DIM
    hi = jnp.logical_not(lo)
    ti2 = _iota((2 * n_l, LANES), 0)
    si2 = _iota((2 * n_l, LANES), 1)
    tq = jnp.where(ti2 >= n_l, ti2 - n_l, ti2)
    sk = jnp.where(si2 >= n_l, si2 - n_l, si2)
    strict2 = (sk > tq) if rev else (sk < tq)
    mask_g = strict2 | ((ti2 >= n_l) & (sk == tq))
    eye2 = jnp.where(ti2 == si2, 1.0, 0.0)
    zeros_l = jnp.zeros((n_l, LANES), BF16)
    bd = (_iota((LANES, LANES), 0) < RW_HEAD_DIM) == (_iota((LANES, LANES), 1) < RW_HEAD_DIM)

    def pair(p, carry):
        c0 = pl.multiple_of(p * LANES, LANES)
        sl = pl.ds(c0, LANES)
        rt = rt_s[:, sl]
        at = at_s[:, sl]
        bt_p = bt_s[:, sl]
        kt_p = kt_s[:, sl]
        vv = v_ref[0, :, sl]
        h0 = h_ref[p]
        h0b = h0.astype(BF16)
        zb = jnp.zeros_like(at)
        g0 = _dot_nt(jnp.concatenate([jnp.where(lo, at, zb), jnp.where(lo, rt, zb)], axis=0),
                     jnp.concatenate([bt_p, kt_p], axis=0))
        g1 = _dot_nt(jnp.concatenate([jnp.where(hi, at, zb), jnp.where(hi, rt, zb)], axis=0),
                     jnp.concatenate([kt_p, bt_p], axis=0))
        g0 = jnp.where(mask_g, g0, 0.0)
        g1 = jnp.where(mask_g, g1, 0.0)
        a_blk = jnp.concatenate([jnp.where(lo, g0[:n_l], 0.0), jnp.where(hi, g1[:n_l], 0.0)], axis=0)
        t_inv = eye2 + a_blk
        a_pow = a_blk
        for _ in range(5):
            ab = a_pow.astype(BF16)
            a_pow = _dot(ab, ab)
            t_inv = t_inv + _dot(a_pow.astype(BF16), t_inv.astype(BF16))
        zv = jnp.concatenate([zeros_l, vv], axis=0)
        vz = jnp.concatenate([vv, zeros_l], axis=0)
        rhs0 = _dot_nt(jnp.where(lo, at, zb), h0b) + _dot(g0[:n_l].astype(BF16), zv)
        rhs1 = _dot_nt(jnp.where(hi, at, zb), h0b) + _dot(g1[:n_l].astype(BF16), vz)
        rhs = jnp.concatenate([jnp.where(lo, rhs0, 0.0), jnp.where(hi, rhs1, 0.0)], axis=0)
        u_st = _dot(t_inv.astype(BF16), rhs.astype(BF16))
        u = (u_st[:n_l] + u_st[n_l:]).astype(BF16)
        uv = jnp.concatenate([u, vv], axis=0)
        if y_ref is not None:
            vu = jnp.concatenate([vv, u], axis=0)
            y = _dot_nt(rt, h0b)
            y_ref[0, :, sl] = y + jnp.where(lo, _dot(g0[n_l:].astype(BF16), uv), _dot(g1[n_l:].astype(BF16), vu))
        upd = _dot_tn(uv, jnp.concatenate([bh_s[:, sl], kh_s[:, sl]], axis=0))
        h_ref[p] = h0 * etot_s[0:1, sl] + jnp.where(bd, upd, 0.0)
        return carry

    lax.fori_loop(0, D_MODEL // LANES, pair, 0)


def _rw_scan_kernel(*refs, n_cc):
    ctx_f, lat_f, ctx_b, lat_b = refs[0:6], refs[6:12], refs[12:18], refs[18:24]
    yf_ref, yb_ref, hf_ref, hb_ref = refs[24:28]
    stage_f, stage_b = refs[28:35], refs[35:42]
    j = pl.program_id(1)

    @pl.when(j == 0)
    def _():
        hf_ref[...] = jnp.zeros_like(hf_ref)
        hb_ref[...] = jnp.zeros_like(hb_ref)

    @pl.when(j < n_cc)
    def _():
        _rw_chunk(ctx_f, hf_ref, None, stage_f, False)
        _rw_chunk(ctx_b, hb_ref, None, stage_b, True)

    @pl.when(j >= n_cc)
    def _():
        _rw_chunk(lat_f, hf_ref, yf_ref, stage_f, False)
        _rw_chunk(lat_b, hb_ref, yb_ref, stage_b, True)


def _rw_scan(cx, lat):
    n_b, n_t, _ = lat["r"].shape
    n_l = RW_CHUNK
    n_cc = cx["r"].shape[1] // n_l
    n_lc = n_t // n_l
    blk = (1, n_l, D_MODEL)
    ctx_f = pl.BlockSpec(blk, lambda b, j: (b, jnp.minimum(j, n_cc - 1), 0))
    lat_f = pl.BlockSpec(blk, lambda b, j: (b, jnp.maximum(j - n_cc, 0), 0))
    ctx_b = pl.BlockSpec(blk, lambda b, j: (b, jnp.maximum(n_cc - 1 - j, 0), 0))
    lat_b = pl.BlockSpec(blk, lambda b, j: (b, n_lc - 1 - jnp.maximum(j - n_cc, 0), 0))
    fwd = ("r", "lw0", "k0", "v", "a", "b0")
    bwd = ("r", "lw1", "k1", "v", "a", "b1")
    args = ([cx[n] for n in fwd] + [lat[n] for n in fwd] + [cx[n] for n in bwd] + [lat[n] for n in bwd])
    stage = [pltpu.VMEM((n_l, D_MODEL), BF16)] * 6 + [pltpu.VMEM((8, D_MODEL), F32)]
    return pl.pallas_call(
        functools.partial(_rw_scan_kernel, n_cc=n_cc),
        grid=(n_b, n_cc + n_lc),
        in_specs=[ctx_f] * 6 + [lat_f] * 6 + [ctx_b] * 6 + [lat_b] * 6,
        out_specs=[lat_f, lat_b],
        out_shape=[jax.ShapeDtypeStruct((n_b, n_t, D_MODEL), F32)] * 2,
        scratch_shapes=[pltpu.VMEM((D_MODEL // LANES, LANES, LANES), F32)] * 2 + stage + stage,
        compiler_params=_cparams(("parallel", "arbitrary")),
        name="rw_scan",
    )(*args)


def _rw_out_kernel(x_ref, yf_ref, yb_ref, r_ref, k0_ref, k1_ref, v_ref, g_ref, ga_ref, shm_ref, scm_ref, gm_ref,
                   rk_ref, lnw_ref, lnb_ref, bd_ref, wo_ref, w1_ref, w2_ref, o_ref):
    bd = bd_ref[...]
    inv_n = 1.0 / RW_HEAD_DIM
    y = yf_ref[0] + yb_ref[0]
    dlt = y - _group_sum(y, bd) * inv_n
    var = _group_sum(dlt * dlt, bd) * inv_n
    yn = dlt * lax.rsqrt(var + RW_GN_EPS) * lnw_ref[...] + lnb_ref[...]
    r = r_ref[0].astype(F32)
    kd = k0_ref[0].astype(F32) + k1_ref[0].astype(F32)
    coef = _group_sum(r * kd * rk_ref[...], bd)
    o = _dot(((yn + coef * v_ref[0].astype(F32)) * g_ref[0].astype(F32)).astype(BF16), wo_ref[...])
    x1 = x_ref[0] + ga_ref[0] * o
    o_ref[0] = _mlp(x1, shm_ref[0], scm_ref[0], gm_ref[0], w1_ref, w2_ref)


def _rw_out_mlp(x, yf, yb, lat, mods, prm, w1, w2):
    n_b, n_t, _ = x.shape
    tm = TOKEN_TILE
    tok = pl.BlockSpec((1, tm, D_MODEL), lambda b, i: (b, i, 0))
    mod = pl.BlockSpec((1, 1, D_MODEL), lambda b, i: (b, 0, 0))
    vec = _const_spec((1, D_MODEL))
    return pl.pallas_call(
        _rw_out_kernel,
        grid=(n_b, n_t // tm),
        in_specs=[tok] * 8 + [mod] * 4 + [vec, vec, vec, _const_spec((LANES, LANES)),
                                          _const_spec((D_MODEL, D_MODEL)),
                                          _const_spec((D_MODEL, D_FF)), _const_spec((D_FF, D_MODEL))],
        out_specs=tok,
        out_shape=jax.ShapeDtypeStruct(x.shape, F32),
        compiler_params=_cparams(("parallel", "parallel")),
        name="rw_out_mlp",
    )(x, yf, yb, lat["r"], lat["k0"], lat["k1"], lat["v"], lat["g"],
      mods["gate_a"], mods["shift_m"], mods["scale_m"], mods["gate_m"],
      prm["r_k"], prm["lnx_w"], prm["lnx_b"], prm["bd128"], prm["w_o"], w1, w2)


def _ab_params(w_in, gate_b, q_norm, k_norm, n_tokens):
    nw, mw = NA_WIDTH, ML_WIDTH
    qb = w_in[:, 3 * nw:3 * nw + mw]
    kb = w_in[:, 3 * nw + mw:3 * nw + 2 * mw]
    lane = np.arange(mw)
    partner = (lane // 32 ^ 1) * 32 + lane % 32
    head_major = np.array([4 * (j % 4) + j // 4 for j in range(4 * ML_HEADS)])
    wg = w_in[:, 3 * nw + 4 * mw:][:, head_major]
    w_all = jnp.concatenate([w_in[:, :3 * nw + 4 * mw], qb[:, partner], kb[:, partner],
                             wg, jnp.zeros((D_MODEL, LANES - 4 * ML_HEADS), F32)], axis=1).astype(BF16)
    pos = np.arange(n_tokens)
    n_freq = ML_HEAD_DIM // 4
    inv_freq = ROPE_BASE ** (-jnp.arange(n_freq, dtype=F32) / n_freq)
    ang_r = jnp.asarray(pos // GRID_W, F32)[:, None] * inv_freq
    ang_c = jnp.asarray(pos % GRID_W, F32)[:, None] * inv_freq
    cos = jnp.concatenate([jnp.cos(ang_r)] * 2 + [jnp.cos(ang_c)] * 2, axis=1)
    sin = jnp.concatenate([-jnp.sin(ang_r), jnp.sin(ang_r), -jnp.sin(ang_c), jnp.sin(ang_c)], axis=1)
    grp = np.arange(NA_WIDTH) // NA_HEAD_DIM
    bd512 = jnp.asarray((grp[:, None] == grp[None, :]) / NA_HEAD_DIM, BF16)
    gb = gate_b[head_major]
    return dict(w_all=w_all, wg_t=wg.T.astype(BF16), bd512=bd512,
                q_norm=jnp.tile(q_norm, NA_HEADS)[None], k_norm=jnp.tile(k_norm, NA_HEADS)[None],
                cos=cos, sin=sin, gate_b=gb[None], gate_b_t=gb[:, None])


def _rw_params(mu, w_rkv, w0, w1, w2, a0, a1, a2, g1, g2, k_k, k_a, r_k, lnx_w, lnx_b, w_o):
    grp = np.arange(LANES) // RW_HEAD_DIM
    pad_g = 256 - RW_GATE_LORA
    return dict(
        mu=jnp.concatenate([mu, jnp.zeros((2, D_MODEL), F32)], axis=0),
        w_rkv=w_rkv.astype(BF16),
        w1=jnp.concatenate([w1[0], w1[1]], axis=1).astype(BF16),
        w2=jnp.concatenate([w2[0], w2[1]], axis=0).astype(BF16), w0=w0,
        a1=jnp.concatenate([a1[0], a1[1]], axis=1).astype(BF16),
        a2=jnp.concatenate([a2[0], a2[1]], axis=0).astype(BF16), a0=a0,
        g1=jnp.pad(g1, ((0, 0), (0, pad_g))).astype(BF16), g2=jnp.pad(g2, ((0, pad_g), (0, 0))).astype(BF16),
        k_k=k_k[None], k_a=k_a[None], r_k=r_k[None], lnx_w=lnx_w[None], lnx_b=lnx_b[None],
        bd128=jnp.asarray(grp[:, None] == grp[None, :], BF16), w_o=w_o.astype(BF16))


def _mod_dict(m, n_b):
    names = ("shift_a", "scale_a", "gate_a", "shift_m", "scale_m", "gate_m")
    parts = jnp.split(m, 6, axis=-1)
    return {n: jnp.broadcast_to(p[:, None, :], (n_b, 1, D_MODEL)) for n, p in zip(names, parts)}


def _gate_layout(g, g_t):
    n_b, n_t, _ = g.shape
    return (jnp.transpose(g.reshape(n_b, n_t, ML_HEADS, 4), (0, 2, 1, 3)),
            g_t.reshape(n_b, ML_HEADS, 4, n_t))


_PREP_NAMES = ("r", "v", "g", "a", "k0", "k1", "b0", "b1", "lw0", "lw1")


def kernel(x, c, ctx, c_ctx, ada_w, ada_b, ab_w_in, ab_gate_b, na_q_norm, na_k_norm, na_rpb, ml_head_norm, ab_w_out, rw_mu, rw_w_rkv, rw_w0, rw_w1, rw_w2, rw_a0, rw_a1, rw_a2, rw_g1, rw_g2, rw_k_k, rw_k_a, rw_r_k, rw_lnx_w, rw_lnx_b, rw_w_o, mlp_w1, mlp_w2):
    n_b, n_t, _ = x.shape
    assert ada_w.shape[0] == 2, "even (attention / mLSTM) layer followed by an odd (RWKV-7) layer"
    cond = jnp.concatenate([c, c_ctx[None], jnp.zeros((16 - n_b - 1, D_MODEL), F32)], axis=0)
    mods = _adaln(cond, ada_w, ada_b)
    w1 = mlp_w1.astype(BF16)
    w2 = mlp_w2.astype(BF16)

    m_l = _mod_dict(mods[0, :n_b], n_b)
    m_c = _mod_dict(mods[0, n_b:n_b + 1], n_b)
    prm = _ab_params(ab_w_in[0], ab_gate_b[0], na_q_norm[0], na_k_norm[0], n_t)
    qa_l, ka_l, va_l, qb_l, kb_l, vb_l, ob_l, g_l, gt_l = _ab_project(x, m_l["shift_a"], m_l["scale_a"], prm, True)
    qa_c, ka_c, va_c, qb_c, kb_c, vb_c, ob_c, g_c, gt_c = _ab_project(ctx, m_c["shift_a"], m_c["scale_a"], prm, False)
    bias = _na_bias_table(na_rpb[0], n_t // GRID_W)
    na_l = _na_attention(qa_l, ka_l, va_l, ka_c, va_c, bias)
    na_c = _ctx_attention(qa_c, ka_c, va_c)
    hlf, hlb, hcf, hcb = _mlstm(qb_l, kb_l, vb_l, *_gate_layout(g_l, gt_l), qb_c, kb_c, vb_c, *_gate_layout(g_c, gt_c))
    head_norm = ml_head_norm[0].reshape(1, ML_WIDTH)
    w_out = ab_w_out[0].astype(BF16)
    x = _ab_out_mlp(x, na_l, hlf, hlb, ob_l, m_l, head_norm, w_out, w1[0], w2[0])
    ctx = _ab_out_mlp(ctx, na_c, hcf, hcb, ob_c, m_c, head_norm, w_out, w1[0], w2[0])

    m_l = _mod_dict(mods[1, :n_b], n_b)
    m_c = _mod_dict(mods[1, n_b:n_b + 1], n_b)
    rprm = _rw_params(rw_mu[0], rw_w_rkv[0], rw_w0[0], rw_w1[0], rw_w2[0], rw_a0[0], rw_a1[0], rw_a2[0],
                      rw_g1[0], rw_g2[0], rw_k_k[0], rw_k_a[0], rw_r_k[0], rw_lnx_w[0], rw_lnx_b[0], rw_w_o[0])
    lat = dict(zip(_PREP_NAMES, _rw_prepare(x, m_l["shift_a"], m_l["scale_a"], rprm)))
    cxp = dict(zip(_PREP_NAMES, _rw_prepare(ctx, m_c["shift_a"], m_c["scale_a"], rprm)))
    yf, yb = _rw_scan(cxp, lat)
    return _rw_out_mlp(x, yf, yb, lat, m_l, rprm, w1[1], w2[1])
```

```python
import functools

import numpy as np
import jax
import jax.numpy as jnp
from jax import lax
from jax.experimental import pallas as pl
from jax.experimental.pallas import tpu as pltpu

F32 = jnp.float32
BF16 = jnp.bfloat16

D_MODEL = 1024
GRID_W = 64
NA_HEAD_DIM = 64
NA_HEADS = 8
NA_WIDTH = 512
NA_KH = 8
NA_KW = 16
ML_HEAD_DIM = 128
ML_HEADS = 4
ML_WIDTH = 512
ML_CHUNK = 128
ML_GATE_CAP = 15.0
RW_HEAD_DIM = 64
RW_HEADS = 16
RW_LORA = 64
RW_GATE_LORA = 160
RW_GN_EPS = 64e-5
D_FF = 4 * D_MODEL
ROPE_BASE = 10000.0
NORM_EPS = 1e-6

LANES = 128
VMEM_LIMIT = 56 * 1024 * 1024
TOKEN_TILE = 256
NA_ROWS = 4
NA_KEY_ROWS = NA_ROWS + NA_KH - 1
RW_CHUNK = 64
RW_PAIR_UNROLL = 4
NEG_BIG = -1e30
RW_DECAY_SCALE = float(np.exp(-0.5))


def _cparams(sem):
    return pltpu.CompilerParams(dimension_semantics=sem, vmem_limit_bytes=VMEM_LIMIT)


def _const_spec(shape):
    nd = len(shape)
    return pl.BlockSpec(shape, lambda *_: (0,) * nd)


def _dot(a, b):
    return jnp.dot(a, b, preferred_element_type=F32)


def _dot_nt(a, b):
    return lax.dot_general(a, b, (((1,), (1,)), ((), ())), preferred_element_type=F32)


def _dot_tn(a, b):
    return lax.dot_general(a, b, (((0,), (0,)), ((), ())), preferred_element_type=F32)


def _split3(x):
    hi = x.astype(BF16)
    r1 = x - hi.astype(F32)
    mid = r1.astype(BF16)
    lo = (r1 - mid.astype(F32)).astype(BF16)
    return hi, mid, lo


def _dot_exact_rhs(a_bf16, x):
    hi, mid, lo = _split3(x)
    return _dot(a_bf16, hi) + _dot(a_bf16, mid) + _dot(a_bf16, lo)


def _dot_exact_lhs(x, a_bf16):
    hi, mid, lo = _split3(x)
    return _dot(hi, a_bf16) + _dot(mid, a_bf16) + _dot(lo, a_bf16)


def _dot_f32(a, b):
    ah, am, al = _split3(a)
    bh, bm, bl = _split3(b)
    return (_dot(ah, bh) + _dot(ah, bm) + _dot(am, bh)
            + _dot(ah, bl) + _dot(am, bm) + _dot(al, bh))


def _sigmoid(x):
    return 1.0 / (1.0 + jnp.exp(-x))


def _log_sigmoid(x):
    return jnp.minimum(x, 0.0) - jnp.log(1.0 + jnp.exp(-jnp.abs(x)))


def _rms(x):
    return x * lax.rsqrt(jnp.mean(x * x, axis=-1, keepdims=True) + NORM_EPS)


def _modulate(x, shift, scale):
    return _rms(x) * (1.0 + scale) + shift


def _iota(shape, dim):
    return lax.broadcasted_iota(jnp.int32, shape, dim)


def _round_robin(chains):
    live = list(chains)
    while live:
        nxt = []
        for ch in live:
            try:
                next(ch)
                nxt.append(ch)
            except StopIteration:
                pass
        live = nxt


def _adaln_kernel(c_ref, w_ref, b_ref, o_ref):
    c = c_ref[...]
    s = c * _sigmoid(c)
    o_ref[0] = _dot_f32(s, w_ref[0]) + b_ref[0]


def _adaln(cond, ada_w, ada_b):
    depth = ada_w.shape[0]
    tn = 512
    return pl.pallas_call(
        _adaln_kernel,
        grid=(depth, 6 * D_MODEL // tn),
        in_specs=[
            pl.BlockSpec((16, D_MODEL), lambda l, j: (0, 0)),
            pl.BlockSpec((1, D_MODEL, tn), lambda l, j: (l, 0, j)),
            pl.BlockSpec((1, 1, tn), lambda l, j: (l, 0, j)),
        ],
        out_specs=pl.BlockSpec((1, 16, tn), lambda l, j: (l, 0, j)),
        out_shape=jax.ShapeDtypeStruct((depth, 16, 6 * D_MODEL), F32),
        compiler_params=_cparams(("parallel", "parallel")),
        name="adaln",
    )(cond, ada_w, ada_b.reshape(depth, 1, 6 * D_MODEL))


_C_QA, _C_KA, _C_VA, _C_QB, _C_KB, _C_VB, _C_OB, _C_QBP, _C_KBP = (
    0, 512, 1024, 1536, 2048, 2560, 3072, 3584, 4096)
_AB_COLS = 4608


def _ab_proj_kernel(x_ref, sh_ref, sc_ref, w_ref, wgt_ref, bd_ref, qn_ref, kn_ref, cos_ref, sin_ref, gbt_ref,
                    qa_ref, ka_ref, va_ref, qb_ref, kb_ref, vb_ref, ob_ref, gt_ref, *, rope):
    h = _modulate(x_ref[0], sh_ref[0], sc_ref[0]).astype(BF16)

    def proj(c0, n=512):
        return _dot(h, w_ref[:, c0:c0 + n])

    bd = bd_ref[...]

    def head_norm(y, gain):
        ms = _dot((y * y).astype(BF16), bd)
        return y * lax.rsqrt(ms + NORM_EPS) * gain

    qa_ref[0] = (head_norm(proj(_C_QA), qn_ref[...]) * (NA_HEAD_DIM ** -0.5)).astype(BF16)
    ka_ref[0] = head_norm(proj(_C_KA), kn_ref[...]).astype(BF16)
    va_ref[0] = proj(_C_VA).astype(BF16)
    qb = proj(_C_QB)
    kb = proj(_C_KB)
    if rope:
        cos = jnp.concatenate([cos_ref[...]] * ML_HEADS, axis=1)
        sin = jnp.concatenate([sin_ref[...]] * ML_HEADS, axis=1)
        qb = qb * cos + proj(_C_QBP) * sin
        kb = kb * cos + proj(_C_KBP) * sin
    qb_ref[0] = qb.astype(BF16)
    kb_ref[0] = (kb * (ML_HEAD_DIM ** -0.5)).astype(BF16)
    vb_ref[0] = proj(_C_VB).astype(BF16)
    ob_ref[0] = proj(_C_OB)
    gt_ref[0] = _dot_nt(wgt_ref[...], h) + gbt_ref[...]


def _ab_project(x, shift, scale, prm, rope):
    n_b, n_t, _ = x.shape
    tm = TOKEN_TILE
    tok = lambda w, dt: jax.ShapeDtypeStruct((n_b, n_t, w), dt)
    tok_spec = lambda w: pl.BlockSpec((1, tm, w), lambda b, i: (b, i, 0))
    mod_spec = pl.BlockSpec((1, 1, D_MODEL), lambda b, i: (b, 0, 0))
    return pl.pallas_call(
        functools.partial(_ab_proj_kernel, rope=rope),
        grid=(n_b, n_t // tm),
        in_specs=[
            tok_spec(D_MODEL), mod_spec, mod_spec,
            _const_spec((D_MODEL, _AB_COLS)), _const_spec((4 * ML_HEADS, D_MODEL)),
            _const_spec((NA_WIDTH, NA_WIDTH)), _const_spec((1, NA_WIDTH)), _const_spec((1, NA_WIDTH)),
            pl.BlockSpec((tm, LANES), lambda b, i: (i, 0)), pl.BlockSpec((tm, LANES), lambda b, i: (i, 0)),
            _const_spec((4 * ML_HEADS, 1)),
        ],
        out_specs=[tok_spec(512)] * 7 + [pl.BlockSpec((1, 4 * ML_HEADS, tm), lambda b, i: (b, 0, i))],
        out_shape=[tok(512, BF16)] * 6 + [tok(512, F32), jax.ShapeDtypeStruct((n_b, 4 * ML_HEADS, n_t), F32)],
        compiler_params=_cparams(("parallel", "parallel")),
        name="ab_proj_rope" if rope else "ab_proj",
    )(x, shift, scale, prm["w_all"], prm["wg_t"], prm["bd512"], prm["q_norm"], prm["k_norm"],
      prm["cos"][:n_t], prm["sin"][:n_t], prm["gate_b_t"])


def _na_kernel(q_ref, k_ref, v_ref, kc_ref, vc_ref, bias_ref, o_ref):
    bi = pl.program_id(2)
    n_rows = k_ref.shape[1] // GRID_W
    kr0 = jnp.clip(bi * NA_ROWS - NA_KH // 2, 0, n_rows - NA_KEY_ROWS)
    k0 = pl.multiple_of(kr0 * GRID_W, GRID_W)
    n_keys = NA_KEY_ROWS * GRID_W
    q = q_ref[0]
    kw = k_ref[0, pl.ds(k0, n_keys), :]
    vw = v_ref[0, pl.ds(k0, n_keys), :]
    kc = kc_ref[0]
    vc = vc_ref[0]
    lane = _iota((1, LANES), 1)
    outs = []
    for hh in range(2):
        in_head = (lane >= hh * NA_HEAD_DIM) & (lane < (hh + 1) * NA_HEAD_DIM)
        qm = jnp.where(in_head, q, jnp.zeros_like(q))
        s_nb = _dot_nt(qm, kw) + bias_ref[0, hh].astype(F32)
        s_cx = _dot_nt(qm, kc)
        m = jnp.maximum(jnp.max(s_nb, axis=-1, keepdims=True), jnp.max(s_cx, axis=-1, keepdims=True))
        p_nb = jnp.exp(s_nb - m)
        p_cx = jnp.exp(s_cx - m)
        den = jnp.sum(p_nb, axis=-1, keepdims=True) + jnp.sum(p_cx, axis=-1, keepdims=True)
        o = _dot(p_nb.astype(BF16), vw) + _dot(p_cx.astype(BF16), vc)
        outs.append(o / den)
    o_ref[0] = jnp.where(lane < NA_HEAD_DIM, outs[0], outs[1]).astype(o_ref.dtype)


def _na_attention(qa, ka, va, ka_c, va_c, bias):
    n_b, n_t, _ = qa.shape
    n_ctx = ka_c.shape[1]
    rq = NA_ROWS * GRID_W
    n_blk = n_t // rq
    nk = NA_KEY_ROWS * GRID_W

    def case(bi):
        return jnp.where(bi == 0, 0, jnp.where(bi == n_blk - 1, 2, 1))

    return pl.pallas_call(
        _na_kernel,
        grid=(n_b, NA_HEADS // 2, n_blk),
        in_specs=[
            pl.BlockSpec((1, rq, LANES), lambda b, hp, bi: (b, bi, hp)),
            pl.BlockSpec((1, n_t, LANES), lambda b, hp, bi: (b, 0, hp)),
            pl.BlockSpec((1, n_t, LANES), lambda b, hp, bi: (b, 0, hp)),
            pl.BlockSpec((1, n_ctx, LANES), lambda b, hp, bi: (b, 0, hp)),
            pl.BlockSpec((1, n_ctx, LANES), lambda b, hp, bi: (b, 0, hp)),
            pl.BlockSpec((1, 2, rq, nk), lambda b, hp, bi: (case(bi), hp, 0, 0)),
        ],
        out_specs=pl.BlockSpec((1, rq, LANES), lambda b, hp, bi: (b, bi, hp)),
        out_shape=jax.ShapeDtypeStruct((n_b, n_t, NA_WIDTH), BF16),
        compiler_params=_cparams(("parallel", "parallel", "arbitrary")),
        name="na_attention",
    )(qa, ka, va, ka_c, va_c, bias)


def _ctx_attn_kernel(q_ref, k_ref, v_ref, o_ref):
    q = q_ref[0]
    k = k_ref[0]
    v = v_ref[0]
    lane = _iota((1, LANES), 1)
    outs = []
    for hh in range(2):
        in_head = (lane >= hh * NA_HEAD_DIM) & (lane < (hh + 1) * NA_HEAD_DIM)
        qm = jnp.where(in_head, q, jnp.zeros_like(q))
        s = _dot_nt(qm, k)
        p = jnp.exp(s - jnp.max(s, axis=-1, keepdims=True))
        outs.append(_dot(p.astype(BF16), v) / jnp.sum(p, axis=-1, keepdims=True))
    o_ref[0] = jnp.where(lane < NA_HEAD_DIM, outs[0], outs[1]).astype(o_ref.dtype)


def _ctx_attention(qa, ka, va):
    n_b, n_ctx, _ = qa.shape
    spec = pl.BlockSpec((1, n_ctx, LANES), lambda b, hp: (b, 0, hp))
    return pl.pallas_call(
        _ctx_attn_kernel,
        grid=(n_b, NA_HEADS // 2),
        in_specs=[spec, spec, spec],
        out_specs=spec,
        out_shape=jax.ShapeDtypeStruct((n_b, n_ctx, NA_WIDTH), BF16),
        compiler_params=_cparams(("parallel", "parallel")),
        name="ctx_attention",
    )(qa, ka, va)


def _na_bias_table(rpb, n_rows):
    n_blk = n_rows // NA_ROWS
    cq = np.arange(GRID_W)[:, None]
    ck = np.arange(GRID_W)[None, :]
    cs = np.clip(cq - NA_KW // 2, 0, GRID_W - NA_KW)
    col_ok = (ck >= cs) & (ck < cs + NA_KW)
    dc = np.clip(ck - cq + NA_KW - 1, 0, 2 * NA_KW - 2)
    toep = jnp.where(col_ok[None, None], rpb[:, :, dc], NEG_BIG)
    toep = jnp.concatenate([toep, jnp.full_like(toep[:, :1], NEG_BIG)], axis=1)
    tables = []
    for bi in (0, 1, n_blk - 1):
        r0 = bi * NA_ROWS
        kr0 = int(np.clip(r0 - NA_KH // 2, 0, n_rows - NA_KEY_ROWS))
        rq = r0 + np.arange(NA_ROWS)[:, None]
        rk = kr0 + np.arange(NA_KEY_ROWS)[None, :]
        rs = np.clip(rq - NA_KH // 2, 0, n_rows - NA_KH)
        row_ok = (rk >= rs) & (rk < rs + NA_KH)
        dr = np.where(row_ok, rk - rq + NA_KH - 1, 2 * NA_KH - 1)
        t = toep[:, dr]
        t = jnp.transpose(t, (0, 1, 3, 2, 4)).reshape(rpb.shape[0], NA_ROWS * GRID_W, NA_KEY_ROWS * GRID_W)
        tables.append(t)
    return jnp.stack(tables).astype(BF16)


def _mlstm_prologue(grow_ref, gcol_ref, rows_s, cols_s):
    n_l = ML_CHUNK
    ti = _iota((n_l, n_l), 0)
    si = _iota((n_l, n_l), 1)
    lower = jnp.where(si <= ti, 1.0, 0.0).astype(BF16)
    upper = jnp.where(si >= ti, 1.0, 0.0).astype(BF16)
    cap = lambda g: ML_GATE_CAP * jnp.tanh(g / ML_GATE_CAP)
    for hh in range(2):
        for d in range(2):
            ch = 2 * hh + d
            rows_s[ch, 1] = cap(grow_ref[0, hh, 2 * d])
            rows_s[ch, 0] = _dot_exact_lhs(_log_sigmoid(cap(grow_ref[0, hh, 2 * d + 1])), lower if d else upper)
            cols_s[ch, 1] = cap(gcol_ref[0, hh, 2 * d])
            cols_s[ch, 0] = _dot_exact_rhs(upper if d else lower, _log_sigmoid(cap(gcol_ref[0, hh, 2 * d + 1])))


def _mlstm_chain(ch, c, q_ref, k_ref, v_ref, rows_s, cols_s, c_s, n_s, m_s, out_ref):
    n_l = ML_CHUNK
    hh, rev = ch // 2, ch % 2 == 1
    t0 = pl.multiple_of(c * n_l, n_l)
    hs = pl.ds(hh * ML_HEAD_DIM, ML_HEAD_DIM)
    q = q_ref[0, pl.ds(t0, n_l), hs]
    k = k_ref[0, pl.ds(t0, n_l), hs]
    v = v_ref[0, pl.ds(t0, n_l), hs]
    c_st = c_s[ch]
    n_st = n_s[ch, 0:1, :]
    m_st = m_s[ch, 0:1, 0:1]
    b_row = rows_s[ch, 0, pl.ds(c, 1), :]
    i_row = rows_s[ch, 1, pl.ds(c, 1), :]
    at_c = _iota((1, LANES), 1) == c
    b_t = jnp.sum(jnp.where(at_c, cols_s[ch, 0], 0.0), axis=1, keepdims=True)
    i_col = jnp.sum(jnp.where(at_c, cols_s[ch, 1], 0.0), axis=1, keepdims=True)
    ti = _iota((n_l, n_l), 0)
    si = _iota((n_l, n_l), 1)
    mask = (si >= ti) if rev else (si <= ti)
    last = 0 if rev else n_l - 1
    log_d = jnp.where(mask, b_t - b_row + i_row, -jnp.inf)
    log_inter = b_t + m_st
    m_t = jnp.maximum(log_inter, jnp.max(log_d, axis=-1, keepdims=True))
    is_last = _iota((n_l, 1), 0) == last
    m_new = jnp.sum(jnp.where(is_last, m_t, 0.0), axis=0, keepdims=True)
    b_last = jnp.sum(jnp.where(is_last, b_t, 0.0), axis=0, keepdims=True)
    w_s = jnp.exp(b_last - b_t + i_col - m_new)
    decay = jnp.exp(b_last + m_st - m_new)
    d_mat = jnp.exp(log_d - m_t)
    w_inter = jnp.exp(log_inter - m_t)
    kw = k.astype(F32) * w_s
    s = _dot_nt(q, k)
    qc = _dot(q, c_st.astype(BF16))
    upd = _dot_tn(kw.astype(BF16), v)
    yield
    s = s * d_mat
    sv = _dot(s.astype(BF16), v)
    c_s[ch] = decay * c_st + upd
    n_s[ch] = jnp.broadcast_to(decay * n_st + jnp.sum(kw, axis=0, keepdims=True), n_s.shape[1:])
    m_s[ch] = jnp.broadcast_to(m_new, m_s.shape[1:])
    yield
    num = sv + w_inter * qc
    den = jnp.sum(s, axis=-1, keepdims=True) + w_inter * jnp.sum(q.astype(F32) * n_st, axis=-1, keepdims=True)
    out_ref[0, pl.ds(t0, n_l), hs] = num / jnp.maximum(jnp.abs(den), jnp.exp(-m_t))


def _mlstm_kernel(ql_ref, kl_ref, vl_ref, glr_ref, glc_ref, qc_ref, kc_ref, vc_ref, gcr_ref, gcc_ref,
                  hlf_ref, hlb_ref, hcf_ref, hcb_ref,
                  rows_l, cols_l, rows_c, cols_c, c_s, n_s, m_s):
    n_l = ML_CHUNK
    _mlstm_prologue(gcr_ref, gcc_ref, rows_c, cols_c)
    _mlstm_prologue(glr_ref, glc_ref, rows_l, cols_l)
    c_s[...] = jnp.zeros_like(c_s)
    n_s[...] = jnp.zeros_like(n_s)
    m_s[...] = jnp.zeros_like(m_s)

    def sweep(q_ref, k_ref, v_ref, rows_s, cols_s, outs):
        n_c = q_ref.shape[1] // n_l

        def body(j, carry):
            _round_robin([_mlstm_chain(ch, (n_c - 1 - j) if ch % 2 else j, q_ref, k_ref, v_ref, rows_s, cols_s,
                                       c_s, n_s, m_s, outs[ch % 2]) for ch in range(4)])
            return carry

        lax.fori_loop(0, n_c, body, 0)

    sweep(qc_ref, kc_ref, vc_ref, rows_c, cols_c, (hcf_ref, hcb_ref))
    sweep(ql_ref, kl_ref, vl_ref, rows_l, cols_l, (hlf_ref, hlb_ref))


def _mlstm(ql, kl, vl, gl_t, qc, kc, vc, gc_t):
    n_b, n_t, _ = ql.shape
    n_ctx = qc.shape[1]
    n_l = ML_CHUNK

    def gate_layouts(g_t, n):
        n_c = n // n_l
        rows = g_t.reshape(n_b, ML_HEADS, 4, n_c, n_l)
        cols = jnp.swapaxes(rows, 3, 4)
        pad_r = (-n_c) % 8
        return (jnp.pad(rows, ((0, 0),) * 3 + ((0, pad_r), (0, 0))),
                jnp.pad(cols, ((0, 0),) * 4 + ((0, LANES - n_c),)))

    glr, glc = gate_layouts(gl_t, n_t)
    gcr, gcc = gate_layouts(gc_t, n_ctx)
    wide = 2 * ML_HEAD_DIM

    def seq(n):
        return pl.BlockSpec((1, n, wide), lambda b, hp: (b, 0, hp))

    def gspec(a):
        return pl.BlockSpec((1, 2) + a.shape[2:], lambda b, hp: (b, hp, 0, 0, 0))

    vm = lambda *s: pltpu.VMEM(s, F32)
    return pl.pallas_call(
        _mlstm_kernel,
        grid=(n_b, ML_HEADS // 2),
        in_specs=[seq(n_t), seq(n_t), seq(n_t), gspec(glr), gspec(glc),
                  seq(n_ctx), seq(n_ctx), seq(n_ctx), gspec(gcr), gspec(gcc)],
        out_specs=[seq(n_t), seq(n_t), seq(n_ctx), seq(n_ctx)],
        out_shape=[jax.ShapeDtypeStruct((n_b, n_t, ML_WIDTH), F32)] * 2
        + [jax.ShapeDtypeStruct((n_b, n_ctx, ML_WIDTH), F32)] * 2,
        scratch_shapes=[vm(4, 2, glr.shape[3], n_l), vm(4, 2, n_l, LANES),
                        vm(4, 2, gcr.shape[3], n_l), vm(4, 2, n_l, LANES),
                        vm(4, ML_HEAD_DIM, ML_HEAD_DIM), vm(4, 8, ML_HEAD_DIM), vm(4, 8, LANES)],
        compiler_params=_cparams(("parallel", "parallel")),
        name="mlstm",
    )(ql, kl, vl, glr, glc, qc, kc, vc, gcr, gcc)


def _mlp(x1, sh, sc, gt, w1_ref, w2_ref):
    h = _modulate(x1, sh, sc).astype(BF16)
    acc = jnp.zeros_like(x1)
    n_chunk = 1024
    for c in range(D_FF // n_chunk):
        a = jnp.maximum(_dot(h, w1_ref[:, c * n_chunk:(c + 1) * n_chunk]), 0.0)
        acc = acc + _dot((a * a).astype(BF16), w2_ref[c * n_chunk:(c + 1) * n_chunk, :])
    return x1 + gt * acc


def _ab_out_kernel(x_ref, na_ref, hf_ref, hb_ref, ob_ref, ga_ref, shm_ref, scm_ref, gm_ref, hn_ref,
                   wo_ref, w1_ref, w2_ref, o_ref):
    ml = hf_ref[0] + hb_ref[0]
    parts = []
    for hh in range(ML_HEADS):
        parts.append(_rms(ml[:, hh * ML_HEAD_DIM:(hh + 1) * ML_HEAD_DIM]))
    ml = jnp.concatenate(parts, axis=1) * hn_ref[...] * _sigmoid(ob_ref[0])
    o = _dot(na_ref[0], wo_ref[0:NA_WIDTH, :]) + _dot(ml.astype(BF16), wo_ref[NA_WIDTH:, :])
    x1 = x_ref[0] + ga_ref[0] * o
    o_ref[0] = _mlp(x1, shm_ref[0], scm_ref[0], gm_ref[0], w1_ref, w2_ref)


def _ab_out_mlp(x, na, hf, hb, ob, mods, head_norm, w_out, w1, w2):
    n_b, n_t, _ = x.shape
    tm = TOKEN_TILE
    tok = lambda w: pl.BlockSpec((1, tm, w), lambda b, i: (b, i, 0))
    mod = pl.BlockSpec((1, 1, D_MODEL), lambda b, i: (b, 0, 0))
    return pl.pallas_call(
        _ab_out_kernel,
        grid=(n_b, n_t // tm),
        in_specs=[tok(D_MODEL), tok(512), tok(512), tok(512), tok(512), mod, mod, mod, mod,
                  _const_spec((1, ML_WIDTH)), _const_spec((D_MODEL, D_MODEL)),
                  _const_spec((D_MODEL, D_FF)), _const_spec((D_FF, D_MODEL))],
        out_specs=tok(D_MODEL),
        out_shape=jax.ShapeDtypeStruct(x.shape, F32),
        compiler_params=_cparams(("parallel", "parallel")),
        name="ab_out_mlp",
    )(x, na, hf, hb, ob, mods["gate_a"], mods["shift_m"], mods["scale_m"], mods["gate_m"],
      head_norm, w_out, w1, w2)


def _group_sum(x, bd):
    parts = []
    for j in range(D_MODEL // LANES):
        parts.append(_dot(x[:, j * LANES:(j + 1) * LANES].astype(BF16), bd))
    return jnp.concatenate(parts, axis=1)


def _rw_prep_kernel(x_ref, xp_ref, xn_ref, sh_ref, sc_ref, mu_ref, wrkv_ref, w1_ref, w2_ref, w0_ref,
                    a1_ref, a2_ref, a0_ref, g1_ref, g2_ref, kk_ref, ka_ref, bd_ref,
                    r_ref, v_ref, g_ref, a_ref, k0_ref, k1_ref, b0_ref, b1_ref, lw0_ref, lw1_ref):
    i = pl.program_id(1)
    n_i = pl.num_programs(1)
    sh = sh_ref[0]
    sc = sc_ref[0]
    h = _modulate(x_ref[0], sh, sc)
    tm = h.shape[0]
    h_before = _modulate(xp_ref[0, 7:8, :], sh, sc) * jnp.where(i > 0, 1.0, 0.0)
    h_after = _modulate(xn_ref[0, 0:1, :], sh, sc) * jnp.where(i < n_i - 1, 1.0, 0.0)
    row = _iota((tm, 1), 0)
    h_prev = jnp.where(row == 0, h_before, pltpu.roll(h, 1, 0))
    h_next = jnp.where(row == tm - 1, h_after, pltpu.roll(h, tm - 1, 0))
    xx = 0.5 * (h_prev + h_next) - h

    def mix(s):
        return (h + xx * mu_ref[s:s + 1, :]).astype(BF16)

    r = _dot(mix(0), wrkv_ref[0])
    k = _dot(mix(2), wrkv_ref[1])
    v = _dot(mix(3), wrkv_ref[2])
    lane = _iota((1, 2 * RW_LORA), 1)
    hid_w = jnp.tanh(_dot(mix(1), w1_ref[...]))
    hid_a = _dot(mix(4), a1_ref[...])
    g = _dot(_sigmoid(_dot(mix(5), g1_ref[...])).astype(BF16), g2_ref[...])
    kk = k * kk_ref[...]
    ss = _group_sum(kk * kk, bd_ref[...])
    kk = kk * jnp.minimum(lax.rsqrt(ss), 1e12)
    r_ref[0] = r.astype(r_ref.dtype)
    v_ref[0] = v.astype(v_ref.dtype)
    g_ref[0] = g.astype(g_ref.dtype)
    a_ref[0] = (-kk).astype(a_ref.dtype)
    for z, (k_out, b_out, lw_out) in enumerate(((k0_ref, b0_ref, lw0_ref), (k1_ref, b1_ref, lw1_ref))):
        in_dir = (lane >= z * RW_LORA) & (lane < (z + 1) * RW_LORA)
        w_logit = w0_ref[z:z + 1, :] + _dot(jnp.where(in_dir, hid_w, 0.0).astype(BF16), w2_ref[...])
        a = _sigmoid(a0_ref[z:z + 1, :] + _dot(jnp.where(in_dir, hid_a, 0.0).astype(BF16), a2_ref[...]))
        lw_out[0] = -RW_DECAY_SCALE * _sigmoid(w_logit)
        k_out[0] = (k * (1.0 + (a - 1.0) * ka_ref[...])).astype(k_out.dtype)
        b_out[0] = (kk * a).astype(b_out.dtype)


def _rw_prepare(x, shift, scale, prm):
    n_b, n_t, _ = x.shape
    tm = TOKEN_TILE
    tok = pl.BlockSpec((1, tm, D_MODEL), lambda b, i: (b, i, 0))
    mod = pl.BlockSpec((1, 1, D_MODEL), lambda b, i: (b, 0, 0))
    n8 = n_t // 8
    prev_spec = pl.BlockSpec((1, 8, D_MODEL), lambda b, i: (b, jnp.maximum(i * (tm // 8) - 1, 0), 0))
    next_spec = pl.BlockSpec((1, 8, D_MODEL), lambda b, i: (b, jnp.minimum((i + 1) * (tm // 8), n8 - 1), 0))
    out = lambda dt: jax.ShapeDtypeStruct((n_b, n_t, D_MODEL), dt)
    return pl.pallas_call(
        _rw_prep_kernel,
        grid=(n_b, n_t // tm),
        in_specs=[tok, prev_spec, next_spec, mod, mod, _const_spec((8, D_MODEL)),
                  _const_spec((3, D_MODEL, D_MODEL)),
                  _const_spec((D_MODEL, 2 * RW_LORA)), _const_spec((2 * RW_LORA, D_MODEL)), _const_spec((2, D_MODEL)),
                  _const_spec((D_MODEL, 2 * RW_LORA)), _const_spec((2 * RW_LORA, D_MODEL)), _const_spec((2, D_MODEL)),
                  _const_spec((D_MODEL, 256)), _const_spec((256, D_MODEL)),
                  _const_spec((1, D_MODEL)), _const_spec((1, D_MODEL)), _const_spec((LANES, LANES))],
        out_specs=[tok] * 10,
        out_shape=[out(BF16)] * 8 + [out(F32)] * 2,
        compiler_params=_cparams(("parallel", "parallel")),
        name="rw_prepare",
    )(x, x, x, shift, scale, prm["mu"], prm["w_rkv"], prm["w1"], prm["w2"], prm["w0"],
      prm["a1"], prm["a2"], prm["a0"], prm["g1"], prm["g2"], prm["k_k"], prm["k_a"], prm["bd128"])


def _rw_stage(refs, stage, rev):
    r_ref, lw_ref, k_ref, v_ref, a_ref, b_ref = refs
    n_l = RW_CHUNK
    lw = lw_ref[0]
    ti = _iota((n_l, n_l), 0)
    si = _iota((n_l, n_l), 1)
    incl = (si >= ti) if rev else (si <= ti)
    cum = _dot_exact_rhs(jnp.where(incl, 1.0, 0.0).astype(BF16), lw)
    tot = jnp.sum(lw, axis=0, keepdims=True)
    e_pos = jnp.exp(cum)
    e_neg = jnp.exp(-cum)
    e_tot = jnp.exp(tot)
    rt_s, at_s, bt_s, kt_s, bh_s, kh_s, etot_s = stage
    etot_s[...] = jnp.broadcast_to(e_tot, etot_s.shape)
    b = b_ref[0].astype(F32)
    k = k_ref[0].astype(F32)
    rt_s[...] = (r_ref[0].astype(F32) * e_pos).astype(BF16)
    at_s[...] = (a_ref[0].astype(F32) * jnp.exp(cum - lw)).astype(BF16)
    bt = b * e_neg
    kt = k * e_neg
    bt_s[...] = bt.astype(BF16)
    kt_s[...] = kt.astype(BF16)
    bh_s[...] = (bt * e_tot).astype(BF16)
    kh_s[...] = (kt * e_tot).astype(BF16)


def _rw_pair_fn(v_ref, h_ref, y_ref, stage, rev):
    n_l = RW_CHUNK
    rt_s, at_s, bt_s, kt_s, bh_s, kh_s, etot_s = stage
    lane = _iota((1, LANES), 1)
    lo = lane < RW_HEAD_DIM
    hi = jnp.logical_not(lo)
    ti2 = _iota((2 * n_l, LANES), 0)
    si2 = _iota((2 * n_l, LANES), 1)
    tq = jnp.where(ti2 >= n_l, ti2 - n_l, ti2)
    sk = jnp.where(si2 >= n_l, si2 - n_l, si2)
    strict2 = (sk > tq) if rev else (sk < tq)
    mask_g = strict2 | ((ti2 >= n_l) & (sk == tq))
    eye2 = jnp.where(ti2 == si2, 1.0, 0.0)
    zeros_l = jnp.zeros((n_l, LANES), BF16)
    bd = (_iota((LANES, LANES), 0) < RW_HEAD_DIM) == (_iota((LANES, LANES), 1) < RW_HEAD_DIM)

    def pair(p):
        sl = pl.ds(p * LANES, LANES)
        rt = rt_s[:, sl]
        at = at_s[:, sl]
        bt_p = bt_s[:, sl]
        kt_p = kt_s[:, sl]
        vv = v_ref[0, :, sl]
        h0 = h_ref[p]
        h0b = h0.astype(BF16)
        zb = jnp.zeros_like(at)
        g0 = _dot_nt(jnp.concatenate([jnp.where(lo, at, zb), jnp.where(lo, rt, zb)], axis=0),
                     jnp.concatenate([bt_p, kt_p], axis=0))
        g1 = _dot_nt(jnp.concatenate([jnp.where(hi, at, zb), jnp.where(hi, rt, zb)], axis=0),
                     jnp.concatenate([kt_p, bt_p], axis=0))
        zv = jnp.concatenate([zeros_l, vv], axis=0)
        vz = jnp.concatenate([vv, zeros_l], axis=0)
        ah0 = _dot_nt(jnp.where(lo, at, zb), h0b)
        ah1 = _dot_nt(jnp.where(hi, at, zb), h0b)
        yh = _dot_nt(rt, h0b) if y_ref is not None else None
        yield
        g0 = jnp.where(mask_g, g0, 0.0)
        g1 = jnp.where(mask_g, g1, 0.0)
        a_blk = jnp.concatenate([jnp.where(lo, g0[:n_l], 0.0), jnp.where(hi, g1[:n_l], 0.0)], axis=0)
        t_inv = eye2 + a_blk
        a_pow = a_blk
        rhs0 = ah0 + _dot(g0[:n_l].astype(BF16), zv)
        rhs1 = ah1 + _dot(g1[:n_l].astype(BF16), vz)
        rhs = jnp.concatenate([jnp.where(lo, rhs0, 0.0), jnp.where(hi, rhs1, 0.0)], axis=0).astype(BF16)
        for _ in range(5):
            ab = a_pow.astype(BF16)
            a_pow = _dot(ab, ab)
            yield
            t_inv = t_inv + _dot(a_pow.astype(BF16), t_inv.astype(BF16))
        yield
        u_st = _dot(t_inv.astype(BF16), rhs)
        yield
        u = (u_st[:n_l] + u_st[n_l:]).astype(BF16)
        uv = jnp.concatenate([u, vv], axis=0)
        if y_ref is not None:
            vu = jnp.concatenate([vv, u], axis=0)
            y_ref[0, :, sl] = yh + jnp.where(lo, _dot(g0[n_l:].astype(BF16), uv), _dot(g1[n_l:].astype(BF16), vu))
        upd = _dot_tn(uv, jnp.concatenate([bh_s[:, sl], kh_s[:, sl]], axis=0))
        yield
        h_ref[p] = h0 * etot_s[0:1, sl] + jnp.where(bd, upd, 0.0)

    return pair


def _rw_step(refs_f, refs_b, hf_ref, hb_ref, yf_ref, yb_ref, stage_f, stage_b):
    _rw_stage(refs_f, stage_f, False)
    _rw_stage(refs_b, stage_b, True)
    pair_f = _rw_pair_fn(refs_f[3], hf_ref, yf_ref, stage_f, False)
    pair_b = _rw_pair_fn(refs_b[3], hb_ref, yb_ref, stage_b, True)
    n_pairs = D_MODEL // LANES
    for p0 in range(0, n_pairs, RW_PAIR_UNROLL):
        _round_robin([f(p) for p in range(p0, p0 + RW_PAIR_UNROLL) for f in (pair_f, pair_b)])


def _rw_scan_kernel(*refs, n_cc):
    ctx_f, lat_f, ctx_b, lat_b = refs[0:6], refs[6:12], refs[12:18], refs[18:24]
    yf_ref, yb_ref, hf_ref, hb_ref = refs[24:28]
    stage_f, stage_b = refs[28:35], refs[35:42]
    j = pl.program_id(1)

    @pl.when(j == 0)
    def _():
        hf_ref[...] = jnp.zeros_like(hf_ref)
        hb_ref[...] = jnp.zeros_like(hb_ref)

    @pl.when(j < n_cc)
    def _():
        _rw_step(ctx_f, ctx_b, hf_ref, hb_ref, None, None, stage_f, stage_b)

    @pl.when(j >= n_cc)
    def _():
        _rw_step(lat_f, lat_b, hf_ref, hb_ref, yf_ref, yb_ref, stage_f, stage_b)


def _rw_scan(cx, lat):
    n_b, n_t, _ = lat["r"].shape
    n_l = RW_CHUNK
    n_cc = cx["r"].shape[1] // n_l
    n_lc = n_t // n_l
    blk = (1, n_l, D_MODEL)
    ctx_f = pl.BlockSpec(blk, lambda b, j: (b, jnp.minimum(j, n_cc - 1), 0))
    lat_f = pl.BlockSpec(blk, lambda b, j: (b, jnp.maximum(j - n_cc, 0), 0))
    ctx_b = pl.BlockSpec(blk, lambda b, j: (b, jnp.maximum(n_cc - 1 - j, 0), 0))
    lat_b = pl.BlockSpec(blk, lambda b, j: (b, n_lc - 1 - jnp.maximum(j - n_cc, 0), 0))
    fwd = ("r", "lw0", "k0", "v", "a", "b0")
    bwd = ("r", "lw1", "k1", "v", "a", "b1")
    args = ([cx[n] for n in fwd] + [lat[n] for n in fwd] + [cx[n] for n in bwd] + [lat[n] for n in bwd])
    stage = [pltpu.VMEM((n_l, D_MODEL), BF16)] * 6 + [pltpu.VMEM((8, D_MODEL), F32)]
    return pl.pallas_call(
        functools.partial(_rw_scan_kernel, n_cc=n_cc),
        grid=(n_b, n_cc + n_lc),
        in_specs=[ctx_f] * 6 + [lat_f] * 6 + [ctx_b] * 6 + [lat_b] * 6,
        out_specs=[lat_f, lat_b],
        out_shape=[jax.ShapeDtypeStruct((n_b, n_t, D_MODEL), F32)] * 2,
        scratch_shapes=[pltpu.VMEM((D_MODEL // LANES, LANES, LANES), F32)] * 2 + stage + stage,
        compiler_params=_cparams(("parallel", "arbitrary")),
        name="rw_scan",
    )(*args)


def _rw_out_kernel(x_ref, yf_ref, yb_ref, r_ref, k0_ref, k1_ref, v_ref, g_ref, ga_ref, shm_ref, scm_ref, gm_ref,
                   rk_ref, lnw_ref, lnb_ref, bd_ref, wo_ref, w1_ref, w2_ref, o_ref):
    bd = bd_ref[...]
    inv_n = 1.0 / RW_HEAD_DIM
    y = yf_ref[0] + yb_ref[0]
    dlt = y - _group_sum(y, bd) * inv_n
    var = _group_sum(dlt * dlt, bd) * inv_n
    yn = dlt * lax.rsqrt(var + RW_GN_EPS) * lnw_ref[...] + lnb_ref[...]
    r = r_ref[0].astype(F32)
    kd = k0_ref[0].astype(F32) + k1_ref[0].astype(F32)
    coef = _group_sum(r * kd * rk_ref[...], bd)
    o = _dot(((yn + coef * v_ref[0].astype(F32)) * g_ref[0].astype(F32)).astype(BF16), wo_ref[...])
    x1 = x_ref[0] + ga_ref[0] * o
    o_ref[0] = _mlp(x1, shm_ref[0], scm_ref[0], gm_ref[0], w1_ref, w2_ref)


def _rw_out_mlp(x, yf, yb, lat, mods, prm, w1, w2):
    n_b, n_t, _ = x.shape
    tm = TOKEN_TILE
    tok = pl.BlockSpec((1, tm, D_MODEL), lambda b, i: (b, i, 0))
    mod = pl.BlockSpec((1, 1, D_MODEL), lambda b, i: (b, 0, 0))
    vec = _const_spec((1, D_MODEL))
    return pl.pallas_call(
        _rw_out_kernel,
        grid=(n_b, n_t // tm),
        in_specs=[tok] * 8 + [mod] * 4 + [vec, vec, vec, _const_spec((LANES, LANES)),
                                          _const_spec((D_MODEL, D_MODEL)),
                                          _const_spec((D_MODEL, D_FF)), _const_spec((D_FF, D_MODEL))],
        out_specs=tok,
        out_shape=jax.ShapeDtypeStruct(x.shape, F32),
        compiler_params=_cparams(("parallel", "parallel")),
        name="rw_out_mlp",
    )(x, yf, yb, lat["r"], lat["k0"], lat["k1"], lat["v"], lat["g"],
      mods["gate_a"], mods["shift_m"], mods["scale_m"], mods["gate_m"],
      prm["r_k"], prm["lnx_w"], prm["lnx_b"], prm["bd128"], prm["w_o"], w1, w2)


def _ab_params(w_in, gate_b, q_norm, k_norm, n_tokens):
    nw, mw = NA_WIDTH, ML_WIDTH
    qb = w_in[:, 3 * nw:3 * nw + mw]
    kb = w_in[:, 3 * nw + mw:3 * nw + 2 * mw]
    lane = np.arange(mw)
    partner = (lane // 32 ^ 1) * 32 + lane % 32
    head_major = np.array([4 * (j % 4) + j // 4 for j in range(4 * ML_HEADS)])
    wg = w_in[:, 3 * nw + 4 * mw:][:, head_major]
    w_all = jnp.concatenate([w_in[:, :3 * nw + 4 * mw], qb[:, partner], kb[:, partner]], axis=1).astype(BF16)
    pos = np.arange(n_tokens)
    n_freq = ML_HEAD_DIM // 4
    inv_freq = ROPE_BASE ** (-jnp.arange(n_freq, dtype=F32) / n_freq)
    ang_r = jnp.asarray(pos // GRID_W, F32)[:, None] * inv_freq
    ang_c = jnp.asarray(pos % GRID_W, F32)[:, None] * inv_freq
    cos = jnp.concatenate([jnp.cos(ang_r)] * 2 + [jnp.cos(ang_c)] * 2, axis=1)
    sin = jnp.concatenate([-jnp.sin(ang_r), jnp.sin(ang_r), -jnp.sin(ang_c), jnp.sin(ang_c)], axis=1)
    grp = np.arange(NA_WIDTH) // NA_HEAD_DIM
    bd512 = jnp.asarray((grp[:, None] == grp[None, :]) / NA_HEAD_DIM, BF16)
    gb = gate_b[head_major]
    return dict(w_all=w_all, wg_t=wg.T.astype(BF16), bd512=bd512,
                q_norm=jnp.tile(q_norm, NA_HEADS)[None], k_norm=jnp.tile(k_norm, NA_HEADS)[None],
                cos=cos, sin=sin, gate_b_t=gb[:, None])


def _rw_params(mu, w_rkv, w0, w1, w2, a0, a1, a2, g1, g2, k_k, k_a, r_k, lnx_w, lnx_b, w_o):
    grp = np.arange(LANES) // RW_HEAD_DIM
    pad_g = 256 - RW_GATE_LORA
    return dict(
        mu=jnp.concatenate([mu, jnp.zeros((2, D_MODEL), F32)], axis=0),
        w_rkv=w_rkv.astype(BF16),
        w1=jnp.concatenate([w1[0], w1[1]], axis=1).astype(BF16),
        w2=jnp.concatenate([w2[0], w2[1]], axis=0).astype(BF16), w0=w0,
        a1=jnp.concatenate([a1[0], a1[1]], axis=1).astype(BF16),
        a2=jnp.concatenate([a2[0], a2[1]], axis=0).astype(BF16), a0=a0,
        g1=jnp.pad(g1, ((0, 0), (0, pad_g))).astype(BF16), g2=jnp.pad(g2, ((0, pad_g), (0, 0))).astype(BF16),
        k_k=k_k[None], k_a=k_a[None], r_k=r_k[None], lnx_w=lnx_w[None], lnx_b=lnx_b[None],
        bd128=jnp.asarray(grp[:, None] == grp[None, :], BF16), w_o=w_o.astype(BF16))


def _mod_dict(m, n_b):
    names = ("shift_a", "scale_a", "gate_a", "shift_m", "scale_m", "gate_m")
    parts = jnp.split(m, 6, axis=-1)
    return {n: jnp.broadcast_to(p[:, None, :], (n_b, 1, D_MODEL)) for n, p in zip(names, parts)}


_PREP_NAMES = ("r", "v", "g", "a", "k0", "k1", "b0", "b1", "lw0", "lw1")


def kernel(x, c, ctx, c_ctx, ada_w, ada_b, ab_w_in, ab_gate_b, na_q_norm, na_k_norm, na_rpb, ml_head_norm, ab_w_out, rw_mu, rw_w_rkv, rw_w0, rw_w1, rw_w2, rw_a0, rw_a1, rw_a2, rw_g1, rw_g2, rw_k_k, rw_k_a, rw_r_k, rw_lnx_w, rw_lnx_b, rw_w_o, mlp_w1, mlp_w2):
    n_b, n_t, _ = x.shape
    assert ada_w.shape[0] == 2, "even (attention / mLSTM) layer followed by an odd (RWKV-7) layer"
    cond = jnp.concatenate([c, c_ctx[None], jnp.zeros((16 - n_b - 1, D_MODEL), F32)], axis=0)
    mods = _adaln(cond, ada_w, ada_b)
    w1 = mlp_w1.astype(BF16)
    w2 = mlp_w2.astype(BF16)

    m_l = _mod_dict(mods[0, :n_b], n_b)
    m_c = _mod_dict(mods[0, n_b:n_b + 1], n_b)
    prm = _ab_params(ab_w_in[0], ab_gate_b[0], na_q_norm[0], na_k_norm[0], n_t)
    qa_l, ka_l, va_l, qb_l, kb_l, vb_l, ob_l, gt_l = _ab_project(x, m_l["shift_a"], m_l["scale_a"], prm, True)
    qa_c, ka_c, va_c, qb_c, kb_c, vb_c, ob_c, gt_c = _ab_project(ctx, m_c["shift_a"], m_c["scale_a"], prm, False)
    bias = _na_bias_table(na_rpb[0], n_t // GRID_W)
    na_l = _na_attention(qa_l, ka_l, va_l, ka_c, va_c, bias)
    na_c = _ctx_attention(qa_c, ka_c, va_c)
    hlf, hlb, hcf, hcb = _mlstm(qb_l, kb_l, vb_l, gt_l, qb_c, kb_c, vb_c, gt_c)
    head_norm = ml_head_norm[0].reshape(1, ML_WIDTH)
    w_out = ab_w_out[0].astype(BF16)
    x = _ab_out_mlp(x, na_l, hlf, hlb, ob_l, m_l, head_norm, w_out, w1[0], w2[0])
    ctx = _ab_out_mlp(ctx, na_c, hcf, hcb, ob_c, m_c, head_norm, w_out, w1[0], w2[0])

    m_l = _mod_dict(mods[1, :n_b], n_b)
    m_c = _mod_dict(mods[1, n_b:n_b + 1], n_b)
    rprm = _rw_params(rw_mu[0], rw_w_rkv[0], rw_w0[0], rw_w1[0], rw_w2[0], rw_a0[0], rw_a1[0], rw_a2[0],
                      rw_g1[0], rw_g2[0], rw_k_k[0], rw_k_a[0], rw_r_k[0], rw_lnx_w[0], rw_lnx_b[0], rw_w_o[0])
    lat = dict(zip(_PREP_NAMES, _rw_prepare(x, m_l["shift_a"], m_l["scale_a"], rprm)))
    cxp = dict(zip(_PREP_NAMES, _rw_prepare(ctx, m_c["shift_a"], m_c["scale_a"], rprm)))
    yf, yb = _rw_scan(cxp, lat)
    return _rw_out_mlp(x, yf, yb, lat, m_l, rprm, w1[1], w2[1])
```

```python
import functools

import numpy as np
import jax
import jax.numpy as jnp
from jax import lax
from jax.experimental import pallas as pl
from jax.experimental.pallas import tpu as pltpu

F32 = jnp.float32
BF16 = jnp.bfloat16

D_MODEL = 1024
GRID_W = 64
NA_HEAD_DIM = 64
NA_HEADS = 8
NA_WIDTH = 512
NA_KH = 8
NA_KW = 16
ML_HEAD_DIM = 128
ML_HEADS = 4
ML_WIDTH = 512
ML_CHUNK = 128
ML_GATE_CAP = 15.0
RW_HEAD_DIM = 64
RW_HEADS = 16
RW_LORA = 64
RW_GATE_LORA = 160
RW_GN_EPS = 64e-5
D_FF = 4 * D_MODEL
ROPE_BASE = 10000.0
NORM_EPS = 1e-6

LANES = 128
VMEM_LIMIT = 56 * 1024 * 1024
TOKEN_TILE = 512
MLP_TILE = 512
NA_ROWS = 4
NA_KEY_ROWS = NA_ROWS + NA_KH - 1
NA_UNROLL = 4
RW_CHUNK = 64
RW_BLOCK = 256
RW_PAIR_UNROLL = 8
NEG_BIG = -1e30
RW_DECAY_SCALE = float(np.exp(-0.5))


def _cparams(sem):
    return pltpu.CompilerParams(dimension_semantics=sem, vmem_limit_bytes=VMEM_LIMIT)


def _const_spec(shape):
    nd = len(shape)
    return pl.BlockSpec(shape, lambda *_: (0,) * nd, pipeline_mode=pl.Buffered(1))


def _dot(a, b):
    return jnp.dot(a, b, preferred_element_type=F32)


def _dot_nt(a, b):
    return lax.dot_general(a, b, (((1,), (1,)), ((), ())), preferred_element_type=F32)


def _dot_tn(a, b):
    return lax.dot_general(a, b, (((0,), (0,)), ((), ())), preferred_element_type=F32)


def _split3(x):
    hi = x.astype(BF16)
    r1 = x - hi.astype(F32)
    mid = r1.astype(BF16)
    lo = (r1 - mid.astype(F32)).astype(BF16)
    return hi, mid, lo


def _dot_exact_rhs(a_bf16, x):
    hi, mid, lo = _split3(x)
    return _dot(a_bf16, hi) + _dot(a_bf16, mid) + _dot(a_bf16, lo)


def _dot_exact_lhs(x, a_bf16):
    hi, mid, lo = _split3(x)
    return _dot(hi, a_bf16) + _dot(mid, a_bf16) + _dot(lo, a_bf16)


def _dot_f32(a, b):
    ah, am, al = _split3(a)
    bh, bm, bl = _split3(b)
    return (_dot(ah, bh) + _dot(ah, bm) + _dot(am, bh)
            + _dot(ah, bl) + _dot(am, bm) + _dot(al, bh))


def _sigmoid(x):
    return 1.0 / (1.0 + jnp.exp(-x))


def _log_sigmoid(x):
    return jnp.minimum(x, 0.0) - jnp.log(1.0 + jnp.exp(-jnp.abs(x)))


def _rms(x):
    return x * lax.rsqrt(jnp.mean(x * x, axis=-1, keepdims=True) + NORM_EPS)


def _modulate(x, shift, scale):
    return _rms(x) * (1.0 + scale) + shift


def _iota(shape, dim):
    return lax.broadcasted_iota(jnp.int32, shape, dim)


def _round_robin(chains, stagger=False):
    pending = list(chains)
    live = []
    while pending or live:
        if pending:
            live.extend(pending[:1] if stagger else pending)
            del pending[:1 if stagger else len(pending)]
        nxt = []
        for ch in live:
            try:
                next(ch)
                nxt.append(ch)
            except StopIteration:
                pass
        live = nxt


def _adaln_kernel(c_ref, w_ref, b_ref, o_ref):
    c = c_ref[...]
    s = c * _sigmoid(c)
    o_ref[0] = _dot_f32(s, w_ref[0]) + b_ref[0]


def _adaln(cond, ada_w, ada_b):
    depth = ada_w.shape[0]
    tn = 512
    return pl.pallas_call(
        _adaln_kernel,
        grid=(depth, 6 * D_MODEL // tn),
        in_specs=[
            pl.BlockSpec((16, D_MODEL), lambda l, j: (0, 0)),
            pl.BlockSpec((1, D_MODEL, tn), lambda l, j: (l, 0, j)),
            pl.BlockSpec((1, 1, tn), lambda l, j: (l, 0, j)),
        ],
        out_specs=pl.BlockSpec((1, 16, tn), lambda l, j: (l, 0, j)),
        out_shape=jax.ShapeDtypeStruct((depth, 16, 6 * D_MODEL), F32),
        compiler_params=_cparams(("parallel", "parallel")),
        name="adaln",
    )(cond, ada_w, ada_b.reshape(depth, 1, 6 * D_MODEL))


_C_QA, _C_KA, _C_VA, _C_QB, _C_KB, _C_VB, _C_OB, _C_QBP, _C_KBP = (
    0, 512, 1024, 1536, 2048, 2560, 3072, 3584, 4096)
_AB_COLS = 4608


def _ab_proj_kernel(x_ref, sh_ref, sc_ref, w_ref, wgt_ref, bd_ref, qn_ref, kn_ref, cos_ref, sin_ref, gbt_ref,
                    qa_ref, ka_ref, va_ref, qb_ref, kb_ref, vb_ref, ob_ref, gt_ref, *, rope):
    h = _modulate(x_ref[0], sh_ref[0], sc_ref[0]).astype(BF16)

    def proj(c0, n=512):
        return _dot(h, w_ref[:, c0:c0 + n])

    bd = bd_ref[...]

    def head_norm(y, gain):
        ms = _dot((y * y).astype(BF16), bd)
        return y * lax.rsqrt(ms + NORM_EPS) * gain

    qa_ref[0] = (head_norm(proj(_C_QA), qn_ref[...]) * (NA_HEAD_DIM ** -0.5)).astype(BF16)
    ka_ref[0] = head_norm(proj(_C_KA), kn_ref[...]).astype(BF16)
    va_ref[0] = proj(_C_VA).astype(BF16)
    qb = proj(_C_QB)
    kb = proj(_C_KB)
    if rope:
        cos = jnp.concatenate([cos_ref[...]] * ML_HEADS, axis=1)
        sin = jnp.concatenate([sin_ref[...]] * ML_HEADS, axis=1)
        qb = qb * cos + proj(_C_QBP) * sin
        kb = kb * cos + proj(_C_KBP) * sin
    qb_ref[0] = qb.astype(BF16)
    kb_ref[0] = (kb * (ML_HEAD_DIM ** -0.5)).astype(BF16)
    vb_ref[0] = proj(_C_VB).astype(BF16)
    ob_ref[0] = proj(_C_OB)
    gt_ref[0] = _dot_nt(wgt_ref[...], h) + gbt_ref[...]


def _ab_project(x, shift, scale, prm, rope):
    n_b, n_t, _ = x.shape
    tm = min(TOKEN_TILE, n_t)
    tok = lambda w, dt: jax.ShapeDtypeStruct((n_b, n_t, w), dt)
    tok_spec = lambda w: pl.BlockSpec((1, tm, w), lambda b, i: (b, i, 0))
    mod_spec = pl.BlockSpec((1, 1, D_MODEL), lambda b, i: (b, 0, 0))
    return pl.pallas_call(
        functools.partial(_ab_proj_kernel, rope=rope),
        grid=(n_b, n_t // tm),
        in_specs=[
            tok_spec(D_MODEL), mod_spec, mod_spec,
            _const_spec((D_MODEL, _AB_COLS)), _const_spec((4 * ML_HEADS, D_MODEL)),
            _const_spec((NA_WIDTH, NA_WIDTH)), _const_spec((1, NA_WIDTH)), _const_spec((1, NA_WIDTH)),
            pl.BlockSpec((tm, LANES), lambda b, i: (i, 0)), pl.BlockSpec((tm, LANES), lambda b, i: (i, 0)),
            _const_spec((4 * ML_HEADS, 1)),
        ],
        out_specs=[tok_spec(512)] * 7 + [pl.BlockSpec((1, 4 * ML_HEADS, tm), lambda b, i: (b, 0, i))],
        out_shape=[tok(512, BF16)] * 6 + [tok(512, F32), jax.ShapeDtypeStruct((n_b, 4 * ML_HEADS, n_t), F32)],
        compiler_params=_cparams(("parallel", "parallel")),
        name="ab_proj_rope" if rope else "ab_proj",
    )(x, shift, scale, prm["w_all"], prm["wg_t"], prm["bd512"], prm["q_norm"], prm["k_norm"],
      prm["cos"][:n_t], prm["sin"][:n_t], prm["gate_b_t"])


def _na_kernel(q_ref, k_ref, v_ref, kc_ref, vc_ref, bias_ref, o_ref):
    n_t = q_ref.shape[1]
    n_rows = n_t // GRID_W
    rq = NA_ROWS * GRID_W
    n_blk = n_t // rq
    n_keys = NA_KEY_ROWS * GRID_W
    kc = kc_ref[0]
    vc = vc_ref[0]
    lane = _iota((1, LANES), 1)

    def chain(bi, hh, outs):
        kr0 = jnp.clip(bi * NA_ROWS - NA_KH // 2, 0, n_rows - NA_KEY_ROWS)
        k0 = pl.multiple_of(kr0 * GRID_W, GRID_W)
        case = jnp.where(bi == 0, 0, jnp.where(bi == n_blk - 1, 2, 1))
        q = q_ref[0, pl.ds(pl.multiple_of(bi * rq, rq), rq), :]
        in_head = (lane >= hh * NA_HEAD_DIM) & (lane < (hh + 1) * NA_HEAD_DIM)
        qm = jnp.where(in_head, q, jnp.zeros_like(q))
        s_nb = _dot_nt(qm, k_ref[0, pl.ds(k0, n_keys), :])
        s_cx = _dot_nt(qm, kc)
        yield
        s_nb = s_nb + bias_ref[case, hh].astype(F32)
        m = jnp.maximum(jnp.max(s_nb, axis=-1, keepdims=True), jnp.max(s_cx, axis=-1, keepdims=True))
        p_nb = jnp.exp(s_nb - m)
        p_cx = jnp.exp(s_cx - m)
        den = jnp.sum(p_nb, axis=-1, keepdims=True) + jnp.sum(p_cx, axis=-1, keepdims=True)
        o = _dot(p_nb.astype(BF16), v_ref[0, pl.ds(k0, n_keys), :]) + _dot(p_cx.astype(BF16), vc)
        yield
        outs[hh] = o / den

    def body(i, carry):
        blocks = [(NA_UNROLL * i + u, [None, None]) for u in range(NA_UNROLL)]
        _round_robin([chain(bi, hh, outs) for bi, outs in blocks for hh in range(2)], stagger=True)
        for bi, outs in blocks:
            o_ref[0, pl.ds(pl.multiple_of(bi * rq, rq), rq), :] = jnp.where(
                lane < NA_HEAD_DIM, outs[0], outs[1]).astype(o_ref.dtype)
        return carry

    lax.fori_loop(0, n_blk // NA_UNROLL, body, 0)


def _na_attention(qa, ka, va, ka_c, va_c, bias):
    n_b, n_t, _ = qa.shape
    n_ctx = ka_c.shape[1]
    seq = pl.BlockSpec((1, n_t, LANES), lambda b, hp: (b, 0, hp))
    cseq = pl.BlockSpec((1, n_ctx, LANES), lambda b, hp: (b, 0, hp))
    return pl.pallas_call(
        _na_kernel,
        grid=(n_b, NA_HEADS // 2),
        in_specs=[seq, seq, seq, cseq, cseq,
                  pl.BlockSpec((3, 2) + bias.shape[2:], lambda b, hp: (0, hp, 0, 0))],
        out_specs=seq,
        out_shape=jax.ShapeDtypeStruct((n_b, n_t, NA_WIDTH), BF16),
        compiler_params=_cparams(("parallel", "arbitrary")),
        name="na_attention",
    )(qa, ka, va, ka_c, va_c, bias)


def _ctx_attn_kernel(q_ref, k_ref, v_ref, o_ref):
    q = q_ref[0]
    k = k_ref[0]
    v = v_ref[0]
    lane = _iota((1, LANES), 1)
    outs = []
    for hh in range(2):
        in_head = (lane >= hh * NA_HEAD_DIM) & (lane < (hh + 1) * NA_HEAD_DIM)
        qm = jnp.where(in_head, q, jnp.zeros_like(q))
        s = _dot_nt(qm, k)
        p = jnp.exp(s - jnp.max(s, axis=-1, keepdims=True))
        outs.append(_dot(p.astype(BF16), v) / jnp.sum(p, axis=-1, keepdims=True))
    o_ref[0] = jnp.where(lane < NA_HEAD_DIM, outs[0], outs[1]).astype(o_ref.dtype)


def _ctx_attention(qa, ka, va):
    n_b, n_ctx, _ = qa.shape
    spec = pl.BlockSpec((1, n_ctx, LANES), lambda b, hp: (b, 0, hp))
    return pl.pallas_call(
        _ctx_attn_kernel,
        grid=(n_b, NA_HEADS // 2),
        in_specs=[spec, spec, spec],
        out_specs=spec,
        out_shape=jax.ShapeDtypeStruct((n_b, n_ctx, NA_WIDTH), BF16),
        compiler_params=_cparams(("parallel", "parallel")),
        name="ctx_attention",
    )(qa, ka, va)


def _na_bias_table(rpb, n_rows):
    n_blk = n_rows // NA_ROWS
    cq = np.arange(GRID_W)[:, None]
    ck = np.arange(GRID_W)[None, :]
    cs = np.clip(cq - NA_KW // 2, 0, GRID_W - NA_KW)
    col_ok = (ck >= cs) & (ck < cs + NA_KW)
    dc = np.clip(ck - cq + NA_KW - 1, 0, 2 * NA_KW - 2)
    toep = jnp.where(col_ok[None, None], rpb[:, :, dc], NEG_BIG)
    toep = jnp.concatenate([toep, jnp.full_like(toep[:, :1], NEG_BIG)], axis=1)
    tables = []
    for bi in (0, 1, n_blk - 1):
        r0 = bi * NA_ROWS
        kr0 = int(np.clip(r0 - NA_KH // 2, 0, n_rows - NA_KEY_ROWS))
        rq = r0 + np.arange(NA_ROWS)[:, None]
        rk = kr0 + np.arange(NA_KEY_ROWS)[None, :]
        rs = np.clip(rq - NA_KH // 2, 0, n_rows - NA_KH)
        row_ok = (rk >= rs) & (rk < rs + NA_KH)
        dr = np.where(row_ok, rk - rq + NA_KH - 1, 2 * NA_KH - 1)
        t = toep[:, dr]
        t = jnp.transpose(t, (0, 1, 3, 2, 4)).reshape(rpb.shape[0], NA_ROWS * GRID_W, NA_KEY_ROWS * GRID_W)
        tables.append(t)
    return jnp.stack(tables).astype(BF16)


def _mlstm_prologue(grow_ref, gcol_ref, rows_s, cols_s):
    n_l = ML_CHUNK
    ti = _iota((n_l, n_l), 0)
    si = _iota((n_l, n_l), 1)
    lower = jnp.where(si <= ti, 1.0, 0.0).astype(BF16)
    upper = jnp.where(si >= ti, 1.0, 0.0).astype(BF16)
    cap = lambda g: ML_GATE_CAP * jnp.tanh(g / ML_GATE_CAP)
    for hh in range(2):
        for d in range(2):
            ch = 2 * hh + d
            rows_s[ch, 1] = cap(grow_ref[0, hh, 2 * d])
            rows_s[ch, 0] = _dot_exact_lhs(_log_sigmoid(cap(grow_ref[0, hh, 2 * d + 1])), lower if d else upper)
            cols_s[ch, 1] = cap(gcol_ref[0, hh, 2 * d])
            cols_s[ch, 0] = _dot_exact_rhs(upper if d else lower, _log_sigmoid(cap(gcol_ref[0, hh, 2 * d + 1])))


def _mlstm_chain(ch, c, q_ref, k_ref, v_ref, rows_s, cols_s, c_s, n_s, m_s, out_ref):
    n_l = ML_CHUNK
    hh, rev = ch // 2, ch % 2 == 1
    t0 = pl.multiple_of(c * n_l, n_l)
    hs = pl.ds(hh * ML_HEAD_DIM, ML_HEAD_DIM)
    q = q_ref[0, pl.ds(t0, n_l), hs]
    k = k_ref[0, pl.ds(t0, n_l), hs]
    v = v_ref[0, pl.ds(t0, n_l), hs]
    c_st = c_s[ch]
    n_st = n_s[ch, 0:1, :]
    m_st = m_s[ch, 0:1, 0:1]
    b_row = rows_s[ch, 0, pl.ds(c, 1), :]
    i_row = rows_s[ch, 1, pl.ds(c, 1), :]
    at_c = _iota((1, LANES), 1) == c
    b_t = jnp.sum(jnp.where(at_c, cols_s[ch, 0], 0.0), axis=1, keepdims=True)
    i_col = jnp.sum(jnp.where(at_c, cols_s[ch, 1], 0.0), axis=1, keepdims=True)
    ti = _iota((n_l, n_l), 0)
    si = _iota((n_l, n_l), 1)
    mask = (si >= ti) if rev else (si <= ti)
    last = 0 if rev else n_l - 1
    log_d = jnp.where(mask, b_t - b_row + i_row, -jnp.inf)
    log_inter = b_t + m_st
    m_t = jnp.maximum(log_inter, jnp.max(log_d, axis=-1, keepdims=True))
    is_last = _iota((n_l, 1), 0) == last
    m_new = jnp.sum(jnp.where(is_last, m_t, 0.0), axis=0, keepdims=True)
    b_last = jnp.sum(jnp.where(is_last, b_t, 0.0), axis=0, keepdims=True)
    w_s = jnp.exp(b_last - b_t + i_col - m_new)
    decay = jnp.exp(b_last + m_st - m_new)
    d_mat = jnp.exp(log_d - m_t)
    w_inter = jnp.exp(log_inter - m_t)
    kw = k.astype(F32) * w_s
    s = _dot_nt(q, k)
    qc = _dot(q, c_st.astype(BF16))
    upd = _dot_tn(kw.astype(BF16), v)
    yield
    s = s * d_mat
    sv = _dot(s.astype(BF16), v)
    c_s[ch] = decay * c_st + upd
    n_s[ch] = jnp.broadcast_to(decay * n_st + jnp.sum(kw, axis=0, keepdims=True), n_s.shape[1:])
    m_s[ch] = jnp.broadcast_to(m_new, m_s.shape[1:])
    yield
    num = sv + w_inter * qc
    den = jnp.sum(s, axis=-1, keepdims=True) + w_inter * jnp.sum(q.astype(F32) * n_st, axis=-1, keepdims=True)
    out_ref[0, pl.ds(t0, n_l), hs] = num / jnp.maximum(jnp.abs(den), jnp.exp(-m_t))


def _mlstm_kernel(ql_ref, kl_ref, vl_ref, glr_ref, glc_ref, qc_ref, kc_ref, vc_ref, gcr_ref, gcc_ref,
                  hlf_ref, hlb_ref, hcf_ref, hcb_ref,
                  rows_l, cols_l, rows_c, cols_c, c_s, n_s, m_s):
    n_l = ML_CHUNK
    _mlstm_prologue(gcr_ref, gcc_ref, rows_c, cols_c)
    _mlstm_prologue(glr_ref, glc_ref, rows_l, cols_l)
    c_s[...] = jnp.zeros_like(c_s)
    n_s[...] = jnp.zeros_like(n_s)
    m_s[...] = jnp.zeros_like(m_s)

    def sweep(q_ref, k_ref, v_ref, rows_s, cols_s, outs):
        n_c = q_ref.shape[1] // n_l

        def body(j, carry):
            _round_robin([_mlstm_chain(ch, (n_c - 1 - j) if ch % 2 else j, q_ref, k_ref, v_ref, rows_s, cols_s,
                                       c_s, n_s, m_s, outs[ch % 2]) for ch in range(4)])
            return carry

        lax.fori_loop(0, n_c, body, 0)

    sweep(qc_ref, kc_ref, vc_ref, rows_c, cols_c, (hcf_ref, hcb_ref))
    sweep(ql_ref, kl_ref, vl_ref, rows_l, cols_l, (hlf_ref, hlb_ref))


def _mlstm(ql, kl, vl, gl_t, qc, kc, vc, gc_t):
    n_b, n_t, _ = ql.shape
    n_ctx = qc.shape[1]
    n_l = ML_CHUNK

    def gate_layouts(g_t, n):
        n_c = n // n_l
        rows = g_t.reshape(n_b, ML_HEADS, 4, n_c, n_l)
        cols = jnp.swapaxes(rows, 3, 4)
        pad_r = (-n_c) % 8
        return (jnp.pad(rows, ((0, 0),) * 3 + ((0, pad_r), (0, 0))),
                jnp.pad(cols, ((0, 0),) * 4 + ((0, LANES - n_c),)))

    glr, glc = gate_layouts(gl_t, n_t)
    gcr, gcc = gate_layouts(gc_t, n_ctx)
    wide = 2 * ML_HEAD_DIM

    def seq(n):
        return pl.BlockSpec((1, n, wide), lambda b, hp: (b, 0, hp))

    def gspec(a):
        return pl.BlockSpec((1, 2) + a.shape[2:], lambda b, hp: (b, hp, 0, 0, 0))

    vm = lambda *s: pltpu.VMEM(s, F32)
    return pl.pallas_call(
        _mlstm_kernel,
        grid=(n_b, ML_HEADS // 2),
        in_specs=[seq(n_t), seq(n_t), seq(n_t), gspec(glr), gspec(glc),
                  seq(n_ctx), seq(n_ctx), seq(n_ctx), gspec(gcr), gspec(gcc)],
        out_specs=[seq(n_t), seq(n_t), seq(n_ctx), seq(n_ctx)],
        out_shape=[jax.ShapeDtypeStruct((n_b, n_t, ML_WIDTH), F32)] * 2
        + [jax.ShapeDtypeStruct((n_b, n_ctx, ML_WIDTH), F32)] * 2,
        scratch_shapes=[vm(4, 2, glr.shape[3], n_l), vm(4, 2, n_l, LANES),
                        vm(4, 2, gcr.shape[3], n_l), vm(4, 2, n_l, LANES),
                        vm(4, ML_HEAD_DIM, ML_HEAD_DIM), vm(4, 8, ML_HEAD_DIM), vm(4, 8, LANES)],
        compiler_params=_cparams(("parallel", "parallel")),
        name="mlstm",
    )(ql, kl, vl, glr, glc, qc, kc, vc, gcr, gcc)


def _mlp(x1, sh, sc, gt, w1_ref, w2_ref):
    h = _modulate(x1, sh, sc).astype(BF16)
    acc = jnp.zeros_like(x1)
    n_chunk = 1024
    for c in range(D_FF // n_chunk):
        a = jnp.maximum(_dot(h, w1_ref[:, c * n_chunk:(c + 1) * n_chunk]), 0.0)
        acc = acc + _dot((a * a).astype(BF16), w2_ref[c * n_chunk:(c + 1) * n_chunk, :])
    return x1 + gt * acc


def _ab_out_kernel(x_ref, na_ref, hf_ref, hb_ref, ob_ref, ga_ref, shm_ref, scm_ref, gm_ref, hn_ref,
                   wo_ref, w1_ref, w2_ref, o_ref):
    ml = hf_ref[0] + hb_ref[0]
    parts = []
    for hh in range(ML_HEADS):
        parts.append(_rms(ml[:, hh * ML_HEAD_DIM:(hh + 1) * ML_HEAD_DIM]))
    ml = jnp.concatenate(parts, axis=1) * hn_ref[...] * _sigmoid(ob_ref[0])
    o = _dot(na_ref[0], wo_ref[0:NA_WIDTH, :]) + _dot(ml.astype(BF16), wo_ref[NA_WIDTH:, :])
    x1 = x_ref[0] + ga_ref[0] * o
    o_ref[0] = _mlp(x1, shm_ref[0], scm_ref[0], gm_ref[0], w1_ref, w2_ref)


def _ab_out_mlp(x, na, hf, hb, ob, mods, head_norm, w_out, w1, w2):
    n_b, n_t, _ = x.shape
    tm = min(MLP_TILE, n_t)
    tok = lambda w: pl.BlockSpec((1, tm, w), lambda b, i: (b, i, 0))
    mod = pl.BlockSpec((1, 1, D_MODEL), lambda b, i: (b, 0, 0))
    return pl.pallas_call(
        _ab_out_kernel,
        grid=(n_b, n_t // tm),
        in_specs=[tok(D_MODEL), tok(512), tok(512), tok(512), tok(512), mod, mod, mod, mod,
                  _const_spec((1, ML_WIDTH)), _const_spec((D_MODEL, D_MODEL)),
                  _const_spec((D_MODEL, D_FF)), _const_spec((D_FF, D_MODEL))],
        out_specs=tok(D_MODEL),
        out_shape=jax.ShapeDtypeStruct(x.shape, F32),
        compiler_params=_cparams(("parallel", "parallel")),
        name="ab_out_mlp",
    )(x, na, hf, hb, ob, mods["gate_a"], mods["shift_m"], mods["scale_m"], mods["gate_m"],
      head_norm, w_out, w1, w2)


def _group_sum(x, bd):
    parts = []
    for j in range(D_MODEL // LANES):
        parts.append(_dot(x[:, j * LANES:(j + 1) * LANES].astype(BF16), bd))
    return jnp.concatenate(parts, axis=1)


def _rw_prep_kernel(x_ref, xp_ref, xn_ref, sh_ref, sc_ref, mu_ref, wrkv_ref, w1_ref, w2_ref, w0_ref,
                    a1_ref, a2_ref, a0_ref, g1_ref, g2_ref, kk_ref, ka_ref, bd_ref,
                    r_ref, v_ref, g_ref, a_ref, k0_ref, k1_ref, b0_ref, b1_ref, lw0_ref, lw1_ref):
    i = pl.program_id(1)
    n_i = pl.num_programs(1)
    sh = sh_ref[0]
    sc = sc_ref[0]
    h = _modulate(x_ref[0], sh, sc)
    tm = h.shape[0]
    h_before = _modulate(xp_ref[0, 7:8, :], sh, sc) * jnp.where(i > 0, 1.0, 0.0)
    h_after = _modulate(xn_ref[0, 0:1, :], sh, sc) * jnp.where(i < n_i - 1, 1.0, 0.0)
    row = _iota((tm, 1), 0)
    h_prev = jnp.where(row == 0, h_before, pltpu.roll(h, 1, 0))
    h_next = jnp.where(row == tm - 1, h_after, pltpu.roll(h, tm - 1, 0))
    xx = 0.5 * (h_prev + h_next) - h

    def mix(s):
        return (h + xx * mu_ref[s:s + 1, :]).astype(BF16)

    r = _dot(mix(0), wrkv_ref[0])
    k = _dot(mix(2), wrkv_ref[1])
    v = _dot(mix(3), wrkv_ref[2])
    lane = _iota((1, 2 * RW_LORA), 1)
    hid_w = jnp.tanh(_dot(mix(1), w1_ref[...]))
    hid_a = _dot(mix(4), a1_ref[...])
    g = _dot(_sigmoid(_dot(mix(5), g1_ref[...])).astype(BF16), g2_ref[...])
    kk = k * kk_ref[...]
    ss = _group_sum(kk * kk, bd_ref[...])
    kk = kk * jnp.minimum(lax.rsqrt(ss), 1e12)
    r_ref[0] = r.astype(r_ref.dtype)
    v_ref[0] = v.astype(v_ref.dtype)
    g_ref[0] = g.astype(g_ref.dtype)
    a_ref[0] = (-kk).astype(a_ref.dtype)
    for z, (k_out, b_out, lw_out) in enumerate(((k0_ref, b0_ref, lw0_ref), (k1_ref, b1_ref, lw1_ref))):
        in_dir = (lane >= z * RW_LORA) & (lane < (z + 1) * RW_LORA)
        w_logit = w0_ref[z:z + 1, :] + _dot(jnp.where(in_dir, hid_w, 0.0).astype(BF16), w2_ref[...])
        a = _sigmoid(a0_ref[z:z + 1, :] + _dot(jnp.where(in_dir, hid_a, 0.0).astype(BF16), a2_ref[...]))
        lw_out[0] = -RW_DECAY_SCALE * _sigmoid(w_logit)
        k_out[0] = (k * (1.0 + (a - 1.0) * ka_ref[...])).astype(k_out.dtype)
        b_out[0] = (kk * a).astype(b_out.dtype)


def _rw_prepare(x, shift, scale, prm):
    n_b, n_t, _ = x.shape
    tm = min(TOKEN_TILE, n_t)
    tok = pl.BlockSpec((1, tm, D_MODEL), lambda b, i: (b, i, 0))
    mod = pl.BlockSpec((1, 1, D_MODEL), lambda b, i: (b, 0, 0))
    n8 = n_t // 8
    prev_spec = pl.BlockSpec((1, 8, D_MODEL), lambda b, i: (b, jnp.maximum(i * (tm // 8) - 1, 0), 0))
    next_spec = pl.BlockSpec((1, 8, D_MODEL), lambda b, i: (b, jnp.minimum((i + 1) * (tm // 8), n8 - 1), 0))
    out = lambda dt: jax.ShapeDtypeStruct((n_b, n_t, D_MODEL), dt)
    return pl.pallas_call(
        _rw_prep_kernel,
        grid=(n_b, n_t // tm),
        in_specs=[tok, prev_spec, next_spec, mod, mod, _const_spec((8, D_MODEL)),
                  _const_spec((3, D_MODEL, D_MODEL)),
                  _const_spec((D_MODEL, 2 * RW_LORA)), _const_spec((2 * RW_LORA, D_MODEL)), _const_spec((2, D_MODEL)),
                  _const_spec((D_MODEL, 2 * RW_LORA)), _const_spec((2 * RW_LORA, D_MODEL)), _const_spec((2, D_MODEL)),
                  _const_spec((D_MODEL, 256)), _const_spec((256, D_MODEL)),
                  _const_spec((1, D_MODEL)), _const_spec((1, D_MODEL)), _const_spec((LANES, LANES))],
        out_specs=[tok] * 10,
        out_shape=[out(BF16)] * 8 + [out(F32)] * 2,
        compiler_params=_cparams(("parallel", "parallel")),
        name="rw_prepare",
    )(x, x, x, shift, scale, prm["mu"], prm["w_rkv"], prm["w1"], prm["w2"], prm["w0"],
      prm["a1"], prm["a2"], prm["a0"], prm["g1"], prm["g2"], prm["k_k"], prm["k_a"], prm["bd128"])


def _rw_stage(refs, rows, stage, rev):
    r_ref, lw_ref, k_ref, v_ref, a_ref, b_ref = refs
    n_l = RW_CHUNK
    lw = lw_ref[0, rows, :]
    ti = _iota((n_l, n_l), 0)
    si = _iota((n_l, n_l), 1)
    incl = (si >= ti) if rev else (si <= ti)
    cum = _dot_exact_rhs(jnp.where(incl, 1.0, 0.0).astype(BF16), lw)
    tot = jnp.sum(lw, axis=0, keepdims=True)
    e_pos = jnp.exp(cum)
    e_neg = jnp.exp(-cum)
    e_tot = jnp.exp(tot)
    rt_s, at_s, bt_s, kt_s, bh_s, kh_s, etot_s = stage
    etot_s[...] = jnp.broadcast_to(e_tot, etot_s.shape)
    b = b_ref[0, rows, :].astype(F32)
    k = k_ref[0, rows, :].astype(F32)
    rt_s[...] = (r_ref[0, rows, :].astype(F32) * e_pos).astype(BF16)
    at_s[...] = (a_ref[0, rows, :].astype(F32) * jnp.exp(cum - lw)).astype(BF16)
    bt = b * e_neg
    kt = k * e_neg
    bt_s[...] = bt.astype(BF16)
    kt_s[...] = kt.astype(BF16)
    bh_s[...] = (bt * e_tot).astype(BF16)
    kh_s[...] = (kt * e_tot).astype(BF16)


def _rw_pair_fn(v_ref, rows, h_ref, y_ref, stage, rev):
    n_l = RW_CHUNK
    rt_s, at_s, bt_s, kt_s, bh_s, kh_s, etot_s = stage
    lane = _iota((1, LANES), 1)
    lo = lane < RW_HEAD_DIM
    hi = jnp.logical_not(lo)
    ti2 = _iota((2 * n_l, LANES), 0)
    si2 = _iota((2 * n_l, LANES), 1)
    tq = jnp.where(ti2 >= n_l, ti2 - n_l, ti2)
    sk = jnp.where(si2 >= n_l, si2 - n_l, si2)
    strict2 = (sk > tq) if rev else (sk < tq)
    mask_g = strict2 | ((ti2 >= n_l) & (sk == tq))
    eye2 = jnp.where(ti2 == si2, 1.0, 0.0)
    zeros_l = jnp.zeros((n_l, LANES), BF16)
    bd = (_iota((LANES, LANES), 0) < RW_HEAD_DIM) == (_iota((LANES, LANES), 1) < RW_HEAD_DIM)

    def pair(p):
        sl = pl.ds(p * LANES, LANES)
        rt = rt_s[:, sl]
        at = at_s[:, sl]
        bt_p = bt_s[:, sl]
        kt_p = kt_s[:, sl]
        vv = v_ref[0, rows, sl]
        h0 = h_ref[p]
        h0b = h0.astype(BF16)
        zb = jnp.zeros_like(at)
        ar = jnp.concatenate([at, rt], axis=0)
        g = _dot_nt(ar, jnp.concatenate([jnp.where(lo, bt_p, zb), jnp.where(lo, kt_p, zb),
                                         jnp.where(hi, kt_p, zb), jnp.where(hi, bt_p, zb)], axis=0))
        arh = _dot_nt(ar, h0b)
        yield
        g0 = jnp.where(mask_g, g[:, :LANES], 0.0)
        g1 = jnp.where(mask_g, g[:, LANES:], 0.0)
        a_blk = jnp.concatenate([jnp.where(lo, g0[:n_l], 0.0), jnp.where(hi, g1[:n_l], 0.0)], axis=0)
        v_lo = jnp.where(lo, vv, zb)
        v_hi = jnp.where(hi, vv, zb)
        ga = jnp.concatenate([g0[:n_l], g1[:n_l]], axis=1).astype(BF16)
        rhs = arh[:n_l] + _dot(ga, jnp.concatenate([zeros_l, v_lo, v_hi, zeros_l], axis=0))
        rhs = jnp.concatenate([jnp.where(lo, rhs, 0.0), jnp.where(hi, rhs, 0.0)], axis=0).astype(BF16)
        t_inv = eye2 + a_blk
        ab = a_blk.astype(BF16)
        a_pow = _dot(ab, ab)
        yield
        for _ in range(4):
            ab = a_pow.astype(BF16)
            both = _dot(ab, jnp.concatenate([ab, t_inv.astype(BF16)], axis=1))
            yield
            a_pow = both[:, :LANES]
            t_inv = t_inv + both[:, LANES:]
        t_inv = t_inv + _dot(a_pow.astype(BF16), t_inv.astype(BF16))
        yield
        u_st = _dot(t_inv.astype(BF16), rhs)
        yield
        u = (u_st[:n_l] + u_st[n_l:]).astype(BF16)
        if y_ref is not None:
            gr = jnp.concatenate([g0[n_l:], g1[n_l:]], axis=1).astype(BF16)
            uv_heads = jnp.concatenate([jnp.where(lo, u, zb), v_lo, v_hi, jnp.where(hi, u, zb)], axis=0)
            y_ref[0, rows, sl] = arh[n_l:] + _dot(gr, uv_heads)
        upd = _dot_tn(jnp.concatenate([u, vv], axis=0),
                      jnp.concatenate([bh_s[:, sl], kh_s[:, sl]], axis=0))
        yield
        h_ref[p] = h0 * etot_s[0:1, sl] + jnp.where(bd, upd, 0.0)

    return pair


def _rw_block(refs_f, refs_b, hf_ref, hb_ref, yf_ref, yb_ref, stage_f, stage_b):
    n_l = RW_CHUNK
    n_sub = refs_f[0].shape[1] // n_l
    n_pairs = D_MODEL // LANES

    def body(i, carry):
        rows_f = pl.ds(pl.multiple_of(i * n_l, n_l), n_l)
        rows_b = pl.ds(pl.multiple_of((n_sub - 1 - i) * n_l, n_l), n_l)
        _rw_stage(refs_f, rows_f, stage_f, False)
        _rw_stage(refs_b, rows_b, stage_b, True)
        pair_f = _rw_pair_fn(refs_f[3], rows_f, hf_ref, yf_ref, stage_f, False)
        pair_b = _rw_pair_fn(refs_b[3], rows_b, hb_ref, yb_ref, stage_b, True)
        for p0 in range(0, n_pairs, RW_PAIR_UNROLL):
            _round_robin([f(p) for p in range(p0, p0 + RW_PAIR_UNROLL) for f in (pair_f, pair_b)])
        return carry

    lax.fori_loop(0, n_sub, body, 0)


def _rw_scan_kernel(*refs, n_cc):
    ctx_f, lat_f, ctx_b, lat_b = refs[0:6], refs[6:12], refs[12:18], refs[18:24]
    yf_ref, yb_ref, hf_ref, hb_ref = refs[24:28]
    stage_f, stage_b = refs[28:35], refs[35:42]
    j = pl.program_id(1)

    @pl.when(j == 0)
    def _():
        hf_ref[...] = jnp.zeros_like(hf_ref)
        hb_ref[...] = jnp.zeros_like(hb_ref)

    @pl.when(j < n_cc)
    def _():
        _rw_block(ctx_f, ctx_b, hf_ref, hb_ref, None, None, stage_f, stage_b)

    @pl.when(j >= n_cc)
    def _():
        _rw_block(lat_f, lat_b, hf_ref, hb_ref, yf_ref, yb_ref, stage_f, stage_b)


def _rw_scan(cx, lat):
    n_b, n_t, _ = lat["r"].shape
    n_l = RW_CHUNK
    n_cc = cx["r"].shape[1] // RW_BLOCK
    n_lc = n_t // RW_BLOCK
    blk = (1, RW_BLOCK, D_MODEL)
    ctx_f = pl.BlockSpec(blk, lambda b, j: (b, jnp.minimum(j, n_cc - 1), 0))
    lat_f = pl.BlockSpec(blk, lambda b, j: (b, jnp.maximum(j - n_cc, 0), 0))
    ctx_b = pl.BlockSpec(blk, lambda b, j: (b, jnp.maximum(n_cc - 1 - j, 0), 0))
    lat_b = pl.BlockSpec(blk, lambda b, j: (b, n_lc - 1 - jnp.maximum(j - n_cc, 0), 0))
    fwd = ("r", "lw0", "k0", "v", "a", "b0")
    bwd = ("r", "lw1", "k1", "v", "a", "b1")
    args = ([cx[n] for n in fwd] + [lat[n] for n in fwd] + [cx[n] for n in bwd] + [lat[n] for n in bwd])
    stage = [pltpu.VMEM((n_l, D_MODEL), BF16)] * 6 + [pltpu.VMEM((8, D_MODEL), F32)]
    return pl.pallas_call(
        functools.partial(_rw_scan_kernel, n_cc=n_cc),
        grid=(n_b, n_cc + n_lc),
        in_specs=[ctx_f] * 6 + [lat_f] * 6 + [ctx_b] * 6 + [lat_b] * 6,
        out_specs=[lat_f, lat_b],
        out_shape=[jax.ShapeDtypeStruct((n_b, n_t, D_MODEL), F32)] * 2,
        scratch_shapes=[pltpu.VMEM((D_MODEL // LANES, LANES, LANES), F32)] * 2 + stage + stage,
        compiler_params=_cparams(("parallel", "arbitrary")),
        name="rw_scan",
    )(*args)


def _rw_out_kernel(x_ref, yf_ref, yb_ref, r_ref, k0_ref, k1_ref, v_ref, g_ref, ga_ref, shm_ref, scm_ref, gm_ref,
                   rk_ref, lnw_ref, lnb_ref, bd_ref, wo_ref, w1_ref, w2_ref, o_ref):
    bd = bd_ref[...]
    inv_n = 1.0 / RW_HEAD_DIM
    y = yf_ref[0] + yb_ref[0]
    dlt = y - _group_sum(y, bd) * inv_n
    var = _group_sum(dlt * dlt, bd) * inv_n
    yn = dlt * lax.rsqrt(var + RW_GN_EPS) * lnw_ref[...] + lnb_ref[...]
    r = r_ref[0].astype(F32)
    kd = k0_ref[0].astype(F32) + k1_ref[0].astype(F32)
    coef = _group_sum(r * kd * rk_ref[...], bd)
    o = _dot(((yn + coef * v_ref[0].astype(F32)) * g_ref[0].astype(F32)).astype(BF16), wo_ref[...])
    x1 = x_ref[0] + ga_ref[0] * o
    o_ref[0] = _mlp(x1, shm_ref[0], scm_ref[0], gm_ref[0], w1_ref, w2_ref)


def _rw_out_mlp(x, yf, yb, lat, mods, prm, w1, w2):
    n_b, n_t, _ = x.shape
    tm = min(MLP_TILE, n_t)
    tok = pl.BlockSpec((1, tm, D_MODEL), lambda b, i: (b, i, 0))
    mod = pl.BlockSpec((1, 1, D_MODEL), lambda b, i: (b, 0, 0))
    vec = _const_spec((1, D_MODEL))
    return pl.pallas_call(
        _rw_out_kernel,
        grid=(n_b, n_t // tm),
        in_specs=[tok] * 8 + [mod] * 4 + [vec, vec, vec, _const_spec((LANES, LANES)),
                                          _const_spec((D_MODEL, D_MODEL)),
                                          _const_spec((D_MODEL, D_FF)), _const_spec((D_FF, D_MODEL))],
        out_specs=tok,
        out_shape=jax.ShapeDtypeStruct(x.shape, F32),
        compiler_params=_cparams(("parallel", "parallel")),
        name="rw_out_mlp",
    )(x, yf, yb, lat["r"], lat["k0"], lat["k1"], lat["v"], lat["g"],
      mods["gate_a"], mods["shift_m"], mods["scale_m"], mods["gate_m"],
      prm["r_k"], prm["lnx_w"], prm["lnx_b"], prm["bd128"], prm["w_o"], w1, w2)


def _ab_params(w_in, gate_b, q_norm, k_norm, n_tokens):
    nw, mw = NA_WIDTH, ML_WIDTH
    qb = w_in[:, 3 * nw:3 * nw + mw]
    kb = w_in[:, 3 * nw + mw:3 * nw + 2 * mw]
    lane = np.arange(mw)
    partner = (lane // 32 ^ 1) * 32 + lane % 32
    head_major = np.array([4 * (j % 4) + j // 4 for j in range(4 * ML_HEADS)])
    wg = w_in[:, 3 * nw + 4 * mw:][:, head_major]
    w_all = jnp.concatenate([w_in[:, :3 * nw + 4 * mw], qb[:, partner], kb[:, partner]], axis=1).astype(BF16)
    pos = np.arange(n_tokens)
    n_freq = ML_HEAD_DIM // 4
    inv_freq = ROPE_BASE ** (-jnp.arange(n_freq, dtype=F32) / n_freq)
    ang_r = jnp.asarray(pos // GRID_W, F32)[:, None] * inv_freq
    ang_c = jnp.asarray(pos % GRID_W, F32)[:, None] * inv_freq
    cos = jnp.concatenate([jnp.cos(ang_r)] * 2 + [jnp.cos(ang_c)] * 2, axis=1)
    sin = jnp.concatenate([-jnp.sin(ang_r), jnp.sin(ang_r), -jnp.sin(ang_c), jnp.sin(ang_c)], axis=1)
    grp = np.arange(NA_WIDTH) // NA_HEAD_DIM
    bd512 = jnp.asarray((grp[:, None] == grp[None, :]) / NA_HEAD_DIM, BF16)
    gb = gate_b[head_major]
    return dict(w_all=w_all, wg_t=wg.T.astype(BF16), bd512=bd512,
                q_norm=jnp.tile(q_norm, NA_HEADS)[None], k_norm=jnp.tile(k_norm, NA_HEADS)[None],
                cos=cos, sin=sin, gate_b_t=gb[:, None])


def _rw_params(mu, w_rkv, w0, w1, w2, a0, a1, a2, g1, g2, k_k, k_a, r_k, lnx_w, lnx_b, w_o):
    grp = np.arange(LANES) // RW_HEAD_DIM
    pad_g = 256 - RW_GATE_LORA
    return dict(
        mu=jnp.concatenate([mu, jnp.zeros((2, D_MODEL), F32)], axis=0),
        w_rkv=w_rkv.astype(BF16),
        w1=jnp.concatenate([w1[0], w1[1]], axis=1).astype(BF16),
        w2=jnp.concatenate([w2[0], w2[1]], axis=0).astype(BF16), w0=w0,
        a1=jnp.concatenate([a1[0], a1[1]], axis=1).astype(BF16),
        a2=jnp.concatenate([a2[0], a2[1]], axis=0).astype(BF16), a0=a0,
        g1=jnp.pad(g1, ((0, 0), (0, pad_g))).astype(BF16), g2=jnp.pad(g2, ((0, pad_g), (0, 0))).astype(BF16),
        k_k=k_k[None], k_a=k_a[None], r_k=r_k[None], lnx_w=lnx_w[None], lnx_b=lnx_b[None],
        bd128=jnp.asarray(grp[:, None] == grp[None, :], BF16), w_o=w_o.astype(BF16))


def _mod_dict(m, n_b):
    names = ("shift_a", "scale_a", "gate_a", "shift_m", "scale_m", "gate_m")
    parts = jnp.split(m, 6, axis=-1)
    return {n: jnp.broadcast_to(p[:, None, :], (n_b, 1, D_MODEL)) for n, p in zip(names, parts)}


_PREP_NAMES = ("r", "v", "g", "a", "k0", "k1", "b0", "b1", "lw0", "lw1")


def kernel(x, c, ctx, c_ctx, ada_w, ada_b, ab_w_in, ab_gate_b, na_q_norm, na_k_norm, na_rpb, ml_head_norm, ab_w_out, rw_mu, rw_w_rkv, rw_w0, rw_w1, rw_w2, rw_a0, rw_a1, rw_a2, rw_g1, rw_g2, rw_k_k, rw_k_a, rw_r_k, rw_lnx_w, rw_lnx_b, rw_w_o, mlp_w1, mlp_w2):
    n_b, n_t, _ = x.shape
    assert ada_w.shape[0] == 2, "even (attention / mLSTM) layer followed by an odd (RWKV-7) layer"
    cond = jnp.concatenate([c, c_ctx[None], jnp.zeros((16 - n_b - 1, D_MODEL), F32)], axis=0)
    mods = _adaln(cond, ada_w, ada_b)
    w1 = mlp_w1.astype(BF16)
    w2 = mlp_w2.astype(BF16)

    m_l = _mod_dict(mods[0, :n_b], n_b)
    m_c = _mod_dict(mods[0, n_b:n_b + 1], n_b)
    prm = _ab_params(ab_w_in[0], ab_gate_b[0], na_q_norm[0], na_k_norm[0], n_t)
    qa_l, ka_l, va_l, qb_l, kb_l, vb_l, ob_l, gt_l = _ab_project(x, m_l["shift_a"], m_l["scale_a"], prm, True)
    qa_c, ka_c, va_c, qb_c, kb_c, vb_c, ob_c, gt_c = _ab_project(ctx, m_c["shift_a"], m_c["scale_a"], prm, False)
    bias = _na_bias_table(na_rpb[0], n_t // GRID_W)
    na_l = _na_attention(qa_l, ka_l, va_l, ka_c, va_c, bias)
    na_c = _ctx_attention(qa_c, ka_c, va_c)
    hlf, hlb, hcf, hcb = _mlstm(qb_l, kb_l, vb_l, gt_l, qb_c, kb_c, vb_c, gt_c)
    head_norm = ml_head_norm[0].reshape(1, ML_WIDTH)
    w_out = ab_w_out[0].astype(BF16)
    x = _ab_out_mlp(x, na_l, hlf, hlb, ob_l, m_l, head_norm, w_out, w1[0], w2[0])
    ctx = _ab_out_mlp(ctx, na_c, hcf, hcb, ob_c, m_c, head_norm, w_out, w1[0], w2[0])

    m_l = _mod_dict(mods[1, :n_b], n_b)
    m_c = _mod_dict(mods[1, n_b:n_b + 1], n_b)
    rprm = _rw_params(rw_mu[0], rw_w_rkv[0], rw_w0[0], rw_w1[0], rw_w2[0], rw_a0[0], rw_a1[0], rw_a2[0],
                      rw_g1[0], rw_g2[0], rw_k_k[0], rw_k_a[0], rw_r_k[0], rw_lnx_w[0], rw_lnx_b[0], rw_w_o[0])
    lat = dict(zip(_PREP_NAMES, _rw_prepare(x, m_l["shift_a"], m_l["scale_a"], rprm)))
    cxp = dict(zip(_PREP_NAMES, _rw_prepare(ctx, m_c["shift_a"], m_c["scale_a"], rprm)))
    yf, yb = _rw_scan(cxp, lat)
    return _rw_out_mlp(x, yf, yb, lat, m_l, rprm, w1[1], w2[1])
```

```python
import functools

import numpy as np
import jax
import jax.numpy as jnp
from jax import lax
from jax.experimental import pallas as pl
from jax.experimental.pallas import tpu as pltpu

F32 = jnp.float32
BF16 = jnp.bfloat16

D_MODEL = 1024
GRID_W = 64
NA_HEAD_DIM = 64
NA_HEADS = 8
NA_WIDTH = 512
NA_KH = 8
NA_KW = 16
ML_HEAD_DIM = 128
ML_HEADS = 4
ML_WIDTH = 512
ML_CHUNK = 128
ML_GATE_CAP = 15.0
RW_HEAD_DIM = 64
RW_HEADS = 16
RW_LORA = 64
RW_GATE_LORA = 160
RW_GN_EPS = 64e-5
D_FF = 4 * D_MODEL
ROPE_BASE = 10000.0
NORM_EPS = 1e-6

LANES = 128
GROUP_W = 256
VMEM_LIMIT = 56 * 1024 * 1024
TOKEN_TILE = 512
MLP_TILE = 512
NA_ROWS = 4
NA_KEY_ROWS = NA_ROWS + NA_KH - 1
NA_UNROLL = 4
RW_CHUNK = 64
RW_BLOCK = 256
RW_PAIR_UNROLL = 8
NEG_BIG = -1e30
RW_DECAY_SCALE = float(np.exp(-0.5))


def _cparams(sem):
    return pltpu.CompilerParams(dimension_semantics=sem, vmem_limit_bytes=VMEM_LIMIT)


def _const_spec(shape):
    nd = len(shape)
    return pl.BlockSpec(shape, lambda *_: (0,) * nd, pipeline_mode=pl.Buffered(1))


def _dot(a, b):
    return jnp.dot(a, b, preferred_element_type=F32)


def _dot_nt(a, b):
    return lax.dot_general(a, b, (((1,), (1,)), ((), ())), preferred_element_type=F32)


def _dot_tn(a, b):
    return lax.dot_general(a, b, (((0,), (0,)), ((), ())), preferred_element_type=F32)


def _split3(x):
    hi = x.astype(BF16)
    r1 = x - hi.astype(F32)
    mid = r1.astype(BF16)
    lo = (r1 - mid.astype(F32)).astype(BF16)
    return hi, mid, lo


def _dot_exact_rhs(a_bf16, x):
    hi, mid, lo = _split3(x)
    return _dot(a_bf16, hi) + _dot(a_bf16, mid) + _dot(a_bf16, lo)


def _dot_exact_lhs(x, a_bf16):
    hi, mid, lo = _split3(x)
    return _dot(hi, a_bf16) + _dot(mid, a_bf16) + _dot(lo, a_bf16)


def _dot_f32(a, b):
    ah, am, al = _split3(a)
    bh, bm, bl = _split3(b)
    return (_dot(ah, bh) + _dot(ah, bm) + _dot(am, bh)
            + _dot(ah, bl) + _dot(am, bm) + _dot(al, bh))


def _sigmoid(x):
    return 0.5 * jnp.tanh(0.5 * x) + 0.5


def _log_sigmoid(x):
    return jnp.minimum(x, 0.0) - jnp.log(1.0 + jnp.exp(-jnp.abs(x)))


def _rms(x):
    return x * lax.rsqrt(jnp.mean(x * x, axis=-1, keepdims=True) + NORM_EPS)


def _modulate(x, shift, scale):
    return _rms(x) * (1.0 + scale) + shift


def _iota(shape, dim):
    return lax.broadcasted_iota(jnp.int32, shape, dim)


def _group_sum(x, bd):
    parts = []
    for j in range(x.shape[1] // GROUP_W):
        parts.append(_dot(x[:, j * GROUP_W:(j + 1) * GROUP_W].astype(BF16), bd))
    return jnp.concatenate(parts, axis=1)


def _group_ones():
    grp = np.arange(GROUP_W) // 64
    return jnp.asarray(grp[:, None] == grp[None, :], BF16)


def _round_robin(chains, stagger=False):
    pending = list(chains)
    live = []
    while pending or live:
        if pending:
            live.extend(pending[:1] if stagger else pending)
            del pending[:1 if stagger else len(pending)]
        nxt = []
        for ch in live:
            try:
                next(ch)
                nxt.append(ch)
            except StopIteration:
                pass
        live = nxt


def _adaln_kernel(c_ref, w_ref, b_ref, o_ref):
    c = c_ref[...]
    s = c * _sigmoid(c)
    o_ref[0] = _dot_f32(s, w_ref[0]) + b_ref[0]


def _adaln(cond, ada_w, ada_b):
    depth = ada_w.shape[0]
    tn = 512
    return pl.pallas_call(
        _adaln_kernel,
        grid=(depth, 6 * D_MODEL // tn),
        in_specs=[
            pl.BlockSpec((16, D_MODEL), lambda l, j: (0, 0)),
            pl.BlockSpec((1, D_MODEL, tn), lambda l, j: (l, 0, j)),
            pl.BlockSpec((1, 1, tn), lambda l, j: (l, 0, j)),
        ],
        out_specs=pl.BlockSpec((1, 16, tn), lambda l, j: (l, 0, j)),
        out_shape=jax.ShapeDtypeStruct((depth, 16, 6 * D_MODEL), F32),
        compiler_params=_cparams(("parallel", "parallel")),
        name="adaln",
    )(cond, ada_w, ada_b.reshape(depth, 1, 6 * D_MODEL))


_C_QA, _C_KA, _C_VA, _C_QB, _C_KB, _C_VB, _C_OB = (0, 512, 1024, 1536, 2048, 2560, 3072)
_AB_COLS = 3584


def _ab_proj_kernel(x_ref, sh_ref, sc_ref, w_ref, wgt_ref, bd_ref, qn_ref, kn_ref, cos_ref, sin_ref, gbt_ref,
                    qa_ref, ka_ref, va_ref, qb_ref, kb_ref, vb_ref, ob_ref, gt_ref, *, rope):
    h = _modulate(x_ref[0], sh_ref[0], sc_ref[0]).astype(BF16)

    def proj(c0, n=512):
        return _dot(h, w_ref[:, c0:c0 + n])

    bd = bd_ref[...]

    def head_norm(y, gain):
        ms = _group_sum(y * y, bd) * (1.0 / NA_HEAD_DIM)
        return y * lax.rsqrt(ms + NORM_EPS) * gain

    qa_ref[0] = (head_norm(proj(_C_QA), qn_ref[...]) * (NA_HEAD_DIM ** -0.5)).astype(BF16)
    ka_ref[0] = head_norm(proj(_C_KA), kn_ref[...]).astype(BF16)
    va_ref[0] = proj(_C_VA).astype(BF16)
    qb = proj(_C_QB)
    kb = proj(_C_KB)
    if rope:
        cos = jnp.concatenate([cos_ref[...]] * ML_HEADS, axis=1)
        sin = jnp.concatenate([sin_ref[...]] * ML_HEADS, axis=1)
        first_half = (_iota((1, ML_WIDTH), 1) & (ML_HEAD_DIM // 4)) == 0

        def partner(y):
            return jnp.where(first_half, pltpu.roll(y, ML_WIDTH - ML_HEAD_DIM // 4, 1),
                             pltpu.roll(y, ML_HEAD_DIM // 4, 1))

        qb = qb * cos + partner(qb) * sin
        kb = kb * cos + partner(kb) * sin
    qb_ref[0] = qb.astype(BF16)
    kb_ref[0] = (kb * (ML_HEAD_DIM ** -0.5)).astype(BF16)
    vb_ref[0] = proj(_C_VB).astype(BF16)
    ob_ref[0] = proj(_C_OB)
    gt_ref[0] = _dot_nt(wgt_ref[...], h) + gbt_ref[...]


def _ab_project(x, shift, scale, prm, rope):
    n_b, n_t, _ = x.shape
    tm = min(TOKEN_TILE, n_t)
    tok = lambda w, dt: jax.ShapeDtypeStruct((n_b, n_t, w), dt)
    tok_spec = lambda w: pl.BlockSpec((1, tm, w), lambda b, i: (b, i, 0))
    mod_spec = pl.BlockSpec((1, 1, D_MODEL), lambda b, i: (b, 0, 0))
    return pl.pallas_call(
        functools.partial(_ab_proj_kernel, rope=rope),
        grid=(n_b, n_t // tm),
        in_specs=[
            tok_spec(D_MODEL), mod_spec, mod_spec,
            _const_spec((D_MODEL, _AB_COLS)), _const_spec((4 * ML_HEADS, D_MODEL)),
            _const_spec((GROUP_W, GROUP_W)), _const_spec((1, NA_WIDTH)), _const_spec((1, NA_WIDTH)),
            pl.BlockSpec((tm, LANES), lambda b, i: (i, 0)), pl.BlockSpec((tm, LANES), lambda b, i: (i, 0)),
            _const_spec((4 * ML_HEADS, 1)),
        ],
        out_specs=[tok_spec(512)] * 7 + [pl.BlockSpec((1, 4 * ML_HEADS, tm), lambda b, i: (b, 0, i))],
        out_shape=[tok(512, BF16)] * 6 + [tok(512, F32), jax.ShapeDtypeStruct((n_b, 4 * ML_HEADS, n_t), F32)],
        compiler_params=_cparams(("parallel", "parallel")),
        name="ab_proj_rope" if rope else "ab_proj",
    )(x, shift, scale, prm["w_all"], prm["wg_t"], prm["bd"], prm["q_norm"], prm["k_norm"],
      prm["cos"][:n_t], prm["sin"][:n_t], prm["gate_b_t"])


def _na_kernel(q_ref, k_ref, v_ref, kc_ref, vc_ref, tiles_ref, o_ref, bias_ref, *, row_offsets):
    n_t = q_ref.shape[1]
    n_rows = n_t // GRID_W
    rq = NA_ROWS * GRID_W
    n_blk = n_t // rq
    n_keys = NA_KEY_ROWS * GRID_W
    kc = kc_ref[0]
    vc = vc_ref[0]
    lane = _iota((1, LANES), 1)

    @pl.when(pl.program_id(1) == 0)
    def _():
        for case in range(3):
            for hh in range(2):
                for i in range(NA_ROWS):
                    for j in range(NA_KEY_ROWS):
                        bias_ref[case, hh, i * GRID_W:(i + 1) * GRID_W, j * GRID_W:(j + 1) * GRID_W] = (
                            tiles_ref[hh, int(row_offsets[case, i, j])])

    def chain(bi, hh, outs):
        kr0 = jnp.clip(bi * NA_ROWS - NA_KH // 2, 0, n_rows - NA_KEY_ROWS)
        k0 = pl.multiple_of(kr0 * GRID_W, GRID_W)
        case = jnp.where(bi == 0, 0, jnp.where(bi == n_blk - 1, 2, 1))
        q = q_ref[0, pl.ds(pl.multiple_of(bi * rq, rq), rq), :]
        in_head = (lane >= hh * NA_HEAD_DIM) & (lane < (hh + 1) * NA_HEAD_DIM)
        qm = jnp.where(in_head, q, jnp.zeros_like(q))
        s_nb = _dot_nt(qm, k_ref[0, pl.ds(k0, n_keys), :])
        s_cx = _dot_nt(qm, kc)
        yield
        s_nb = s_nb + bias_ref[case, hh]
        m = jnp.maximum(jnp.max(s_nb, axis=-1, keepdims=True), jnp.max(s_cx, axis=-1, keepdims=True))
        p_nb = jnp.exp(s_nb - m)
        p_cx = jnp.exp(s_cx - m)
        den = jnp.sum(p_nb, axis=-1, keepdims=True) + jnp.sum(p_cx, axis=-1, keepdims=True)
        o = _dot(p_nb.astype(BF16), v_ref[0, pl.ds(k0, n_keys), :]) + _dot(p_cx.astype(BF16), vc)
        yield
        outs[hh] = o / den

    def body(i, carry):
        blocks = [(NA_UNROLL * i + u, [None, None]) for u in range(NA_UNROLL)]
        _round_robin([chain(bi, hh, outs) for bi, outs in blocks for hh in range(2)], stagger=True)
        for bi, outs in blocks:
            o_ref[0, pl.ds(pl.multiple_of(bi * rq, rq), rq), :] = jnp.where(
                lane < NA_HEAD_DIM, outs[0], outs[1]).astype(o_ref.dtype)
        return carry

    lax.fori_loop(0, n_blk // NA_UNROLL, body, 0)


def _na_attention(qa, ka, va, ka_c, va_c, col_tiles):
    n_b, n_t, _ = qa.shape
    n_ctx = ka_c.shape[1]
    seq = pl.BlockSpec((1, n_t, LANES), lambda hp, b: (b, 0, hp))
    cseq = pl.BlockSpec((1, n_ctx, LANES), lambda hp, b: (b, 0, hp))
    return pl.pallas_call(
        functools.partial(_na_kernel, row_offsets=_na_row_offsets(n_t // GRID_W)),
        grid=(NA_HEADS // 2, n_b),
        in_specs=[seq, seq, seq, cseq, cseq,
                  pl.BlockSpec((2,) + col_tiles.shape[1:], lambda hp, b: (hp, 0, 0, 0))],
        out_specs=seq,
        out_shape=jax.ShapeDtypeStruct((n_b, n_t, NA_WIDTH), BF16),
        scratch_shapes=[pltpu.VMEM((3, 2, NA_ROWS * GRID_W, NA_KEY_ROWS * GRID_W), F32)],
        compiler_params=_cparams(("parallel", "arbitrary")),
        name="na_attention",
    )(qa, ka, va, ka_c, va_c, col_tiles)


def _ctx_attn_kernel(q_ref, k_ref, v_ref, o_ref):
    q = q_ref[0]
    k = k_ref[0]
    v = v_ref[0]
    lane = _iota((1, LANES), 1)
    outs = []
    for hh in range(2):
        in_head = (lane >= hh * NA_HEAD_DIM) & (lane < (hh + 1) * NA_HEAD_DIM)
        qm = jnp.where(in_head, q, jnp.zeros_like(q))
        s = _dot_nt(qm, k)
        p = jnp.exp(s - jnp.max(s, axis=-1, keepdims=True))
        outs.append(_dot(p.astype(BF16), v) / jnp.sum(p, axis=-1, keepdims=True))
    o_ref[0] = jnp.where(lane < NA_HEAD_DIM, outs[0], outs[1]).astype(o_ref.dtype)


def _ctx_attention(qa, ka, va):
    n_b, n_ctx, _ = qa.shape
    spec = pl.BlockSpec((1, n_ctx, LANES), lambda b, hp: (b, 0, hp))
    return pl.pallas_call(
        _ctx_attn_kernel,
        grid=(n_b, NA_HEADS // 2),
        in_specs=[spec, spec, spec],
        out_specs=spec,
        out_shape=jax.ShapeDtypeStruct((n_b, n_ctx, NA_WIDTH), BF16),
        compiler_params=_cparams(("parallel", "parallel")),
        name="ctx_attention",
    )(qa, ka, va)


def _na_col_tiles(rpb):
    cq = np.arange(GRID_W)[:, None]
    ck = np.arange(GRID_W)[None, :]
    cs = np.clip(cq - NA_KW // 2, 0, GRID_W - NA_KW)
    col_ok = (ck >= cs) & (ck < cs + NA_KW)
    dc = np.clip(ck - cq + NA_KW - 1, 0, 2 * NA_KW - 2)
    toep = jnp.where(col_ok[None, None], rpb[:, :, dc], NEG_BIG)
    return jnp.concatenate([toep, jnp.full_like(toep[:, :1], NEG_BIG)], axis=1)


def _na_row_offsets(n_rows):
    n_blk = n_rows // NA_ROWS
    out = []
    for bi in (0, 1, n_blk - 1):
        r0 = bi * NA_ROWS
        kr0 = int(np.clip(r0 - NA_KH // 2, 0, n_rows - NA_KEY_ROWS))
        rq = r0 + np.arange(NA_ROWS)[:, None]
        rk = kr0 + np.arange(NA_KEY_ROWS)[None, :]
        rs = np.clip(rq - NA_KH // 2, 0, n_rows - NA_KH)
        row_ok = (rk >= rs) & (rk < rs + NA_KH)
        out.append(np.where(row_ok, rk - rq + NA_KH - 1, 2 * NA_KH - 1))
    return np.stack(out)


def _mlstm_prologue(grow_ref, gcol_ref, rows_s, cols_s):
    n_l = ML_CHUNK
    ti = _iota((n_l, n_l), 0)
    si = _iota((n_l, n_l), 1)
    lower = jnp.where(si <= ti, 1.0, 0.0).astype(BF16)
    upper = jnp.where(si >= ti, 1.0, 0.0).astype(BF16)
    cap = lambda g: ML_GATE_CAP * jnp.tanh(g / ML_GATE_CAP)
    for hh in range(2):
        for d in range(2):
            ch = 2 * hh + d
            rows_s[ch, 1] = cap(grow_ref[0, hh, 2 * d])
            rows_s[ch, 0] = _dot_exact_lhs(_log_sigmoid(cap(grow_ref[0, hh, 2 * d + 1])), lower if d else upper)
            cols_s[ch, 1] = cap(gcol_ref[0, hh, 2 * d])
            cols_s[ch, 0] = _dot_exact_rhs(upper if d else lower, _log_sigmoid(cap(gcol_ref[0, hh, 2 * d + 1])))


def _mlstm_chain(ch, c, q_ref, k_ref, v_ref, rows_s, cols_s, c_s, n_s, m_s, out_ref):
    n_l = ML_CHUNK
    hh, rev = ch // 2, ch % 2 == 1
    t0 = pl.multiple_of(c * n_l, n_l)
    hs = pl.ds(hh * ML_HEAD_DIM, ML_HEAD_DIM)
    q = q_ref[0, pl.ds(t0, n_l), hs]
    k = k_ref[0, pl.ds(t0, n_l), hs]
    v = v_ref[0, pl.ds(t0, n_l), hs]
    c_st = c_s[ch]
    n_st = n_s[ch, 0:1, :]
    m_st = m_s[ch, 0:1, 0:1]
    b_row = rows_s[ch, 0, pl.ds(c, 1), :]
    i_row = rows_s[ch, 1, pl.ds(c, 1), :]
    at_c = _iota((1, LANES), 1) == c
    b_t = jnp.sum(jnp.where(at_c, cols_s[ch, 0], 0.0), axis=1, keepdims=True)
    i_col = jnp.sum(jnp.where(at_c, cols_s[ch, 1], 0.0), axis=1, keepdims=True)
    ti = _iota((n_l, n_l), 0)
    si = _iota((n_l, n_l), 1)
    mask = (si >= ti) if rev else (si <= ti)
    last = 0 if rev else n_l - 1
    log_d = jnp.where(mask, b_t - b_row + i_row, -jnp.inf)
    log_inter = b_t + m_st
    m_t = jnp.maximum(log_inter, jnp.max(log_d, axis=-1, keepdims=True))
    is_last = _iota((n_l, 1), 0) == last
    m_new = jnp.sum(jnp.where(is_last, m_t, 0.0), axis=0, keepdims=True)
    b_last = jnp.sum(jnp.where(is_last, b_t, 0.0), axis=0, keepdims=True)
    w_s = jnp.exp(b_last - b_t + i_col - m_new)
    decay = jnp.exp(b_last + m_st - m_new)
    d_mat = jnp.exp(log_d - m_t)
    w_inter = jnp.exp(log_inter - m_t)
    kw = k.astype(F32) * w_s
    s = _dot_nt(q, k)
    qc = _dot(q, c_st.astype(BF16))
    upd = _dot_tn(kw.astype(BF16), v)
    yield
    s = s * d_mat
    sv = _dot(s.astype(BF16), v)
    c_s[ch] = decay * c_st + upd
    n_s[ch] = jnp.broadcast_to(decay * n_st + jnp.sum(kw, axis=0, keepdims=True), n_s.shape[1:])
    m_s[ch] = jnp.broadcast_to(m_new, m_s.shape[1:])
    yield
    num = sv + w_inter * qc
    den = jnp.sum(s, axis=-1, keepdims=True) + w_inter * jnp.sum(q.astype(F32) * n_st, axis=-1, keepdims=True)
    out_ref[0, pl.ds(t0, n_l), hs] = num / jnp.maximum(jnp.abs(den), jnp.exp(-m_t))


def _mlstm_kernel(ql_ref, kl_ref, vl_ref, glr_ref, glc_ref, qc_ref, kc_ref, vc_ref, gcr_ref, gcc_ref,
                  hlf_ref, hlb_ref, hcf_ref, hcb_ref,
                  rows_l, cols_l, rows_c, cols_c, c_s, n_s, m_s):
    n_l = ML_CHUNK
    _mlstm_prologue(gcr_ref, gcc_ref, rows_c, cols_c)
    _mlstm_prologue(glr_ref, glc_ref, rows_l, cols_l)
    c_s[...] = jnp.zeros_like(c_s)
    n_s[...] = jnp.zeros_like(n_s)
    m_s[...] = jnp.zeros_like(m_s)

    def sweep(q_ref, k_ref, v_ref, rows_s, cols_s, outs):
        n_c = q_ref.shape[1] // n_l

        def body(j, carry):
            _round_robin([_mlstm_chain(ch, (n_c - 1 - j) if ch % 2 else j, q_ref, k_ref, v_ref, rows_s, cols_s,
                                       c_s, n_s, m_s, outs[ch % 2]) for ch in range(4)])
            return carry

        lax.fori_loop(0, n_c, body, 0)

    sweep(qc_ref, kc_ref, vc_ref, rows_c, cols_c, (hcf_ref, hcb_ref))
    sweep(ql_ref, kl_ref, vl_ref, rows_l, cols_l, (hlf_ref, hlb_ref))


def _mlstm(ql, kl, vl, gl_t, qc, kc, vc, gc_t):
    n_b, n_t, _ = ql.shape
    n_ctx = qc.shape[1]
    n_l = ML_CHUNK

    def gate_layouts(g_t, n):
        n_c = n // n_l
        rows = g_t.reshape(n_b, ML_HEADS, 4, n_c, n_l)
        cols = jnp.swapaxes(rows, 3, 4)
        pad_r = (-n_c) % 8
        return (jnp.pad(rows, ((0, 0),) * 3 + ((0, pad_r), (0, 0))),
                jnp.pad(cols, ((0, 0),) * 4 + ((0, LANES - n_c),)))

    glr, glc = gate_layouts(gl_t, n_t)
    gcr, gcc = gate_layouts(gc_t, n_ctx)
    wide = 2 * ML_HEAD_DIM

    def seq(n):
        return pl.BlockSpec((1, n, wide), lambda b, hp: (b, 0, hp))

    def gspec(a):
        return pl.BlockSpec((1, 2) + a.shape[2:], lambda b, hp: (b, hp, 0, 0, 0))

    vm = lambda *s: pltpu.VMEM(s, F32)
    return pl.pallas_call(
        _mlstm_kernel,
        grid=(n_b, ML_HEADS // 2),
        in_specs=[seq(n_t), seq(n_t), seq(n_t), gspec(glr), gspec(glc),
                  seq(n_ctx), seq(n_ctx), seq(n_ctx), gspec(gcr), gspec(gcc)],
        out_specs=[seq(n_t), seq(n_t), seq(n_ctx), seq(n_ctx)],
        out_shape=[jax.ShapeDtypeStruct((n_b, n_t, ML_WIDTH), F32)] * 2
        + [jax.ShapeDtypeStruct((n_b, n_ctx, ML_WIDTH), F32)] * 2,
        scratch_shapes=[vm(4, 2, glr.shape[3], n_l), vm(4, 2, n_l, LANES),
                        vm(4, 2, gcr.shape[3], n_l), vm(4, 2, n_l, LANES),
                        vm(4, ML_HEAD_DIM, ML_HEAD_DIM), vm(4, 8, ML_HEAD_DIM), vm(4, 8, LANES)],
        compiler_params=_cparams(("parallel", "parallel")),
        name="mlstm",
    )(ql, kl, vl, glr, glc, qc, kc, vc, gcr, gcc)


def _mlp(x1, sh, sc, gt, w1_ref, w2_ref):
    h = _modulate(x1, sh, sc).astype(BF16)
    acc = jnp.zeros_like(x1)
    n_chunk = 1024
    for c in range(D_FF // n_chunk):
        a = jnp.maximum(_dot(h, w1_ref[:, c * n_chunk:(c + 1) * n_chunk]), 0.0)
        acc = acc + _dot((a * a).astype(BF16), w2_ref[c * n_chunk:(c + 1) * n_chunk, :])
    return x1 + gt * acc


def _ab_out_kernel(x_ref, na_ref, hf_ref, hb_ref, ob_ref, ga_ref, shm_ref, scm_ref, gm_ref, hn_ref,
                   wo_ref, w1_ref, w2_ref, o_ref):
    ml = hf_ref[0] + hb_ref[0]
    parts = []
    for hh in range(ML_HEADS):
        parts.append(_rms(ml[:, hh * ML_HEAD_DIM:(hh + 1) * ML_HEAD_DIM]))
    ml = jnp.concatenate(parts, axis=1) * hn_ref[...] * _sigmoid(ob_ref[0])
    o = _dot(na_ref[0], wo_ref[0:NA_WIDTH, :]) + _dot(ml.astype(BF16), wo_ref[NA_WIDTH:, :])
    x1 = x_ref[0] + ga_ref[0] * o
    o_ref[0] = _mlp(x1, shm_ref[0], scm_ref[0], gm_ref[0], w1_ref, w2_ref)


def _ab_out_mlp(x, na, hf, hb, ob, mods, head_norm, w_out, w1, w2):
    n_b, n_t, _ = x.shape
    tm = min(MLP_TILE, n_t)
    tok = lambda w: pl.BlockSpec((1, tm, w), lambda b, i: (b, i, 0))
    mod = pl.BlockSpec((1, 1, D_MODEL), lambda b, i: (b, 0, 0))
    return pl.pallas_call(
        _ab_out_kernel,
        grid=(n_b, n_t // tm),
        in_specs=[tok(D_MODEL), tok(512), tok(512), tok(512), tok(512), mod, mod, mod, mod,
                  _const_spec((1, ML_WIDTH)), _const_spec((D_MODEL, D_MODEL)),
                  _const_spec((D_MODEL, D_FF)), _const_spec((D_FF, D_MODEL))],
        out_specs=tok(D_MODEL),
        out_shape=jax.ShapeDtypeStruct(x.shape, F32),
        compiler_params=_cparams(("parallel", "parallel")),
        name="ab_out_mlp",
    )(x, na, hf, hb, ob, mods["gate_a"], mods["shift_m"], mods["scale_m"], mods["gate_m"],
      head_norm, w_out, w1, w2)


def _rw_prep_kernel(x_ref, xp_ref, xn_ref, sh_ref, sc_ref, mu_ref, wrkv_ref, w1_ref, w2_ref, w0_ref,
                    a1_ref, a2_ref, a0_ref, g1_ref, g2_ref, kk_ref, ka_ref, bd_ref,
                    r_ref, v_ref, g_ref, a_ref, k0_ref, k1_ref, b0_ref, b1_ref, lw0_ref, lw1_ref):
    i = pl.program_id(1)
    n_i = pl.num_programs(1)
    sh = sh_ref[0]
    sc = sc_ref[0]
    h = _modulate(x_ref[0], sh, sc)
    tm = h.shape[0]
    h_before = _modulate(xp_ref[0, 7:8, :], sh, sc) * jnp.where(i > 0, 1.0, 0.0)
    h_after = _modulate(xn_ref[0, 0:1, :], sh, sc) * jnp.where(i < n_i - 1, 1.0, 0.0)
    row = _iota((tm, 1), 0)
    h_prev = jnp.where(row == 0, h_before, pltpu.roll(h, 1, 0))
    h_next = jnp.where(row == tm - 1, h_after, pltpu.roll(h, tm - 1, 0))
    xx = 0.5 * (h_prev + h_next) - h

    def mix(s):
        return (h + xx * mu_ref[s:s + 1, :]).astype(BF16)

    r = _dot(mix(0), wrkv_ref[0])
    k = _dot(mix(2), wrkv_ref[1])
    v = _dot(mix(3), wrkv_ref[2])
    lane = _iota((1, 2 * RW_LORA), 1)
    hid_w = jnp.tanh(_dot(mix(1), w1_ref[...]))
    hid_a = _dot(mix(4), a1_ref[...])
    g = _dot(_sigmoid(_dot(mix(5), g1_ref[...])).astype(BF16), g2_ref[...])
    kk = k * kk_ref[...]
    ss = _group_sum(kk * kk, bd_ref[...])
    kk = kk * jnp.minimum(lax.rsqrt(ss), 1e12)
    r_ref[0] = r.astype(r_ref.dtype)
    v_ref[0] = v.astype(v_ref.dtype)
    g_ref[0] = g.astype(g_ref.dtype)
    a_ref[0] = (-kk).astype(a_ref.dtype)
    for z, (k_out, b_out, lw_out) in enumerate(((k0_ref, b0_ref, lw0_ref), (k1_ref, b1_ref, lw1_ref))):
        in_dir = (lane >= z * RW_LORA) & (lane < (z + 1) * RW_LORA)
        w_logit = w0_ref[z:z + 1, :] + _dot(jnp.where(in_dir, hid_w, 0.0).astype(BF16), w2_ref[...])
        a = _sigmoid(a0_ref[z:z + 1, :] + _dot(jnp.where(in_dir, hid_a, 0.0).astype(BF16), a2_ref[...]))
        lw_out[0] = -RW_DECAY_SCALE * _sigmoid(w_logit)
        k_out[0] = (k * (1.0 + (a - 1.0) * ka_ref[...])).astype(k_out.dtype)
        b_out[0] = (kk * a).astype(b_out.dtype)


def _rw_prepare(x, shift, scale, prm):
    n_b, n_t, _ = x.shape
    tm = min(TOKEN_TILE, n_t)
    tok = pl.BlockSpec((1, tm, D_MODEL), lambda b, i: (b, i, 0))
    mod = pl.BlockSpec((1, 1, D_MODEL), lambda b, i: (b, 0, 0))
    n8 = n_t // 8
    prev_spec = pl.BlockSpec((1, 8, D_MODEL), lambda b, i: (b, jnp.maximum(i * (tm // 8) - 1, 0), 0))
    next_spec = pl.BlockSpec((1, 8, D_MODEL), lambda b, i: (b, jnp.minimum((i + 1) * (tm // 8), n8 - 1), 0))
    out = lambda dt: jax.ShapeDtypeStruct((n_b, n_t, D_MODEL), dt)
    return pl.pallas_call(
        _rw_prep_kernel,
        grid=(n_b, n_t // tm),
        in_specs=[tok, prev_spec, next_spec, mod, mod, _const_spec((8, D_MODEL)),
                  _const_spec((3, D_MODEL, D_MODEL)),
                  _const_spec((D_MODEL, 2 * RW_LORA)), _const_spec((2 * RW_LORA, D_MODEL)), _const_spec((2, D_MODEL)),
                  _const_spec((D_MODEL, 2 * RW_LORA)), _const_spec((2 * RW_LORA, D_MODEL)), _const_spec((2, D_MODEL)),
                  _const_spec((D_MODEL, 256)), _const_spec((256, D_MODEL)),
                  _const_spec((1, D_MODEL)), _const_spec((1, D_MODEL)), _const_spec((GROUP_W, GROUP_W))],
        out_specs=[tok] * 10,
        out_shape=[out(BF16)] * 8 + [out(F32)] * 2,
        compiler_params=_cparams(("parallel", "parallel")),
        name="rw_prepare",
    )(x, x, x, shift, scale, prm["mu"], prm["w_rkv"], prm["w1"], prm["w2"], prm["w0"],
      prm["a1"], prm["a2"], prm["a0"], prm["g1"], prm["g2"], prm["k_k"], prm["k_a"], prm["bd"])


def _rw_stage(refs, rows, stage, rev):
    r_ref, lw_ref, k_ref, v_ref, a_ref, b_ref = refs
    n_l = RW_CHUNK
    lw = lw_ref[0, rows, :]
    ti = _iota((n_l, n_l), 0)
    si = _iota((n_l, n_l), 1)
    incl = (si >= ti) if rev else (si <= ti)
    cum = _dot_exact_rhs(jnp.where(incl, 1.0, 0.0).astype(BF16), lw)
    tot = jnp.sum(lw, axis=0, keepdims=True)
    e_pos = jnp.exp(cum)
    e_neg = jnp.exp(-cum)
    e_tot = jnp.exp(tot)
    rt_s, at_s, bt_s, kt_s, bh_s, kh_s, etot_s = stage
    etot_s[...] = jnp.broadcast_to(e_tot, etot_s.shape)
    b = b_ref[0, rows, :].astype(F32)
    k = k_ref[0, rows, :].astype(F32)
    rt_s[...] = (r_ref[0, rows, :].astype(F32) * e_pos).astype(BF16)
    at_s[...] = (a_ref[0, rows, :].astype(F32) * jnp.exp(cum - lw)).astype(BF16)
    bt = b * e_neg
    kt = k * e_neg
    bt_s[...] = bt.astype(BF16)
    kt_s[...] = kt.astype(BF16)
    bh_s[...] = (bt * e_tot).astype(BF16)
    kh_s[...] = (kt * e_tot).astype(BF16)


def _rw_pair_fn(v_ref, rows, h_ref, y_ref, stage, rev):
    n_l = RW_CHUNK
    rt_s, at_s, bt_s, kt_s, bh_s, kh_s, etot_s = stage
    lane = _iota((1, LANES), 1)
    lo = lane < RW_HEAD_DIM
    hi = jnp.logical_not(lo)
    ti2 = _iota((2 * n_l, LANES), 0)
    si2 = _iota((2 * n_l, LANES), 1)
    tq = jnp.where(ti2 >= n_l, ti2 - n_l, ti2)
    sk = jnp.where(si2 >= n_l, si2 - n_l, si2)
    strict2 = (sk > tq) if rev else (sk < tq)
    mask_g = strict2 | ((ti2 >= n_l) & (sk == tq))
    eye2 = jnp.where(ti2 == si2, 1.0, 0.0)
    zeros_l = jnp.zeros((n_l, LANES), BF16)
    bd = (_iota((LANES, LANES), 0) < RW_HEAD_DIM) == (_iota((LANES, LANES), 1) < RW_HEAD_DIM)

    def pair(p):
        sl = pl.ds(p * LANES, LANES)
        rt = rt_s[:, sl]
        at = at_s[:, sl]
        bt_p = bt_s[:, sl]
        kt_p = kt_s[:, sl]
        vv = v_ref[0, rows, sl]
        h0 = h_ref[p]
        h0b = h0.astype(BF16)
        zb = jnp.zeros_like(at)
        ar = jnp.concatenate([at, rt], axis=0)
        g = _dot_nt(ar, jnp.concatenate([jnp.where(lo, bt_p, zb), jnp.where(lo, kt_p, zb),
                                         jnp.where(hi, kt_p, zb), jnp.where(hi, bt_p, zb)], axis=0))
        arh = _dot_nt(ar, h0b)
        yield
        g0 = jnp.where(mask_g, g[:, :LANES], 0.0)
        g1 = jnp.where(mask_g, g[:, LANES:], 0.0)
        a_blk = jnp.concatenate([jnp.where(lo, g0[:n_l], 0.0), jnp.where(hi, g1[:n_l], 0.0)], axis=0)
        v_lo = jnp.where(lo, vv, zb)
        v_hi = jnp.where(hi, vv, zb)
        ga = jnp.concatenate([g0[:n_l], g1[:n_l]], axis=1).astype(BF16)
        rhs = arh[:n_l] + _dot(ga, jnp.concatenate([zeros_l, v_lo, v_hi, zeros_l], axis=0))
        rhs = jnp.concatenate([jnp.where(lo, rhs, 0.0), jnp.where(hi, rhs, 0.0)], axis=0).astype(BF16)
        t_inv = eye2 + a_blk
        ab = a_blk.astype(BF16)
        a_pow = _dot(ab, ab)
        yield
        for _ in range(4):
            ab = a_pow.astype(BF16)
            both = _dot(ab, jnp.concatenate([ab, t_inv.astype(BF16)], axis=1))
            yield
            a_pow = both[:, :LANES]
            t_inv = t_inv + both[:, LANES:]
        t_inv = t_inv + _dot(a_pow.astype(BF16), t_inv.astype(BF16))
        yield
        u_st = _dot(t_inv.astype(BF16), rhs)
        yield
        u = (u_st[:n_l] + u_st[n_l:]).astype(BF16)
        if y_ref is not None:
            gr = jnp.concatenate([g0[n_l:], g1[n_l:]], axis=1).astype(BF16)
            uv_heads = jnp.concatenate([jnp.where(lo, u, zb), v_lo, v_hi, jnp.where(hi, u, zb)], axis=0)
            y_ref[0, rows, sl] = arh[n_l:] + _dot(gr, uv_heads)
        upd = _dot_tn(jnp.concatenate([u, vv], axis=0),
                      jnp.concatenate([bh_s[:, sl], kh_s[:, sl]], axis=0))
        yield
        h_ref[p] = h0 * etot_s[0:1, sl] + jnp.where(bd, upd, 0.0)

    return pair


def _rw_block(refs_f, refs_b, hf_ref, hb_ref, yf_ref, yb_ref, stage_f, stage_b):
    n_l = RW_CHUNK
    n_sub = refs_f[0].shape[1] // n_l
    n_pairs = D_MODEL // LANES

    def body(i, carry):
        rows_f = pl.ds(pl.multiple_of(i * n_l, n_l), n_l)
        rows_b = pl.ds(pl.multiple_of((n_sub - 1 - i) * n_l, n_l), n_l)
        _rw_stage(refs_f, rows_f, stage_f, False)
        _rw_stage(refs_b, rows_b, stage_b, True)
        pair_f = _rw_pair_fn(refs_f[3], rows_f, hf_ref, yf_ref, stage_f, False)
        pair_b = _rw_pair_fn(refs_b[3], rows_b, hb_ref, yb_ref, stage_b, True)
        for p0 in range(0, n_pairs, RW_PAIR_UNROLL):
            _round_robin([f(p) for p in range(p0, p0 + RW_PAIR_UNROLL) for f in (pair_f, pair_b)])
        return carry

    lax.fori_loop(0, n_sub, body, 0)


def _rw_scan_kernel(*refs, n_cc):
    ctx_f, lat_f, ctx_b, lat_b = refs[0:6], refs[6:12], refs[12:18], refs[18:24]
    yf_ref, yb_ref, hf_ref, hb_ref = refs[24:28]
    stage_f, stage_b = refs[28:35], refs[35:42]
    j = pl.program_id(1)

    @pl.when(j == 0)
    def _():
        hf_ref[...] = jnp.zeros_like(hf_ref)
        hb_ref[...] = jnp.zeros_like(hb_ref)

    @pl.when(j < n_cc)
    def _():
        _rw_block(ctx_f, ctx_b, hf_ref, hb_ref, None, None, stage_f, stage_b)

    @pl.when(j >= n_cc)
    def _():
        _rw_block(lat_f, lat_b, hf_ref, hb_ref, yf_ref, yb_ref, stage_f, stage_b)


def _rw_scan(cx, lat):
    n_b, n_t, _ = lat["r"].shape
    n_l = RW_CHUNK
    n_cc = cx["r"].shape[1] // RW_BLOCK
    n_lc = n_t // RW_BLOCK
    blk = (1, RW_BLOCK, D_MODEL)
    ctx_f = pl.BlockSpec(blk, lambda b, j: (b, jnp.minimum(j, n_cc - 1), 0))
    lat_f = pl.BlockSpec(blk, lambda b, j: (b, jnp.maximum(j - n_cc, 0), 0))
    ctx_b = pl.BlockSpec(blk, lambda b, j: (b, jnp.maximum(n_cc - 1 - j, 0), 0))
    lat_b = pl.BlockSpec(blk, lambda b, j: (b, n_lc - 1 - jnp.maximum(j - n_cc, 0), 0))
    fwd = ("r", "lw0", "k0", "v", "a", "b0")
    bwd = ("r", "lw1", "k1", "v", "a", "b1")
    args = ([cx[n] for n in fwd] + [lat[n] for n in fwd] + [cx[n] for n in bwd] + [lat[n] for n in bwd])
    stage = [pltpu.VMEM((n_l, D_MODEL), BF16)] * 6 + [pltpu.VMEM((8, D_MODEL), F32)]
    return pl.pallas_call(
        functools.partial(_rw_scan_kernel, n_cc=n_cc),
        grid=(n_b, n_cc + n_lc),
        in_specs=[ctx_f] * 6 + [lat_f] * 6 + [ctx_b] * 6 + [lat_b] * 6,
        out_specs=[lat_f, lat_b],
        out_shape=[jax.ShapeDtypeStruct((n_b, n_t, D_MODEL), F32)] * 2,
        scratch_shapes=[pltpu.VMEM((D_MODEL // LANES, LANES, LANES), F32)] * 2 + stage + stage,
        compiler_params=_cparams(("parallel", "arbitrary")),
        name="rw_scan",
    )(*args)


def _rw_out_kernel(x_ref, yf_ref, yb_ref, r_ref, k0_ref, k1_ref, v_ref, g_ref, ga_ref, shm_ref, scm_ref, gm_ref,
                   rk_ref, lnw_ref, lnb_ref, bd_ref, wo_ref, w1_ref, w2_ref, o_ref):
    bd = bd_ref[...]
    inv_n = 1.0 / RW_HEAD_DIM
    y = yf_ref[0] + yb_ref[0]
    dlt = y - _group_sum(y, bd) * inv_n
    var = _group_sum(dlt * dlt, bd) * inv_n
    yn = dlt * lax.rsqrt(var + RW_GN_EPS) * lnw_ref[...] + lnb_ref[...]
    r = r_ref[0].astype(F32)
    kd = k0_ref[0].astype(F32) + k1_ref[0].astype(F32)
    coef = _group_sum(r * kd * rk_ref[...], bd)
    o = _dot(((yn + coef * v_ref[0].astype(F32)) * g_ref[0].astype(F32)).astype(BF16), wo_ref[...])
    x1 = x_ref[0] + ga_ref[0] * o
    o_ref[0] = _mlp(x1, shm_ref[0], scm_ref[0], gm_ref[0], w1_ref, w2_ref)


def _rw_out_mlp(x, yf, yb, lat, mods, prm, w1, w2):
    n_b, n_t, _ = x.shape
    tm = min(MLP_TILE, n_t)
    tok = pl.BlockSpec((1, tm, D_MODEL), lambda b, i: (b, i, 0))
    mod = pl.BlockSpec((1, 1, D_MODEL), lambda b, i: (b, 0, 0))
    vec = _const_spec((1, D_MODEL))
    return pl.pallas_call(
        _rw_out_kernel,
        grid=(n_b, n_t // tm),
        in_specs=[tok] * 8 + [mod] * 4 + [vec, vec, vec, _const_spec((GROUP_W, GROUP_W)),
                                          _const_spec((D_MODEL, D_MODEL)),
                                          _const_spec((D_MODEL, D_FF)), _const_spec((D_FF, D_MODEL))],
        out_specs=tok,
        out_shape=jax.ShapeDtypeStruct(x.shape, F32),
        compiler_params=_cparams(("parallel", "parallel")),
        name="rw_out_mlp",
    )(x, yf, yb, lat["r"], lat["k0"], lat["k1"], lat["v"], lat["g"],
      mods["gate_a"], mods["shift_m"], mods["scale_m"], mods["gate_m"],
      prm["r_k"], prm["lnx_w"], prm["lnx_b"], prm["bd"], prm["w_o"], w1, w2)


def _ab_params(w_in, gate_b, q_norm, k_norm, n_tokens):
    n_proj = 3 * NA_WIDTH + 4 * ML_WIDTH
    head_major = np.array([4 * (j % 4) + j // 4 for j in range(4 * ML_HEADS)])
    wg = w_in[:, n_proj:][:, head_major]
    pos = np.arange(n_tokens)
    n_freq = ML_HEAD_DIM // 4
    inv_freq = (ROPE_BASE ** (-np.arange(n_freq, dtype=np.float32) / n_freq)).astype(np.float32)
    ang_r = ((pos // GRID_W).astype(np.float32)[:, None] * inv_freq).astype(np.float64)
    ang_c = ((pos % GRID_W).astype(np.float32)[:, None] * inv_freq).astype(np.float64)
    cos = np.concatenate([np.cos(ang_r)] * 2 + [np.cos(ang_c)] * 2, axis=1).astype(np.float32)
    sin = np.concatenate([-np.sin(ang_r), np.sin(ang_r), -np.sin(ang_c), np.sin(ang_c)], axis=1).astype(np.float32)
    gb = gate_b[head_major]
    return dict(w_all=w_in[:, :n_proj].astype(BF16), wg_t=wg.T.astype(BF16), bd=_group_ones(),
                q_norm=jnp.tile(q_norm, NA_HEADS)[None], k_norm=jnp.tile(k_norm, NA_HEADS)[None],
                cos=jnp.asarray(cos), sin=jnp.asarray(sin), gate_b_t=gb[:, None])


def _rw_params(mu, w_rkv, w0, w1, w2, a0, a1, a2, g1, g2, k_k, k_a, r_k, lnx_w, lnx_b, w_o):
    pad_g = 256 - RW_GATE_LORA
    return dict(
        mu=jnp.concatenate([mu, jnp.zeros((2, D_MODEL), F32)], axis=0),
        w_rkv=w_rkv.astype(BF16),
        w1=jnp.concatenate([w1[0], w1[1]], axis=1).astype(BF16),
        w2=jnp.concatenate([w2[0], w2[1]], axis=0).astype(BF16), w0=w0,
        a1=jnp.concatenate([a1[0], a1[1]], axis=1).astype(BF16),
        a2=jnp.concatenate([a2[0], a2[1]], axis=0).astype(BF16), a0=a0,
        g1=jnp.pad(g1, ((0, 0), (0, pad_g))).astype(BF16), g2=jnp.pad(g2, ((0, pad_g), (0, 0))).astype(BF16),
        k_k=k_k[None], k_a=k_a[None], r_k=r_k[None], lnx_w=lnx_w[None], lnx_b=lnx_b[None],
        bd=_group_ones(), w_o=w_o.astype(BF16))


def _mod_dict(m, n_b):
    names = ("shift_a", "scale_a", "gate_a", "shift_m", "scale_m", "gate_m")
    parts = jnp.split(m, 6, axis=-1)
    return {n: jnp.broadcast_to(p[:, None, :], (n_b, 1, D_MODEL)) for n, p in zip(names, parts)}


_PREP_NAMES = ("r", "v", "g", "a", "k0", "k1", "b0", "b1", "lw0", "lw1")


def kernel(x, c, ctx, c_ctx, ada_w, ada_b, ab_w_in, ab_gate_b, na_q_norm, na_k_norm, na_rpb, ml_head_norm, ab_w_out, rw_mu, rw_w_rkv, rw_w0, rw_w1, rw_w2, rw_a0, rw_a1, rw_a2, rw_g1, rw_g2, rw_k_k, rw_k_a, rw_r_k, rw_lnx_w, rw_lnx_b, rw_w_o, mlp_w1, mlp_w2):
    n_b, n_t, _ = x.shape
    assert ada_w.shape[0] == 2, "even (attention / mLSTM) layer followed by an odd (RWKV-7) layer"
    cond = jnp.concatenate([c, c_ctx[None], jnp.zeros((16 - n_b - 1, D_MODEL), F32)], axis=0)
    mods = _adaln(cond, ada_w, ada_b)
    w1 = mlp_w1.astype(BF16)
    w2 = mlp_w2.astype(BF16)

    m_l = _mod_dict(mods[0, :n_b], n_b)
    m_c = _mod_dict(mods[0, n_b:n_b + 1], n_b)
    prm = _ab_params(ab_w_in[0], ab_gate_b[0], na_q_norm[0], na_k_norm[0], n_t)
    qa_l, ka_l, va_l, qb_l, kb_l, vb_l, ob_l, gt_l = _ab_project(x, m_l["shift_a"], m_l["scale_a"], prm, True)
    qa_c, ka_c, va_c, qb_c, kb_c, vb_c, ob_c, gt_c = _ab_project(ctx, m_c["shift_a"], m_c["scale_a"], prm, False)
    na_l = _na_attention(qa_l, ka_l, va_l, ka_c, va_c, _na_col_tiles(na_rpb[0]))
    na_c = _ctx_attention(qa_c, ka_c, va_c)
    hlf, hlb, hcf, hcb = _mlstm(qb_l, kb_l, vb_l, gt_l, qb_c, kb_c, vb_c, gt_c)
    head_norm = ml_head_norm[0].reshape(1, ML_WIDTH)
    w_out = ab_w_out[0].astype(BF16)
    x = _ab_out_mlp(x, na_l, hlf, hlb, ob_l, m_l, head_norm, w_out, w1[0], w2[0])
    ctx = _ab_out_mlp(ctx, na_c, hcf, hcb, ob_c, m_c, head_norm, w_out, w1[0], w2[0])

    m_l = _mod_dict(mods[1, :n_b], n_b)
    m_c = _mod_dict(mods[1, n_b:n_b + 1], n_b)
    rprm = _rw_params(rw_mu[0], rw_w_rkv[0], rw_w0[0], rw_w1[0], rw_w2[0], rw_a0[0], rw_a1[0], rw_a2[0],
                      rw_g1[0], rw_g2[0], rw_k_k[0], rw_k_a[0], rw_r_k[0], rw_lnx_w[0], rw_lnx_b[0], rw_w_o[0])
    lat = dict(zip(_PREP_NAMES, _rw_prepare(x, m_l["shift_a"], m_l["scale_a"], rprm)))
    cxp = dict(zip(_PREP_NAMES, _rw_prepare(ctx, m_c["shift_a"], m_c["scale_a"], rprm)))
    yf, yb = _rw_scan(cxp, lat)
    return _rw_out_mlp(x, yf, yb, lat, m_l, rprm, w1[1], w2[1])
```

```python
import functools

import numpy as np
import jax
import jax.numpy as jnp
from jax import lax
from jax.experimental import pallas as pl
from jax.experimental.pallas import tpu as pltpu

F32 = jnp.float32
BF16 = jnp.bfloat16

D_MODEL = 1024
GRID_W = 64
NA_HEAD_DIM = 64
NA_HEADS = 8
NA_WIDTH = 512
NA_KH = 8
NA_KW = 16
ML_HEAD_DIM = 128
ML_HEADS = 4
ML_WIDTH = 512
ML_CHUNK = 128
ML_GATE_CAP = 15.0
RW_HEAD_DIM = 64
RW_HEADS = 16
RW_LORA = 64
RW_GATE_LORA = 160
RW_GN_EPS = 64e-5
D_FF = 4 * D_MODEL
ROPE_BASE = 10000.0
NORM_EPS = 1e-6

LANES = 128
GROUP_W = 256
VMEM_LIMIT = 56 * 1024 * 1024
TOKEN_TILE = 512
MLP_TILE = 512
NA_ROWS = 4
NA_KEY_ROWS = NA_ROWS + NA_KH - 1
NA_UNROLL = 4
RW_CHUNK = 64
RW_BLOCK = 256
RW_PAIR_UNROLL = 8
NEG_BIG = -1e30
RW_DECAY_SCALE = float(np.exp(-0.5))


def _cparams(sem):
    return pltpu.CompilerParams(dimension_semantics=sem, vmem_limit_bytes=VMEM_LIMIT)


def _const_spec(shape):
    nd = len(shape)
    return pl.BlockSpec(shape, lambda *_: (0,) * nd, pipeline_mode=pl.Buffered(1))


def _dot(a, b):
    return jnp.dot(a, b, preferred_element_type=F32)


def _dot_nt(a, b):
    return lax.dot_general(a, b, (((1,), (1,)), ((), ())), preferred_element_type=F32)


def _dot_tn(a, b):
    return lax.dot_general(a, b, (((0,), (0,)), ((), ())), preferred_element_type=F32)


def _split3(x):
    hi = x.astype(BF16)
    r1 = x - hi.astype(F32)
    mid = r1.astype(BF16)
    lo = (r1 - mid.astype(F32)).astype(BF16)
    return hi, mid, lo


def _dot_exact_rhs(a_bf16, x):
    hi, mid, lo = _split3(x)
    return _dot(a_bf16, hi) + _dot(a_bf16, mid) + _dot(a_bf16, lo)


def _dot_exact_lhs(x, a_bf16):
    hi, mid, lo = _split3(x)
    return _dot(hi, a_bf16) + _dot(mid, a_bf16) + _dot(lo, a_bf16)


def _dot_f32(a, b):
    ah, am, al = _split3(a)
    bh, bm, bl = _split3(b)
    return (_dot(ah, bh) + _dot(ah, bm) + _dot(am, bh)
            + _dot(ah, bl) + _dot(am, bm) + _dot(al, bh))


def _sigmoid(x):
    return 0.5 * jnp.tanh(0.5 * x) + 0.5


def _log_sigmoid(x):
    return jnp.minimum(x, 0.0) - jnp.log(1.0 + jnp.exp(-jnp.abs(x)))


def _rms(x):
    return x * lax.rsqrt(jnp.mean(x * x, axis=-1, keepdims=True) + NORM_EPS)


def _modulate(x, shift, scale):
    return _rms(x) * (1.0 + scale) + shift


def _iota(shape, dim):
    return lax.broadcasted_iota(jnp.int32, shape, dim)


def _group_sum(x, bd):
    parts = []
    for j in range(x.shape[1] // GROUP_W):
        parts.append(_dot(x[:, j * GROUP_W:(j + 1) * GROUP_W].astype(BF16), bd))
    return jnp.concatenate(parts, axis=1)


def _group_ones():
    grp = np.arange(GROUP_W) // 64
    return jnp.asarray(grp[:, None] == grp[None, :], BF16)


def _round_robin(chains, stagger=False):
    pending = list(chains)
    live = []
    while pending or live:
        if pending:
            live.extend(pending[:1] if stagger else pending)
            del pending[:1 if stagger else len(pending)]
        nxt = []
        for ch in live:
            try:
                next(ch)
                nxt.append(ch)
            except StopIteration:
                pass
        live = nxt


def _adaln_kernel(c_ref, w_ref, b_ref, o_ref):
    c = c_ref[...]
    s = c * _sigmoid(c)
    o_ref[0] = _dot_f32(s, w_ref[0]) + b_ref[0]


def _adaln(cond, ada_w, ada_b):
    depth = ada_w.shape[0]
    tn = 512
    return pl.pallas_call(
        _adaln_kernel,
        grid=(depth, 6 * D_MODEL // tn),
        in_specs=[
            pl.BlockSpec((16, D_MODEL), lambda l, j: (0, 0)),
            pl.BlockSpec((1, D_MODEL, tn), lambda l, j: (l, 0, j)),
            pl.BlockSpec((1, 1, tn), lambda l, j: (l, 0, j)),
        ],
        out_specs=pl.BlockSpec((1, 16, tn), lambda l, j: (l, 0, j)),
        out_shape=jax.ShapeDtypeStruct((depth, 16, 6 * D_MODEL), F32),
        compiler_params=_cparams(("parallel", "parallel")),
        name="adaln",
    )(cond, ada_w, ada_b.reshape(depth, 1, 6 * D_MODEL))


_C_QA, _C_KA, _C_VA, _C_QB, _C_KB, _C_VB, _C_OB = (0, 512, 1024, 1536, 2048, 2560, 3072)
_AB_COLS = 3584


def _ab_proj_kernel(x_ref, sh_ref, sc_ref, w_ref, wgt_ref, bd_ref, qn_ref, kn_ref, cos_ref, sin_ref, gbt_ref,
                    qa_ref, ka_ref, va_ref, qb_ref, kb_ref, vb_ref, ob_ref, gt_ref, *, rope):
    h = _modulate(x_ref[0], sh_ref[0], sc_ref[0]).astype(BF16)

    def proj(c0, n=512):
        return _dot(h, w_ref[:, c0:c0 + n])

    bd = bd_ref[...]

    def head_norm(y, gain):
        ms = _group_sum(y * y, bd) * (1.0 / NA_HEAD_DIM)
        return y * lax.rsqrt(ms + NORM_EPS) * gain

    qa_ref[0] = (head_norm(proj(_C_QA), qn_ref[...]) * (NA_HEAD_DIM ** -0.5)).astype(BF16)
    ka_ref[0] = head_norm(proj(_C_KA), kn_ref[...]).astype(BF16)
    va_ref[0] = proj(_C_VA).astype(BF16)
    qb = proj(_C_QB)
    kb = proj(_C_KB)
    if rope:
        cos = jnp.concatenate([cos_ref[...]] * ML_HEADS, axis=1)
        sin = jnp.concatenate([sin_ref[...]] * ML_HEADS, axis=1)
        first_half = (_iota((1, ML_WIDTH), 1) & (ML_HEAD_DIM // 4)) == 0

        def partner(y):
            return jnp.where(first_half, pltpu.roll(y, ML_WIDTH - ML_HEAD_DIM // 4, 1),
                             pltpu.roll(y, ML_HEAD_DIM // 4, 1))

        qb = qb * cos + partner(qb) * sin
        kb = kb * cos + partner(kb) * sin
    qb_ref[0] = qb.astype(BF16)
    kb_ref[0] = (kb * (ML_HEAD_DIM ** -0.5)).astype(BF16)
    vb_ref[0] = proj(_C_VB).astype(BF16)
    ob_ref[0] = proj(_C_OB)
    gt_ref[0] = _dot_nt(wgt_ref[...], h) + gbt_ref[...]


def _ab_project(x, shift, scale, prm, rope):
    n_b, n_t, _ = x.shape
    tm = min(TOKEN_TILE, n_t)
    tok = lambda w, dt: jax.ShapeDtypeStruct((n_b, n_t, w), dt)
    tok_spec = lambda w: pl.BlockSpec((1, tm, w), lambda b, i: (b, i, 0))
    mod_spec = pl.BlockSpec((1, 1, D_MODEL), lambda b, i: (b, 0, 0))
    return pl.pallas_call(
        functools.partial(_ab_proj_kernel, rope=rope),
        grid=(n_b, n_t // tm),
        in_specs=[
            tok_spec(D_MODEL), mod_spec, mod_spec,
            _const_spec((D_MODEL, _AB_COLS)), _const_spec((4 * ML_HEADS, D_MODEL)),
            _const_spec((GROUP_W, GROUP_W)), _const_spec((1, NA_WIDTH)), _const_spec((1, NA_WIDTH)),
            pl.BlockSpec((tm, LANES), lambda b, i: (i, 0)), pl.BlockSpec((tm, LANES), lambda b, i: (i, 0)),
            _const_spec((4 * ML_HEADS, 1)),
        ],
        out_specs=[tok_spec(512)] * 7 + [pl.BlockSpec((1, 4 * ML_HEADS, tm), lambda b, i: (b, 0, i))],
        out_shape=[tok(512, BF16)] * 6 + [tok(512, F32), jax.ShapeDtypeStruct((n_b, 4 * ML_HEADS, n_t), F32)],
        compiler_params=_cparams(("parallel", "parallel")),
        name="ab_proj_rope" if rope else "ab_proj",
    )(x, shift, scale, prm["w_all"], prm["wg_t"], prm["bd"], prm["q_norm"], prm["k_norm"],
      prm["cos"][:n_t], prm["sin"][:n_t], prm["gate_b_t"])


def _na_kernel(q_ref, k_ref, v_ref, kc_ref, vc_ref, tiles_ref, o_ref, bias_ref, *, row_offsets):
    n_t = q_ref.shape[1]
    n_rows = n_t // GRID_W
    rq = NA_ROWS * GRID_W
    n_blk = n_t // rq
    n_keys = NA_KEY_ROWS * GRID_W
    kc = kc_ref[0]
    vc = vc_ref[0]
    lane = _iota((1, LANES), 1)

    @pl.when(pl.program_id(1) == 0)
    def _():
        for case in range(3):
            for hh in range(2):
                for i in range(NA_ROWS):
                    for j in range(NA_KEY_ROWS):
                        bias_ref[case, hh, i * GRID_W:(i + 1) * GRID_W, j * GRID_W:(j + 1) * GRID_W] = (
                            tiles_ref[hh, int(row_offsets[case, i, j])])

    def chain(bi, hh, outs):
        kr0 = jnp.clip(bi * NA_ROWS - NA_KH // 2, 0, n_rows - NA_KEY_ROWS)
        k0 = pl.multiple_of(kr0 * GRID_W, GRID_W)
        case = jnp.where(bi == 0, 0, jnp.where(bi == n_blk - 1, 2, 1))
        q = q_ref[0, pl.ds(pl.multiple_of(bi * rq, rq), rq), :]
        in_head = (lane >= hh * NA_HEAD_DIM) & (lane < (hh + 1) * NA_HEAD_DIM)
        qm = jnp.where(in_head, q, jnp.zeros_like(q))
        s_nb = _dot_nt(qm, k_ref[0, pl.ds(k0, n_keys), :])
        s_cx = _dot_nt(qm, kc)
        yield
        s_nb = s_nb + bias_ref[case, hh]
        m = jnp.maximum(jnp.max(s_nb, axis=-1, keepdims=True), jnp.max(s_cx, axis=-1, keepdims=True))
        p_nb = jnp.exp(s_nb - m)
        p_cx = jnp.exp(s_cx - m)
        den = jnp.sum(p_nb, axis=-1, keepdims=True) + jnp.sum(p_cx, axis=-1, keepdims=True)
        o = _dot(p_nb.astype(BF16), v_ref[0, pl.ds(k0, n_keys), :]) + _dot(p_cx.astype(BF16), vc)
        yield
        outs[hh] = o / den

    def body(i, carry):
        blocks = [(NA_UNROLL * i + u, [None, None]) for u in range(NA_UNROLL)]
        _round_robin([chain(bi, hh, outs) for bi, outs in blocks for hh in range(2)], stagger=True)
        for bi, outs in blocks:
            o_ref[0, pl.ds(pl.multiple_of(bi * rq, rq), rq), :] = jnp.where(
                lane < NA_HEAD_DIM, outs[0], outs[1]).astype(o_ref.dtype)
        return carry

    lax.fori_loop(0, n_blk // NA_UNROLL, body, 0)


def _na_attention(qa, ka, va, ka_c, va_c, col_tiles):
    n_b, n_t, _ = qa.shape
    n_ctx = ka_c.shape[1]
    seq = pl.BlockSpec((1, n_t, LANES), lambda hp, b: (b, 0, hp))
    cseq = pl.BlockSpec((1, n_ctx, LANES), lambda hp, b: (b, 0, hp))
    return pl.pallas_call(
        functools.partial(_na_kernel, row_offsets=_na_row_offsets(n_t // GRID_W)),
        grid=(NA_HEADS // 2, n_b),
        in_specs=[seq, seq, seq, cseq, cseq,
                  pl.BlockSpec((2,) + col_tiles.shape[1:], lambda hp, b: (hp, 0, 0, 0))],
        out_specs=seq,
        out_shape=jax.ShapeDtypeStruct((n_b, n_t, NA_WIDTH), BF16),
        scratch_shapes=[pltpu.VMEM((3, 2, NA_ROWS * GRID_W, NA_KEY_ROWS * GRID_W), F32)],
        compiler_params=_cparams(("parallel", "arbitrary")),
        name="na_attention",
    )(qa, ka, va, ka_c, va_c, col_tiles)


def _ctx_attn_kernel(q_ref, k_ref, v_ref, o_ref):
    q = q_ref[0]
    k = k_ref[0]
    v = v_ref[0]
    lane = _iota((1, LANES), 1)
    outs = []
    for hh in range(2):
        in_head = (lane >= hh * NA_HEAD_DIM) & (lane < (hh + 1) * NA_HEAD_DIM)
        qm = jnp.where(in_head, q, jnp.zeros_like(q))
        s = _dot_nt(qm, k)
        p = jnp.exp(s - jnp.max(s, axis=-1, keepdims=True))
        outs.append(_dot(p.astype(BF16), v) / jnp.sum(p, axis=-1, keepdims=True))
    o_ref[0] = jnp.where(lane < NA_HEAD_DIM, outs[0], outs[1]).astype(o_ref.dtype)


def _ctx_attention(qa, ka, va):
    n_b, n_ctx, _ = qa.shape
    spec = pl.BlockSpec((1, n_ctx, LANES), lambda b, hp: (b, 0, hp))
    return pl.pallas_call(
        _ctx_attn_kernel,
        grid=(n_b, NA_HEADS // 2),
        in_specs=[spec, spec, spec],
        out_specs=spec,
        out_shape=jax.ShapeDtypeStruct((n_b, n_ctx, NA_WIDTH), BF16),
        compiler_params=_cparams(("parallel", "parallel")),
        name="ctx_attention",
    )(qa, ka, va)


def _na_col_tiles(rpb):
    cq = np.arange(GRID_W)[:, None]
    ck = np.arange(GRID_W)[None, :]
    cs = np.clip(cq - NA_KW // 2, 0, GRID_W - NA_KW)
    col_ok = (ck >= cs) & (ck < cs + NA_KW)
    pad = jnp.pad(rpb, ((0, 0), (0, 0), (GRID_W, GRID_W)))
    lo = GRID_W + NA_KW - 1
    toep = jnp.stack([pad[:, :, lo - c:lo - c + GRID_W] for c in range(GRID_W)], axis=2)
    toep = jnp.where(col_ok[None, None], toep, NEG_BIG)
    return jnp.concatenate([toep, jnp.full_like(toep[:, :1], NEG_BIG)], axis=1)


def _na_row_offsets(n_rows):
    n_blk = n_rows // NA_ROWS
    out = []
    for bi in (0, 1, n_blk - 1):
        r0 = bi * NA_ROWS
        kr0 = int(np.clip(r0 - NA_KH // 2, 0, n_rows - NA_KEY_ROWS))
        rq = r0 + np.arange(NA_ROWS)[:, None]
        rk = kr0 + np.arange(NA_KEY_ROWS)[None, :]
        rs = np.clip(rq - NA_KH // 2, 0, n_rows - NA_KH)
        row_ok = (rk >= rs) & (rk < rs + NA_KH)
        out.append(np.where(row_ok, rk - rq + NA_KH - 1, 2 * NA_KH - 1))
    return np.stack(out)


def _mlstm_prologue(grow_ref, gcol_ref, rows_s, cols_s):
    n_l = ML_CHUNK
    ti = _iota((n_l, n_l), 0)
    si = _iota((n_l, n_l), 1)
    lower = jnp.where(si <= ti, 1.0, 0.0).astype(BF16)
    upper = jnp.where(si >= ti, 1.0, 0.0).astype(BF16)
    cap = lambda g: ML_GATE_CAP * jnp.tanh(g / ML_GATE_CAP)
    for hh in range(2):
        for d in range(2):
            ch = 2 * hh + d
            rows_s[ch] = _dot_exact_lhs(_log_sigmoid(cap(grow_ref[0, hh, 2 * d + 1])), lower if d else upper)
            b_col = _dot_exact_rhs(upper if d else lower, _log_sigmoid(cap(gcol_ref[0, hh, 2 * d + 1])))
            cols_s[ch] = cap(gcol_ref[0, hh, 2 * d]) - b_col


def _mlstm_chain(ch, c, q_ref, k_ref, v_ref, rows_s, cols_s, c_s, n_s, m_s, out_ref):
    n_l = ML_CHUNK
    hh, rev = ch // 2, ch % 2 == 1
    t0 = pl.multiple_of(c * n_l, n_l)
    hs = pl.ds(hh * ML_HEAD_DIM, ML_HEAD_DIM)
    q = q_ref[0, pl.ds(t0, n_l), hs]
    k = k_ref[0, pl.ds(t0, n_l), hs]
    v = v_ref[0, pl.ds(t0, n_l), hs]
    c_st = c_s[ch]
    n_st = n_s[ch]
    m_st = m_s[ch, 0:1, 0:1]
    b_row = rows_s[ch, pl.ds(c, 1), :]
    lane = _iota((1, LANES), 1)
    ib_col = jnp.sum(jnp.where(lane == c, cols_s[ch], 0.0), axis=1, keepdims=True)
    is_last = lane == (0 if rev else n_l - 1)
    b_last = jnp.sum(jnp.where(is_last, b_row, 0.0), axis=1, keepdims=True)
    kcn = _dot_nt(jnp.concatenate([k, c_st.astype(BF16), n_st.astype(BF16)], axis=0), q)
    yield
    si = _iota((n_l, n_l), 0)
    ti = _iota((n_l, n_l), 1)
    mask = (si >= ti) if rev else (si <= ti)
    log_d = jnp.where(mask, b_row + ib_col, -jnp.inf)
    log_inter = b_row + m_st
    m_t = jnp.maximum(log_inter, jnp.max(log_d, axis=0, keepdims=True))
    m_new = jnp.sum(jnp.where(is_last, m_t, 0.0), axis=1, keepdims=True)
    w_s = jnp.exp(ib_col + (b_last - m_new))
    decay = jnp.exp(b_last + m_st - m_new)
    w_inter = jnp.exp(log_inter - m_t)
    s_w = kcn[:n_l] * jnp.exp(log_d - m_t)
    kw = k.astype(F32) * w_s
    both = _dot_tn(v, jnp.concatenate([s_w.astype(BF16), kw.astype(BF16)], axis=1))
    c_s[ch] = decay * c_st + both[:, n_l:]
    n_s[ch] = jnp.broadcast_to(decay * n_st[0:1] + jnp.sum(kw, axis=0, keepdims=True), n_s.shape[1:])
    m_s[ch] = jnp.broadcast_to(m_new, m_s.shape[1:])
    yield
    num_t = both[:, :n_l] + w_inter * kcn[n_l:2 * n_l]
    den = jnp.sum(s_w, axis=0, keepdims=True) + w_inter * kcn[2 * n_l:2 * n_l + 1]
    h_t = num_t / jnp.maximum(jnp.abs(den), jnp.exp(-m_t))
    out_ref[0, pl.ds(t0, n_l), hs] = h_t.T


def _mlstm_kernel(ql_ref, kl_ref, vl_ref, glr_ref, glc_ref, qc_ref, kc_ref, vc_ref, gcr_ref, gcc_ref,
                  hlf_ref, hlb_ref, hcf_ref, hcb_ref,
                  rows_l, cols_l, rows_c, cols_c, c_s, n_s, m_s):
    n_l = ML_CHUNK
    _mlstm_prologue(gcr_ref, gcc_ref, rows_c, cols_c)
    _mlstm_prologue(glr_ref, glc_ref, rows_l, cols_l)
    c_s[...] = jnp.zeros_like(c_s)
    n_s[...] = jnp.zeros_like(n_s)
    m_s[...] = jnp.zeros_like(m_s)

    def sweep(q_ref, k_ref, v_ref, rows_s, cols_s, outs):
        n_c = q_ref.shape[1] // n_l

        def body(j, carry):
            _round_robin([_mlstm_chain(ch, (n_c - 1 - j) if ch % 2 else j, q_ref, k_ref, v_ref, rows_s, cols_s,
                                       c_s, n_s, m_s, outs[ch % 2]) for ch in range(4)])
            return carry

        lax.fori_loop(0, n_c, body, 0)

    sweep(qc_ref, kc_ref, vc_ref, rows_c, cols_c, (hcf_ref, hcb_ref))
    sweep(ql_ref, kl_ref, vl_ref, rows_l, cols_l, (hlf_ref, hlb_ref))


def _mlstm(ql, kl, vl, gl_t, qc, kc, vc, gc_t):
    n_b, n_t, _ = ql.shape
    n_ctx = qc.shape[1]
    n_l = ML_CHUNK

    def gate_layouts(g_t, n):
        n_c = n // n_l
        rows = g_t.reshape(n_b, ML_HEADS, 4, n_c, n_l)
        cols = jnp.swapaxes(rows, 3, 4)
        pad_r = (-n_c) % 8
        return (jnp.pad(rows, ((0, 0),) * 3 + ((0, pad_r), (0, 0))),
                jnp.pad(cols, ((0, 0),) * 4 + ((0, LANES - n_c),)))

    glr, glc = gate_layouts(gl_t, n_t)
    gcr, gcc = gate_layouts(gc_t, n_ctx)
    wide = 2 * ML_HEAD_DIM

    def seq(n):
        return pl.BlockSpec((1, n, wide), lambda b, hp: (b, 0, hp))

    def gspec(a):
        return pl.BlockSpec((1, 2) + a.shape[2:], lambda b, hp: (b, hp, 0, 0, 0))

    vm = lambda *s: pltpu.VMEM(s, F32)
    return pl.pallas_call(
        _mlstm_kernel,
        grid=(n_b, ML_HEADS // 2),
        in_specs=[seq(n_t), seq(n_t), seq(n_t), gspec(glr), gspec(glc),
                  seq(n_ctx), seq(n_ctx), seq(n_ctx), gspec(gcr), gspec(gcc)],
        out_specs=[seq(n_t), seq(n_t), seq(n_ctx), seq(n_ctx)],
        out_shape=[jax.ShapeDtypeStruct((n_b, n_t, ML_WIDTH), F32)] * 2
        + [jax.ShapeDtypeStruct((n_b, n_ctx, ML_WIDTH), F32)] * 2,
        scratch_shapes=[vm(4, glr.shape[3], n_l), vm(4, n_l, LANES),
                        vm(4, gcr.shape[3], n_l), vm(4, n_l, LANES),
                        vm(4, ML_HEAD_DIM, ML_HEAD_DIM), vm(4, 16, ML_HEAD_DIM), vm(4, 8, LANES)],
        compiler_params=_cparams(("parallel", "parallel")),
        name="mlstm",
    )(ql, kl, vl, glr, glc, qc, kc, vc, gcr, gcc)


def _mlp(x1, sh, sc, gt, w1_ref, w2_ref):
    h = _modulate(x1, sh, sc).astype(BF16)
    acc = jnp.zeros_like(x1)
    n_chunk = 1024
    for c in range(D_FF // n_chunk):
        a = jnp.maximum(_dot(h, w1_ref[:, c * n_chunk:(c + 1) * n_chunk]), 0.0)
        acc = acc + _dot((a * a).astype(BF16), w2_ref[c * n_chunk:(c + 1) * n_chunk, :])
    return x1 + gt * acc


def _ab_out_kernel(x_ref, na_ref, hf_ref, hb_ref, ob_ref, ga_ref, shm_ref, scm_ref, gm_ref, hn_ref,
                   wo_ref, w1_ref, w2_ref, o_ref):
    ml = hf_ref[0] + hb_ref[0]
    parts = []
    for hh in range(ML_HEADS):
        parts.append(_rms(ml[:, hh * ML_HEAD_DIM:(hh + 1) * ML_HEAD_DIM]))
    ml = jnp.concatenate(parts, axis=1) * hn_ref[...] * _sigmoid(ob_ref[0])
    o = _dot(na_ref[0], wo_ref[0:NA_WIDTH, :]) + _dot(ml.astype(BF16), wo_ref[NA_WIDTH:, :])
    x1 = x_ref[0] + ga_ref[0] * o
    o_ref[0] = _mlp(x1, shm_ref[0], scm_ref[0], gm_ref[0], w1_ref, w2_ref)


def _ab_out_mlp(x, na, hf, hb, ob, mods, head_norm, w_out, w1, w2):
    n_b, n_t, _ = x.shape
    tm = min(MLP_TILE, n_t)
    tok = lambda w: pl.BlockSpec((1, tm, w), lambda b, i: (b, i, 0))
    mod = pl.BlockSpec((1, 1, D_MODEL), lambda b, i: (b, 0, 0))
    return pl.pallas_call(
        _ab_out_kernel,
        grid=(n_b, n_t // tm),
        in_specs=[tok(D_MODEL), tok(512), tok(512), tok(512), tok(512), mod, mod, mod, mod,
                  _const_spec((1, ML_WIDTH)), _const_spec((D_MODEL, D_MODEL)),
                  _const_spec((D_MODEL, D_FF)), _const_spec((D_FF, D_MODEL))],
        out_specs=tok(D_MODEL),
        out_shape=jax.ShapeDtypeStruct(x.shape, F32),
        compiler_params=_cparams(("parallel", "parallel")),
        name="ab_out_mlp",
    )(x, na, hf, hb, ob, mods["gate_a"], mods["shift_m"], mods["scale_m"], mods["gate_m"],
      head_norm, w_out, w1, w2)


def _rw_prep_kernel(x_ref, xp_ref, xn_ref, sh_ref, sc_ref, mu_ref, wrkv_ref, w1_ref, w2_ref, w0_ref,
                    a1_ref, a2_ref, a0_ref, g1_ref, g2_ref, kk_ref, ka_ref, bd_ref,
                    r_ref, v_ref, g_ref, a_ref, k0_ref, k1_ref, b0_ref, b1_ref, lw0_ref, lw1_ref):
    i = pl.program_id(1)
    n_i = pl.num_programs(1)
    sh = sh_ref[0]
    sc = sc_ref[0]
    h = _modulate(x_ref[0], sh, sc)
    tm = h.shape[0]
    h_before = _modulate(xp_ref[0, 7:8, :], sh, sc) * jnp.where(i > 0, 1.0, 0.0)
    h_after = _modulate(xn_ref[0, 0:1, :], sh, sc) * jnp.where(i < n_i - 1, 1.0, 0.0)
    row = _iota((tm, 1), 0)
    h_prev = jnp.where(row == 0, h_before, pltpu.roll(h, 1, 0))
    h_next = jnp.where(row == tm - 1, h_after, pltpu.roll(h, tm - 1, 0))
    xx = 0.5 * (h_prev + h_next) - h

    def mix(s):
        return (h + xx * mu_ref[s:s + 1, :]).astype(BF16)

    r = _dot(mix(0), wrkv_ref[0])
    k = _dot(mix(2), wrkv_ref[1])
    v = _dot(mix(3), wrkv_ref[2])
    lane = _iota((1, 2 * RW_LORA), 1)
    hid_w = jnp.tanh(_dot(mix(1), w1_ref[...]))
    hid_a = _dot(mix(4), a1_ref[...])
    g = _dot(_sigmoid(_dot(mix(5), g1_ref[...])).astype(BF16), g2_ref[...])
    kk = k * kk_ref[...]
    ss = _group_sum(kk * kk, bd_ref[...])
    kk = kk * jnp.minimum(lax.rsqrt(ss), 1e12)
    r_ref[0] = r.astype(r_ref.dtype)
    v_ref[0] = v.astype(v_ref.dtype)
    g_ref[0] = g.astype(g_ref.dtype)
    a_ref[0] = (-kk).astype(a_ref.dtype)
    for z, (k_out, b_out, lw_out) in enumerate(((k0_ref, b0_ref, lw0_ref), (k1_ref, b1_ref, lw1_ref))):
        in_dir = (lane >= z * RW_LORA) & (lane < (z + 1) * RW_LORA)
        w_logit = w0_ref[z:z + 1, :] + _dot(jnp.where(in_dir, hid_w, 0.0).astype(BF16), w2_ref[...])
        a = _sigmoid(a0_ref[z:z + 1, :] + _dot(jnp.where(in_dir, hid_a, 0.0).astype(BF16), a2_ref[...]))
        lw_out[0] = -RW_DECAY_SCALE * _sigmoid(w_logit)
        k_out[0] = (k * (1.0 + (a - 1.0) * ka_ref[...])).astype(k_out.dtype)
        b_out[0] = (kk * a).astype(b_out.dtype)


def _rw_prepare(x, shift, scale, prm):
    n_b, n_t, _ = x.shape
    tm = min(TOKEN_TILE, n_t)
    tok = pl.BlockSpec((1, tm, D_MODEL), lambda b, i: (b, i, 0))
    mod = pl.BlockSpec((1, 1, D_MODEL), lambda b, i: (b, 0, 0))
    n8 = n_t // 8
    prev_spec = pl.BlockSpec((1, 8, D_MODEL), lambda b, i: (b, jnp.maximum(i * (tm // 8) - 1, 0), 0))
    next_spec = pl.BlockSpec((1, 8, D_MODEL), lambda b, i: (b, jnp.minimum((i + 1) * (tm // 8), n8 - 1), 0))
    out = lambda dt: jax.ShapeDtypeStruct((n_b, n_t, D_MODEL), dt)
    return pl.pallas_call(
        _rw_prep_kernel,
        grid=(n_b, n_t // tm),
        in_specs=[tok, prev_spec, next_spec, mod, mod, _const_spec((8, D_MODEL)),
                  _const_spec((3, D_MODEL, D_MODEL)),
                  _const_spec((D_MODEL, 2 * RW_LORA)), _const_spec((2 * RW_LORA, D_MODEL)), _const_spec((2, D_MODEL)),
                  _const_spec((D_MODEL, 2 * RW_LORA)), _const_spec((2 * RW_LORA, D_MODEL)), _const_spec((2, D_MODEL)),
                  _const_spec((D_MODEL, 256)), _const_spec((256, D_MODEL)),
                  _const_spec((1, D_MODEL)), _const_spec((1, D_MODEL)), _const_spec((GROUP_W, GROUP_W))],
        out_specs=[tok] * 10,
        out_shape=[out(BF16)] * 8 + [out(F32)] * 2,
        compiler_params=_cparams(("parallel", "parallel")),
        name="rw_prepare",
    )(x, x, x, shift, scale, prm["mu"], prm["w_rkv"], prm["w1"], prm["w2"], prm["w0"],
      prm["a1"], prm["a2"], prm["a0"], prm["g1"], prm["g2"], prm["k_k"], prm["k_a"], prm["bd"])


def _rw_stage(refs, rows, stage, rev):
    r_ref, lw_ref, k_ref, v_ref, a_ref, b_ref = refs
    n_l = RW_CHUNK
    lw = lw_ref[0, rows, :]
    ti = _iota((n_l, n_l), 0)
    si = _iota((n_l, n_l), 1)
    incl = (si >= ti) if rev else (si <= ti)
    cum = _dot_exact_rhs(jnp.where(incl, 1.0, 0.0).astype(BF16), lw)
    tot = jnp.sum(lw, axis=0, keepdims=True)
    e_pos = jnp.exp(cum)
    e_neg = jnp.exp(-cum)
    e_tot = jnp.exp(tot)
    rt_s, at_s, bt_s, kt_s, bh_s, kh_s, etot_s = stage
    etot_s[...] = jnp.broadcast_to(e_tot, etot_s.shape)
    b = b_ref[0, rows, :].astype(F32)
    k = k_ref[0, rows, :].astype(F32)
    rt_s[...] = (r_ref[0, rows, :].astype(F32) * e_pos).astype(BF16)
    at_s[...] = (a_ref[0, rows, :].astype(F32) * jnp.exp(cum - lw)).astype(BF16)
    bt = b * e_neg
    kt = k * e_neg
    bt_s[...] = bt.astype(BF16)
    kt_s[...] = kt.astype(BF16)
    bh_s[...] = (bt * e_tot).astype(BF16)
    kh_s[...] = (kt * e_tot).astype(BF16)


def _rw_pair_fn(v_ref, rows, h_ref, y_ref, stage, rev):
    n_l = RW_CHUNK
    rt_s, at_s, bt_s, kt_s, bh_s, kh_s, etot_s = stage
    lane = _iota((1, LANES), 1)
    lo = lane < RW_HEAD_DIM
    hi = jnp.logical_not(lo)
    ti2 = _iota((2 * n_l, LANES), 0)
    si2 = _iota((2 * n_l, LANES), 1)
    tq = jnp.where(ti2 >= n_l, ti2 - n_l, ti2)
    sk = jnp.where(si2 >= n_l, si2 - n_l, si2)
    strict2 = (sk > tq) if rev else (sk < tq)
    mask_g = strict2 | ((ti2 >= n_l) & (sk == tq))
    eye2 = jnp.where(ti2 == si2, 1.0, 0.0)
    zeros_l = jnp.zeros((n_l, LANES), BF16)
    bd = (_iota((LANES, LANES), 0) < RW_HEAD_DIM) == (_iota((LANES, LANES), 1) < RW_HEAD_DIM)

    def pair(p):
        sl = pl.ds(p * LANES, LANES)
        rt = rt_s[:, sl]
        at = at_s[:, sl]
        bt_p = bt_s[:, sl]
        kt_p = kt_s[:, sl]
        vv = v_ref[0, rows, sl]
        h0 = h_ref[p]
        h0b = h0.astype(BF16)
        zb = jnp.zeros_like(at)
        ar = jnp.concatenate([at, rt], axis=0)
        g = _dot_nt(ar, jnp.concatenate([jnp.where(lo, bt_p, zb), jnp.where(lo, kt_p, zb),
                                         jnp.where(hi, kt_p, zb), jnp.where(hi, bt_p, zb)], axis=0))
        arh = _dot_nt(ar, h0b)
        yield
        g0 = jnp.where(mask_g, g[:, :LANES], 0.0)
        g1 = jnp.where(mask_g, g[:, LANES:], 0.0)
        a_blk = jnp.concatenate([jnp.where(lo, g0[:n_l], 0.0), jnp.where(hi, g1[:n_l], 0.0)], axis=0)
        v_lo = jnp.where(lo, vv, zb)
        v_hi = jnp.where(hi, vv, zb)
        ga = jnp.concatenate([g0[:n_l], g1[:n_l]], axis=1).astype(BF16)
        rhs = arh[:n_l] + _dot(ga, jnp.concatenate([zeros_l, v_lo, v_hi, zeros_l], axis=0))
        rhs = jnp.concatenate([jnp.where(lo, rhs, 0.0), jnp.where(hi, rhs, 0.0)], axis=0).astype(BF16)
        t_inv = eye2 + a_blk
        ab = a_blk.astype(BF16)
        a_pow = _dot(ab, ab)
        yield
        for _ in range(4):
            ab = a_pow.astype(BF16)
            both = _dot(ab, jnp.concatenate([ab, t_inv.astype(BF16)], axis=1))
            yield
            a_pow = both[:, :LANES]
            t_inv = t_inv + both[:, LANES:]
        t_inv = t_inv + _dot(a_pow.astype(BF16), t_inv.astype(BF16))
        yield
        u_st = _dot(t_inv.astype(BF16), rhs)
        yield
        u = (u_st[:n_l] + u_st[n_l:]).astype(BF16)
        if y_ref is not None:
            gr = jnp.concatenate([g0[n_l:], g1[n_l:]], axis=1).astype(BF16)
            uv_heads = jnp.concatenate([jnp.where(lo, u, zb), v_lo, v_hi, jnp.where(hi, u, zb)], axis=0)
            y_ref[0, rows, sl] = arh[n_l:] + _dot(gr, uv_heads)
        upd = _dot_tn(jnp.concatenate([u, vv], axis=0),
                      jnp.concatenate([bh_s[:, sl], kh_s[:, sl]], axis=0))
        yield
        h_ref[p] = h0 * etot_s[0:1, sl] + jnp.where(bd, upd, 0.0)

    return pair


def _rw_block(refs_f, refs_b, hf_ref, hb_ref, yf_ref, yb_ref, stage_f, stage_b):
    n_l = RW_CHUNK
    n_sub = refs_f[0].shape[1] // n_l
    n_pairs = D_MODEL // LANES

    def body(i, carry):
        rows_f = pl.ds(pl.multiple_of(i * n_l, n_l), n_l)
        rows_b = pl.ds(pl.multiple_of((n_sub - 1 - i) * n_l, n_l), n_l)
        _rw_stage(refs_f, rows_f, stage_f, False)
        _rw_stage(refs_b, rows_b, stage_b, True)
        pair_f = _rw_pair_fn(refs_f[3], rows_f, hf_ref, yf_ref, stage_f, False)
        pair_b = _rw_pair_fn(refs_b[3], rows_b, hb_ref, yb_ref, stage_b, True)
        for p0 in range(0, n_pairs, RW_PAIR_UNROLL):
            _round_robin([f(p) for p in range(p0, p0 + RW_PAIR_UNROLL) for f in (pair_f, pair_b)])
        return carry

    lax.fori_loop(0, n_sub, body, 0)


def _rw_scan_kernel(*refs, n_cc):
    ctx_f, lat_f, ctx_b, lat_b = refs[0:6], refs[6:12], refs[12:18], refs[18:24]
    yf_ref, yb_ref, hf_ref, hb_ref = refs[24:28]
    stage_f, stage_b = refs[28:35], refs[35:42]
    j = pl.program_id(1)

    @pl.when(j == 0)
    def _():
        hf_ref[...] = jnp.zeros_like(hf_ref)
        hb_ref[...] = jnp.zeros_like(hb_ref)

    @pl.when(j < n_cc)
    def _():
        _rw_block(ctx_f, ctx_b, hf_ref, hb_ref, None, None, stage_f, stage_b)

    @pl.when(j >= n_cc)
    def _():
        _rw_block(lat_f, lat_b, hf_ref, hb_ref, yf_ref, yb_ref, stage_f, stage_b)


def _rw_scan(cx, lat):
    n_b, n_t, _ = lat["r"].shape
    n_l = RW_CHUNK
    n_cc = cx["r"].shape[1] // RW_BLOCK
    n_lc = n_t // RW_BLOCK
    blk = (1, RW_BLOCK, D_MODEL)
    ctx_f = pl.BlockSpec(blk, lambda b, j: (b, jnp.minimum(j, n_cc - 1), 0))
    lat_f = pl.BlockSpec(blk, lambda b, j: (b, jnp.maximum(j - n_cc, 0), 0))
    ctx_b = pl.BlockSpec(blk, lambda b, j: (b, jnp.maximum(n_cc - 1 - j, 0), 0))
    lat_b = pl.BlockSpec(blk, lambda b, j: (b, n_lc - 1 - jnp.maximum(j - n_cc, 0), 0))
    fwd = ("r", "lw0", "k0", "v", "a", "b0")
    bwd = ("r", "lw1", "k1", "v", "a", "b1")
    args = ([cx[n] for n in fwd] + [lat[n] for n in fwd] + [cx[n] for n in bwd] + [lat[n] for n in bwd])
    stage = [pltpu.VMEM((n_l, D_MODEL), BF16)] * 6 + [pltpu.VMEM((8, D_MODEL), F32)]
    return pl.pallas_call(
        functools.partial(_rw_scan_kernel, n_cc=n_cc),
        grid=(n_b, n_cc + n_lc),
        in_specs=[ctx_f] * 6 + [lat_f] * 6 + [ctx_b] * 6 + [lat_b] * 6,
        out_specs=[lat_f, lat_b],
        out_shape=[jax.ShapeDtypeStruct((n_b, n_t, D_MODEL), F32)] * 2,
        scratch_shapes=[pltpu.VMEM((D_MODEL // LANES, LANES, LANES), F32)] * 2 + stage + stage,
        compiler_params=_cparams(("parallel", "arbitrary")),
        name="rw_scan",
    )(*args)


def _rw_out_kernel(x_ref, yf_ref, yb_ref, r_ref, k0_ref, k1_ref, v_ref, g_ref, ga_ref, shm_ref, scm_ref, gm_ref,
                   rk_ref, lnw_ref, lnb_ref, bd_ref, wo_ref, w1_ref, w2_ref, o_ref):
    bd = bd_ref[...]
    inv_n = 1.0 / RW_HEAD_DIM
    y = yf_ref[0] + yb_ref[0]
    dlt = y - _group_sum(y, bd) * inv_n
    var = _group_sum(dlt * dlt, bd) * inv_n
    yn = dlt * lax.rsqrt(var + RW_GN_EPS) * lnw_ref[...] + lnb_ref[...]
    r = r_ref[0].astype(F32)
    kd = k0_ref[0].astype(F32) + k1_ref[0].astype(F32)
    coef = _group_sum(r * kd * rk_ref[...], bd)
    o = _dot(((yn + coef * v_ref[0].astype(F32)) * g_ref[0].astype(F32)).astype(BF16), wo_ref[...])
    x1 = x_ref[0] + ga_ref[0] * o
    o_ref[0] = _mlp(x1, shm_ref[0], scm_ref[0], gm_ref[0], w1_ref, w2_ref)


def _rw_out_mlp(x, yf, yb, lat, mods, prm, w1, w2):
    n_b, n_t, _ = x.shape
    tm = min(MLP_TILE, n_t)
    tok = pl.BlockSpec((1, tm, D_MODEL), lambda b, i: (b, i, 0))
    mod = pl.BlockSpec((1, 1, D_MODEL), lambda b, i: (b, 0, 0))
    vec = _const_spec((1, D_MODEL))
    return pl.pallas_call(
        _rw_out_kernel,
        grid=(n_b, n_t // tm),
        in_specs=[tok] * 8 + [mod] * 4 + [vec, vec, vec, _const_spec((GROUP_W, GROUP_W)),
                                          _const_spec((D_MODEL, D_MODEL)),
                                          _const_spec((D_MODEL, D_FF)), _const_spec((D_FF, D_MODEL))],
        out_specs=tok,
        out_shape=jax.ShapeDtypeStruct(x.shape, F32),
        compiler_params=_cparams(("parallel", "parallel")),
        name="rw_out_mlp",
    )(x, yf, yb, lat["r"], lat["k0"], lat["k1"], lat["v"], lat["g"],
      mods["gate_a"], mods["shift_m"], mods["scale_m"], mods["gate_m"],
      prm["r_k"], prm["lnx_w"], prm["lnx_b"], prm["bd"], prm["w_o"], w1, w2)


def _ab_params(w_in, gate_b, q_norm, k_norm, n_tokens):
    n_proj = 3 * NA_WIDTH + 4 * ML_WIDTH
    head_major = np.array([4 * (j % 4) + j // 4 for j in range(4 * ML_HEADS)])
    wg = w_in[:, n_proj:][:, head_major]
    pos = np.arange(n_tokens)
    n_freq = ML_HEAD_DIM // 4
    inv_freq = (ROPE_BASE ** (-np.arange(n_freq, dtype=np.float32) / n_freq)).astype(np.float32)
    ang_r = ((pos // GRID_W).astype(np.float32)[:, None] * inv_freq).astype(np.float64)
    ang_c = ((pos % GRID_W).astype(np.float32)[:, None] * inv_freq).astype(np.float64)
    cos = np.concatenate([np.cos(ang_r)] * 2 + [np.cos(ang_c)] * 2, axis=1).astype(np.float32)
    sin = np.concatenate([-np.sin(ang_r), np.sin(ang_r), -np.sin(ang_c), np.sin(ang_c)], axis=1).astype(np.float32)
    gb = gate_b[head_major]
    return dict(w_all=w_in[:, :n_proj].astype(BF16), wg_t=wg.T.astype(BF16), bd=_group_ones(),
                q_norm=jnp.tile(q_norm, NA_HEADS)[None], k_norm=jnp.tile(k_norm, NA_HEADS)[None],
                cos=jnp.asarray(cos), sin=jnp.asarray(sin), gate_b_t=gb[:, None])


def _rw_params(mu, w_rkv, w0, w1, w2, a0, a1, a2, g1, g2, k_k, k_a, r_k, lnx_w, lnx_b, w_o):
    pad_g = 256 - RW_GATE_LORA
    return dict(
        mu=jnp.concatenate([mu, jnp.zeros((2, D_MODEL), F32)], axis=0),
        w_rkv=w_rkv.astype(BF16),
        w1=jnp.concatenate([w1[0], w1[1]], axis=1).astype(BF16),
        w2=jnp.concatenate([w2[0], w2[1]], axis=0).astype(BF16), w0=w0,
        a1=jnp.concatenate([a1[0], a1[1]], axis=1).astype(BF16),
        a2=jnp.concatenate([a2[0], a2[1]], axis=0).astype(BF16), a0=a0,
        g1=jnp.pad(g1, ((0, 0), (0, pad_g))).astype(BF16), g2=jnp.pad(g2, ((0, pad_g), (0, 0))).astype(BF16),
        k_k=k_k[None], k_a=k_a[None], r_k=r_k[None], lnx_w=lnx_w[None], lnx_b=lnx_b[None],
        bd=_group_ones(), w_o=w_o.astype(BF16))


def _mod_dict(m, n_b):
    names = ("shift_a", "scale_a", "gate_a", "shift_m", "scale_m", "gate_m")
    parts = jnp.split(m, 6, axis=-1)
    return {n: jnp.broadcast_to(p[:, None, :], (n_b, 1, D_MODEL)) for n, p in zip(names, parts)}


_PREP_NAMES = ("r", "v", "g", "a", "k0", "k1", "b0", "b1", "lw0", "lw1")


def kernel(x, c, ctx, c_ctx, ada_w, ada_b, ab_w_in, ab_gate_b, na_q_norm, na_k_norm, na_rpb, ml_head_norm, ab_w_out, rw_mu, rw_w_rkv, rw_w0, rw_w1, rw_w2, rw_a0, rw_a1, rw_a2, rw_g1, rw_g2, rw_k_k, rw_k_a, rw_r_k, rw_lnx_w, rw_lnx_b, rw_w_o, mlp_w1, mlp_w2):
    n_b, n_t, _ = x.shape
    assert ada_w.shape[0] == 2, "even (attention / mLSTM) layer followed by an odd (RWKV-7) layer"
    cond = jnp.concatenate([c, c_ctx[None], jnp.zeros((16 - n_b - 1, D_MODEL), F32)], axis=0)
    mods = _adaln(cond, ada_w, ada_b)
    w1 = mlp_w1.astype(BF16)
    w2 = mlp_w2.astype(BF16)

    m_l = _mod_dict(mods[0, :n_b], n_b)
    m_c = _mod_dict(mods[0, n_b:n_b + 1], n_b)
    prm = _ab_params(ab_w_in[0], ab_gate_b[0], na_q_norm[0], na_k_norm[0], n_t)
    qa_l, ka_l, va_l, qb_l, kb_l, vb_l, ob_l, gt_l = _ab_project(x, m_l["shift_a"], m_l["scale_a"], prm, True)
    qa_c, ka_c, va_c, qb_c, kb_c, vb_c, ob_c, gt_c = _ab_project(ctx, m_c["shift_a"], m_c["scale_a"], prm, False)
    na_l = _na_attention(qa_l, ka_l, va_l, ka_c, va_c, _na_col_tiles(na_rpb[0]))
    na_c = _ctx_attention(qa_c, ka_c, va_c)
    hlf, hlb, hcf, hcb = _mlstm(qb_l, kb_l, vb_l, gt_l, qb_c, kb_c, vb_c, gt_c)
    head_norm = ml_head_norm[0].reshape(1, ML_WIDTH)
    w_out = ab_w_out[0].astype(BF16)
    x = _ab_out_mlp(x, na_l, hlf, hlb, ob_l, m_l, head_norm, w_out, w1[0], w2[0])
    ctx = _ab_out_mlp(ctx, na_c, hcf, hcb, ob_c, m_c, head_norm, w_out, w1[0], w2[0])

    m_l = _mod_dict(mods[1, :n_b], n_b)
    m_c = _mod_dict(mods[1, n_b:n_b + 1], n_b)
    rprm = _rw_params(rw_mu[0], rw_w_rkv[0], rw_w0[0], rw_w1[0], rw_w2[0], rw_a0[0], rw_a1[0], rw_a2[0],
                      rw_g1[0], rw_g2[0], rw_k_k[0], rw_k_a[0], rw_r_k[0], rw_lnx_w[0], rw_lnx_b[0], rw_w_o[0])
    lat = dict(zip(_PREP_NAMES, _rw_prepare(x, m_l["shift_a"], m_l["scale_a"], rprm)))
    cxp = dict(zip(_PREP_NAMES, _rw_prepare(ctx, m_c["shift_a"], m_c["scale_a"], rprm)))
    yf, yb = _rw_scan(cxp, lat)
    return _rw_out_mlp(x, yf, yb, lat, m_l, rprm, w1[1], w2[1])
```

```python
import functools

import numpy as np
import jax
import jax.numpy as jnp
from jax import lax
from jax.experimental import pallas as pl
from jax.experimental.pallas import tpu as pltpu

F32 = jnp.float32
BF16 = jnp.bfloat16

D_MODEL = 1024
GRID_W = 64
NA_HEAD_DIM = 64
NA_HEADS = 8
NA_WIDTH = 512
NA_KH = 8
NA_KW = 16
ML_HEAD_DIM = 128
ML_HEADS = 4
ML_WIDTH = 512
ML_CHUNK = 128
ML_GATE_CAP = 15.0
RW_HEAD_DIM = 64
RW_HEADS = 16
RW_LORA = 64
RW_GATE_LORA = 160
RW_GN_EPS = 64e-5
D_FF = 4 * D_MODEL
ROPE_BASE = 10000.0
NORM_EPS = 1e-6

LANES = 128
GROUP_W = 256
VMEM_LIMIT = 56 * 1024 * 1024
TOKEN_TILE = 512
MLP_TILE = 512
NA_ROWS = 4
NA_KEY_ROWS = NA_ROWS + NA_KH - 1
NA_STRIP = 32
NA_UNROLL = 8
RW_CHUNK = 64
RW_BLOCK = 256
RW_PAIR_UNROLL = 8
NEG_BIG = -1e30
RW_DECAY_SCALE = float(np.exp(-0.5))


def _cparams(sem):
    return pltpu.CompilerParams(dimension_semantics=sem, vmem_limit_bytes=VMEM_LIMIT)


def _const_spec(shape):
    nd = len(shape)
    return pl.BlockSpec(shape, lambda *_: (0,) * nd, pipeline_mode=pl.Buffered(1))


def _dot(a, b):
    return jnp.dot(a, b, preferred_element_type=F32)


def _dot_nt(a, b):
    return lax.dot_general(a, b, (((1,), (1,)), ((), ())), preferred_element_type=F32)


def _dot_tn(a, b):
    return lax.dot_general(a, b, (((0,), (0,)), ((), ())), preferred_element_type=F32)


def _split3(x):
    hi = x.astype(BF16)
    r1 = x - hi.astype(F32)
    mid = r1.astype(BF16)
    lo = (r1 - mid.astype(F32)).astype(BF16)
    return hi, mid, lo


def _dot_exact_rhs(a_bf16, x):
    hi, mid, lo = _split3(x)
    return _dot(a_bf16, hi) + _dot(a_bf16, mid) + _dot(a_bf16, lo)


def _dot_exact_lhs(x, a_bf16):
    hi, mid, lo = _split3(x)
    return _dot(hi, a_bf16) + _dot(mid, a_bf16) + _dot(lo, a_bf16)


def _dot_f32(a, b):
    ah = a.astype(BF16)
    al = (a - ah.astype(F32)).astype(BF16)
    bh = b.astype(BF16)
    bl = (b - bh.astype(F32)).astype(BF16)
    return _dot(ah, bh) + _dot(ah, bl) + _dot(al, bh)


def _sigmoid(x):
    return 0.5 * jnp.tanh(0.5 * x) + 0.5


def _log_sigmoid(x):
    return jnp.minimum(x, 0.0) - jnp.log(1.0 + jnp.exp(-jnp.abs(x)))


def _rms(x):
    return x * lax.rsqrt(jnp.mean(x * x, axis=-1, keepdims=True) + NORM_EPS)


def _modulate(x, shift, scale):
    return _rms(x) * (1.0 + scale) + shift


def _iota(shape, dim):
    return lax.broadcasted_iota(jnp.int32, shape, dim)


def _group_sum(x, bd):
    parts = []
    for j in range(x.shape[1] // GROUP_W):
        parts.append(_dot(x[:, j * GROUP_W:(j + 1) * GROUP_W].astype(BF16), bd))
    return jnp.concatenate(parts, axis=1)


def _group_ones():
    grp = np.arange(GROUP_W) // 64
    return jnp.asarray(grp[:, None] == grp[None, :], BF16)


def _round_robin(chains, stagger=False):
    pending = list(chains)
    live = []
    while pending or live:
        if pending:
            live.extend(pending[:1] if stagger else pending)
            del pending[:1 if stagger else len(pending)]
        nxt = []
        for ch in (reversed(live) if stagger else live):
            try:
                next(ch)
                nxt.append(ch)
            except StopIteration:
                pass
        live = nxt[::-1] if stagger else nxt


def _adaln_kernel(c_ref, w_ref, b_ref, o_ref):
    c = c_ref[...]
    s = c * _sigmoid(c)
    o_ref[0] = _dot_f32(s, w_ref[0]) + b_ref[0]


def _adaln(cond, ada_w, ada_b):
    depth = ada_w.shape[0]
    tn = 512
    return pl.pallas_call(
        _adaln_kernel,
        grid=(depth, 6 * D_MODEL // tn),
        in_specs=[
            pl.BlockSpec((16, D_MODEL), lambda l, j: (0, 0)),
            pl.BlockSpec((1, D_MODEL, tn), lambda l, j: (l, 0, j)),
            pl.BlockSpec((1, 1, tn), lambda l, j: (l, 0, j)),
        ],
        out_specs=pl.BlockSpec((1, 16, tn), lambda l, j: (l, 0, j)),
        out_shape=jax.ShapeDtypeStruct((depth, 16, 6 * D_MODEL), F32),
        compiler_params=_cparams(("parallel", "parallel")),
        name="adaln",
    )(cond, ada_w, ada_b.reshape(depth, 1, 6 * D_MODEL))


_C_QA, _C_KA, _C_VA, _C_QB, _C_KB, _C_VB, _C_OB = (0, 512, 1024, 1536, 2048, 2560, 3072)
_AB_COLS = 3584


def _ab_proj_kernel(x_ref, sh_ref, sc_ref, w_ref, wgt_ref, bd_ref, qn_ref, kn_ref, cos_ref, sin_ref, gbt_ref,
                    qa_ref, ka_ref, va_ref, qb_ref, kb_ref, vb_ref, ob_ref, gt_ref, *, rope):
    h = _modulate(x_ref[0], sh_ref[0], sc_ref[0]).astype(BF16)

    def proj(c0, n=512):
        return _dot(h, w_ref[:, c0:c0 + n])

    bd = bd_ref[...]

    def head_norm(y, gain):
        ms = _group_sum(y * y, bd) * (1.0 / NA_HEAD_DIM)
        return y * lax.rsqrt(ms + NORM_EPS) * gain

    qa_ref[0] = (head_norm(proj(_C_QA), qn_ref[...]) * (NA_HEAD_DIM ** -0.5)).astype(BF16)
    ka_ref[0] = head_norm(proj(_C_KA), kn_ref[...]).astype(BF16)
    va_ref[0] = proj(_C_VA).astype(BF16)
    qb = proj(_C_QB)
    kb = proj(_C_KB)
    if rope:
        cos = jnp.concatenate([cos_ref[...]] * ML_HEADS, axis=1)
        sin = jnp.concatenate([sin_ref[...]] * ML_HEADS, axis=1)
        first_half = (_iota((1, ML_WIDTH), 1) & (ML_HEAD_DIM // 4)) == 0

        def partner(y):
            return jnp.where(first_half, pltpu.roll(y, ML_WIDTH - ML_HEAD_DIM // 4, 1),
                             pltpu.roll(y, ML_HEAD_DIM // 4, 1))

        qb = qb * cos + partner(qb) * sin
        kb = kb * cos + partner(kb) * sin
    qb_ref[0] = qb.astype(BF16)
    kb_ref[0] = (kb * (ML_HEAD_DIM ** -0.5)).astype(BF16)
    vb_ref[0] = proj(_C_VB).astype(BF16)
    ob_ref[0] = proj(_C_OB)
    gt_ref[0] = _dot_nt(wgt_ref[...], h) + gbt_ref[...]


def _ab_project(x, shift, scale, prm, rope):
    n_b, n_t, _ = x.shape
    tm = min(TOKEN_TILE, n_t)
    tok = lambda w, dt: jax.ShapeDtypeStruct((n_b, n_t, w), dt)
    tok_spec = lambda w: pl.BlockSpec((1, tm, w), lambda b, i: (b, i, 0))
    mod_spec = pl.BlockSpec((1, 1, D_MODEL), lambda b, i: (b, 0, 0))
    return pl.pallas_call(
        functools.partial(_ab_proj_kernel, rope=rope),
        grid=(n_b, n_t // tm),
        in_specs=[
            tok_spec(D_MODEL), mod_spec, mod_spec,
            _const_spec((D_MODEL, _AB_COLS)), _const_spec((4 * ML_HEADS, D_MODEL)),
            _const_spec((GROUP_W, GROUP_W)), _const_spec((1, NA_WIDTH)), _const_spec((1, NA_WIDTH)),
            pl.BlockSpec((tm, LANES), lambda b, i: (i, 0)), pl.BlockSpec((tm, LANES), lambda b, i: (i, 0)),
            _const_spec((4 * ML_HEADS, 1)),
        ],
        out_specs=[tok_spec(512)] * 7 + [pl.BlockSpec((1, 4 * ML_HEADS, tm), lambda b, i: (b, 0, i))],
        out_shape=[tok(512, BF16)] * 6 + [tok(512, F32), jax.ShapeDtypeStruct((n_b, 4 * ML_HEADS, n_t), F32)],
        compiler_params=_cparams(("parallel", "parallel")),
        name="ab_proj_rope" if rope else "ab_proj",
    )(x, shift, scale, prm["w_all"], prm["wg_t"], prm["bd"], prm["q_norm"], prm["k_norm"],
      prm["cos"][:n_t], prm["sin"][:n_t], prm["gate_b_t"])


def _na_kernel(q_ref, k_ref, v_ref, kc_ref, vc_ref, tiles_ref, o_ref, bias_ref, *, row_offsets):
    n_t = q_ref.shape[1]
    n_rows = n_t // GRID_W
    rq = NA_ROWS * GRID_W
    n_blk = n_t // rq
    n_keys = NA_KEY_ROWS * GRID_W
    kc = kc_ref[0]
    vc = vc_ref[0]
    lane = _iota((1, LANES), 1)

    @pl.when(pl.program_id(1) == 0)
    def _():
        for case in range(3):
            for hh in range(2):
                for i in range(NA_ROWS):
                    for j in range(NA_KEY_ROWS):
                        bias_ref[case, hh, i * GRID_W:(i + 1) * GRID_W, j * GRID_W:(j + 1) * GRID_W] = (
                            tiles_ref[hh, int(row_offsets[case, i, j])])

    def chain(bi, hh, outs):
        kr0 = jnp.clip(bi * NA_ROWS - NA_KH // 2, 0, n_rows - NA_KEY_ROWS)
        k0 = pl.multiple_of(kr0 * GRID_W, GRID_W)
        case = jnp.where(bi == 0, 0, jnp.where(bi == n_blk - 1, 2, 1))
        q = q_ref[0, pl.ds(pl.multiple_of(bi * rq, rq), rq), :]
        in_head = (lane >= hh * NA_HEAD_DIM) & (lane < (hh + 1) * NA_HEAD_DIM)
        qm = jnp.where(in_head, q, jnp.zeros_like(q))
        s_nb = _dot_nt(qm, k_ref[0, pl.ds(k0, n_keys), :])
        s_cx = _dot_nt(qm, kc)
        yield
        p_nb, p_cx, den = [], [], []
        for r0 in range(0, rq, NA_STRIP):
            rows = slice(r0, r0 + NA_STRIP)
            sn = s_nb[rows] + bias_ref[case, hh, rows, :]
            sc = s_cx[rows]
            m = jnp.maximum(jnp.max(sn, axis=-1, keepdims=True), jnp.max(sc, axis=-1, keepdims=True))
            pn = jnp.exp(sn - m)
            pc = jnp.exp(sc - m)
            den.append(jnp.sum(pn, axis=-1, keepdims=True) + jnp.sum(pc, axis=-1, keepdims=True))
            p_nb.append(pn.astype(BF16))
            p_cx.append(pc.astype(BF16))
        o = (_dot(jnp.concatenate(p_nb, axis=0), v_ref[0, pl.ds(k0, n_keys), :])
             + _dot(jnp.concatenate(p_cx, axis=0), vc))
        yield
        outs[hh] = o / jnp.concatenate(den, axis=0)

    def body(i, carry):
        blocks = [(NA_UNROLL * i + u, [None, None]) for u in range(NA_UNROLL)]
        _round_robin([chain(bi, hh, outs) for bi, outs in blocks for hh in range(2)], stagger=True)
        for bi, outs in blocks:
            o_ref[0, pl.ds(pl.multiple_of(bi * rq, rq), rq), :] = jnp.where(
                lane < NA_HEAD_DIM, outs[0], outs[1]).astype(o_ref.dtype)
        return carry

    lax.fori_loop(0, n_blk // NA_UNROLL, body, 0)


def _na_attention(qa, ka, va, ka_c, va_c, col_tiles):
    n_b, n_t, _ = qa.shape
    n_ctx = ka_c.shape[1]
    seq = pl.BlockSpec((1, n_t, LANES), lambda hp, b: (b, 0, hp))
    cseq = pl.BlockSpec((1, n_ctx, LANES), lambda hp, b: (b, 0, hp))
    return pl.pallas_call(
        functools.partial(_na_kernel, row_offsets=_na_row_offsets(n_t // GRID_W)),
        grid=(NA_HEADS // 2, n_b),
        in_specs=[seq, seq, seq, cseq, cseq,
                  pl.BlockSpec((2,) + col_tiles.shape[1:], lambda hp, b: (hp, 0, 0, 0))],
        out_specs=seq,
        out_shape=jax.ShapeDtypeStruct((n_b, n_t, NA_WIDTH), BF16),
        scratch_shapes=[pltpu.VMEM((3, 2, NA_ROWS * GRID_W, NA_KEY_ROWS * GRID_W), F32)],
        compiler_params=_cparams(("parallel", "arbitrary")),
        name="na_attention",
    )(qa, ka, va, ka_c, va_c, col_tiles)


def _ctx_attn_kernel(q_ref, k_ref, v_ref, o_ref):
    q = q_ref[0]
    k = k_ref[0]
    v = v_ref[0]
    lane = _iota((1, LANES), 1)
    outs = []
    for hh in range(2):
        in_head = (lane >= hh * NA_HEAD_DIM) & (lane < (hh + 1) * NA_HEAD_DIM)
        qm = jnp.where(in_head, q, jnp.zeros_like(q))
        s = _dot_nt(qm, k)
        p = jnp.exp(s - jnp.max(s, axis=-1, keepdims=True))
        outs.append(_dot(p.astype(BF16), v) / jnp.sum(p, axis=-1, keepdims=True))
    o_ref[0] = jnp.where(lane < NA_HEAD_DIM, outs[0], outs[1]).astype(o_ref.dtype)


def _ctx_attention(qa, ka, va):
    n_b, n_ctx, _ = qa.shape
    spec = pl.BlockSpec((1, n_ctx, LANES), lambda b, hp: (b, 0, hp))
    return pl.pallas_call(
        _ctx_attn_kernel,
        grid=(n_b, NA_HEADS // 2),
        in_specs=[spec, spec, spec],
        out_specs=spec,
        out_shape=jax.ShapeDtypeStruct((n_b, n_ctx, NA_WIDTH), BF16),
        compiler_params=_cparams(("parallel", "parallel")),
        name="ctx_attention",
    )(qa, ka, va)


def _na_col_tiles(rpb):
    cq = np.arange(GRID_W)[:, None]
    ck = np.arange(GRID_W)[None, :]
    cs = np.clip(cq - NA_KW // 2, 0, GRID_W - NA_KW)
    col_ok = (ck >= cs) & (ck < cs + NA_KW)
    pad = jnp.pad(rpb, ((0, 0), (0, 0), (GRID_W, GRID_W)))
    lo = GRID_W + NA_KW - 1
    toep = jnp.stack([pad[:, :, lo - c:lo - c + GRID_W] for c in range(GRID_W)], axis=2)
    toep = jnp.where(col_ok[None, None], toep, NEG_BIG)
    return jnp.concatenate([toep, jnp.full_like(toep[:, :1], NEG_BIG)], axis=1)


def _na_row_offsets(n_rows):
    n_blk = n_rows // NA_ROWS
    out = []
    for bi in (0, 1, n_blk - 1):
        r0 = bi * NA_ROWS
        kr0 = int(np.clip(r0 - NA_KH // 2, 0, n_rows - NA_KEY_ROWS))
        rq = r0 + np.arange(NA_ROWS)[:, None]
        rk = kr0 + np.arange(NA_KEY_ROWS)[None, :]
        rs = np.clip(rq - NA_KH // 2, 0, n_rows - NA_KH)
        row_ok = (rk >= rs) & (rk < rs + NA_KH)
        out.append(np.where(row_ok, rk - rq + NA_KH - 1, 2 * NA_KH - 1))
    return np.stack(out)


def _mlstm_prologue(grow_ref, rows_s, cols_s):
    n_l = ML_CHUNK
    n_c = grow_ref.shape[3]
    ti = _iota((n_l, n_l), 0)
    si = _iota((n_l, n_l), 1)
    lower = jnp.where(si <= ti, 1.0, 0.0).astype(BF16)
    upper = jnp.where(si >= ti, 1.0, 0.0).astype(BF16)
    cap = lambda g: ML_GATE_CAP * jnp.tanh(g / ML_GATE_CAP)
    fill = jnp.zeros((LANES - n_c, n_l), F32)
    for hh in range(2):
        for d in range(2):
            ch = 2 * hh + d
            b_rows = _dot_exact_lhs(_log_sigmoid(cap(grow_ref[0, hh, 2 * d + 1])), lower if d else upper)
            rows_s[ch] = b_rows
            ib_rows = cap(grow_ref[0, hh, 2 * d]) - b_rows
            cols_s[ch] = jnp.concatenate([ib_rows, fill], axis=0).T


def _mlstm_chain(ch, c, q_ref, k_ref, v_ref, rows_s, cols_s, c_s, n_s, m_s, out_ref):
    n_l = ML_CHUNK
    hh, rev = ch // 2, ch % 2 == 1
    t0 = pl.multiple_of(c * n_l, n_l)
    hs = pl.ds(hh * ML_HEAD_DIM, ML_HEAD_DIM)
    q = q_ref[0, pl.ds(t0, n_l), hs]
    k = k_ref[0, pl.ds(t0, n_l), hs]
    v = v_ref[0, pl.ds(t0, n_l), hs]
    c_st = c_s[ch]
    n_st = n_s[ch]
    m_st = m_s[ch, 0:1, 0:1]
    b_row = rows_s[ch, pl.ds(c, 1), :]
    lane = _iota((1, LANES), 1)
    ib_col = jnp.sum(jnp.where(lane == c, cols_s[ch], 0.0), axis=1, keepdims=True)
    is_last = lane == (0 if rev else n_l - 1)
    b_last = jnp.sum(jnp.where(is_last, b_row, 0.0), axis=1, keepdims=True)
    kcn = _dot_nt(jnp.concatenate([k, c_st.astype(BF16), n_st.astype(BF16)], axis=0), q)
    yield
    si = _iota((n_l, n_l), 0)
    ti = _iota((n_l, n_l), 1)
    mask = (si >= ti) if rev else (si <= ti)
    log_d = jnp.where(mask, b_row + ib_col, -jnp.inf)
    log_inter = b_row + m_st
    m_t = jnp.maximum(log_inter, jnp.max(log_d, axis=0, keepdims=True))
    m_new = jnp.sum(jnp.where(is_last, m_t, 0.0), axis=1, keepdims=True)
    w_s = jnp.exp(ib_col + (b_last - m_new))
    decay = jnp.exp(b_last + m_st - m_new)
    w_inter = jnp.exp(log_inter - m_t)
    s_w = kcn[:n_l] * jnp.exp(log_d - m_t)
    kw = k.astype(F32) * w_s
    both = _dot_tn(v, jnp.concatenate([s_w.astype(BF16), kw.astype(BF16)], axis=1))
    c_s[ch] = decay * c_st + both[:, n_l:]
    n_s[ch] = jnp.broadcast_to(decay * n_st[0:1] + jnp.sum(kw, axis=0, keepdims=True), n_s.shape[1:])
    m_s[ch] = jnp.broadcast_to(m_new, m_s.shape[1:])
    yield
    num_t = both[:, :n_l] + w_inter * kcn[n_l:2 * n_l]
    den = jnp.sum(s_w, axis=0, keepdims=True) + w_inter * kcn[2 * n_l:2 * n_l + 1]
    h_t = num_t / jnp.maximum(jnp.abs(den), jnp.exp(-m_t))
    out_ref[0, pl.ds(t0, n_l), hs] = h_t.T


def _mlstm_kernel(ql_ref, kl_ref, vl_ref, gl_ref, qc_ref, kc_ref, vc_ref, gc_ref,
                  hlf_ref, hlb_ref, hcf_ref, hcb_ref,
                  rows_l, cols_l, rows_c, cols_c, c_s, n_s, m_s):
    n_l = ML_CHUNK
    _mlstm_prologue(gc_ref, rows_c, cols_c)
    _mlstm_prologue(gl_ref, rows_l, cols_l)
    c_s[...] = jnp.zeros_like(c_s)
    n_s[...] = jnp.zeros_like(n_s)
    m_s[...] = jnp.zeros_like(m_s)

    def sweep(q_ref, k_ref, v_ref, rows_s, cols_s, outs):
        n_c = q_ref.shape[1] // n_l

        def body(j, carry):
            _round_robin([_mlstm_chain(ch, (n_c - 1 - j) if ch % 2 else j, q_ref, k_ref, v_ref, rows_s, cols_s,
                                       c_s, n_s, m_s, outs[ch % 2]) for ch in range(4)])
            return carry

        lax.fori_loop(0, n_c, body, 0)

    sweep(qc_ref, kc_ref, vc_ref, rows_c, cols_c, (hcf_ref, hcb_ref))
    sweep(ql_ref, kl_ref, vl_ref, rows_l, cols_l, (hlf_ref, hlb_ref))


def _mlstm(ql, kl, vl, gl_t, qc, kc, vc, gc_t):
    n_b, n_t, _ = ql.shape
    n_ctx = qc.shape[1]
    n_l = ML_CHUNK

    def gate_rows(g_t, n):
        n_c = n // n_l
        rows = g_t.reshape(n_b, ML_HEADS, 4, n_c, n_l)
        return jnp.pad(rows, ((0, 0),) * 3 + ((0, (-n_c) % 8), (0, 0)))

    glr = gate_rows(gl_t, n_t)
    gcr = gate_rows(gc_t, n_ctx)
    wide = 2 * ML_HEAD_DIM

    def seq(n):
        return pl.BlockSpec((1, n, wide), lambda b, hp: (b, 0, hp))

    def gspec(a):
        return pl.BlockSpec((1, 2) + a.shape[2:], lambda b, hp: (b, hp, 0, 0, 0))

    vm = lambda *s: pltpu.VMEM(s, F32)
    return pl.pallas_call(
        _mlstm_kernel,
        grid=(n_b, ML_HEADS // 2),
        in_specs=[seq(n_t), seq(n_t), seq(n_t), gspec(glr), seq(n_ctx), seq(n_ctx), seq(n_ctx), gspec(gcr)],
        out_specs=[seq(n_t), seq(n_t), seq(n_ctx), seq(n_ctx)],
        out_shape=[jax.ShapeDtypeStruct((n_b, n_t, ML_WIDTH), F32)] * 2
        + [jax.ShapeDtypeStruct((n_b, n_ctx, ML_WIDTH), F32)] * 2,
        scratch_shapes=[vm(4, glr.shape[3], n_l), vm(4, n_l, LANES),
                        vm(4, gcr.shape[3], n_l), vm(4, n_l, LANES),
                        vm(4, ML_HEAD_DIM, ML_HEAD_DIM), vm(4, 16, ML_HEAD_DIM), vm(4, 8, LANES)],
        compiler_params=_cparams(("parallel", "parallel")),
        name="mlstm",
    )(ql, kl, vl, glr, qc, kc, vc, gcr)


def _mlp(x1, sh, sc, gt, w1_ref, w2_ref):
    h = _modulate(x1, sh, sc).astype(BF16)
    acc = jnp.zeros_like(x1)
    n_chunk = 1024
    for c in range(D_FF // n_chunk):
        a = jnp.maximum(_dot(h, w1_ref[:, c * n_chunk:(c + 1) * n_chunk]), 0.0)
        acc = acc + _dot((a * a).astype(BF16), w2_ref[c * n_chunk:(c + 1) * n_chunk, :])
    return x1 + gt * acc


def _ab_out_kernel(x_ref, na_ref, hf_ref, hb_ref, ob_ref, ga_ref, shm_ref, scm_ref, gm_ref, hn_ref,
                   wo_ref, w1_ref, w2_ref, o_ref):
    ml = hf_ref[0] + hb_ref[0]
    parts = []
    for hh in range(ML_HEADS):
        parts.append(_rms(ml[:, hh * ML_HEAD_DIM:(hh + 1) * ML_HEAD_DIM]))
    ml = jnp.concatenate(parts, axis=1) * hn_ref[...] * _sigmoid(ob_ref[0])
    o = _dot(na_ref[0], wo_ref[0:NA_WIDTH, :]) + _dot(ml.astype(BF16), wo_ref[NA_WIDTH:, :])
    x1 = x_ref[0] + ga_ref[0] * o
    o_ref[0] = _mlp(x1, shm_ref[0], scm_ref[0], gm_ref[0], w1_ref, w2_ref)


def _ab_out_mlp(x, na, hf, hb, ob, mods, head_norm, w_out, w1, w2):
    n_b, n_t, _ = x.shape
    tm = min(MLP_TILE, n_t)
    tok = lambda w: pl.BlockSpec((1, tm, w), lambda b, i: (b, i, 0))
    mod = pl.BlockSpec((1, 1, D_MODEL), lambda b, i: (b, 0, 0))
    return pl.pallas_call(
        _ab_out_kernel,
        grid=(n_b, n_t // tm),
        in_specs=[tok(D_MODEL), tok(512), tok(512), tok(512), tok(512), mod, mod, mod, mod,
                  _const_spec((1, ML_WIDTH)), _const_spec((D_MODEL, D_MODEL)),
                  _const_spec((D_MODEL, D_FF)), _const_spec((D_FF, D_MODEL))],
        out_specs=tok(D_MODEL),
        out_shape=jax.ShapeDtypeStruct(x.shape, F32),
        compiler_params=_cparams(("parallel", "parallel")),
        name="ab_out_mlp",
    )(x, na, hf, hb, ob, mods["gate_a"], mods["shift_m"], mods["scale_m"], mods["gate_m"],
      head_norm, w_out, w1, w2)


def _rw_prep_kernel(x_ref, xp_ref, xn_ref, sh_ref, sc_ref, mu_ref, wrkv_ref, w1_ref, w2_ref, w0_ref,
                    a1_ref, a2_ref, a0_ref, g1_ref, g2_ref, kk_ref, ka_ref, bd_ref,
                    r_ref, v_ref, g_ref, a_ref, k0_ref, k1_ref, b0_ref, b1_ref, lw0_ref, lw1_ref):
    i = pl.program_id(1)
    n_i = pl.num_programs(1)
    sh = sh_ref[0]
    sc = sc_ref[0]
    h = _modulate(x_ref[0], sh, sc)
    tm = h.shape[0]
    h_before = _modulate(xp_ref[0, 7:8, :], sh, sc) * jnp.where(i > 0, 1.0, 0.0)
    h_after = _modulate(xn_ref[0, 0:1, :], sh, sc) * jnp.where(i < n_i - 1, 1.0, 0.0)
    row = _iota((tm, 1), 0)
    h_prev = jnp.where(row == 0, h_before, pltpu.roll(h, 1, 0))
    h_next = jnp.where(row == tm - 1, h_after, pltpu.roll(h, tm - 1, 0))
    xx = 0.5 * (h_prev + h_next) - h

    def mix(s):
        return (h + xx * mu_ref[s:s + 1, :]).astype(BF16)

    r = _dot(mix(0), wrkv_ref[0])
    k = _dot(mix(2), wrkv_ref[1])
    v = _dot(mix(3), wrkv_ref[2])
    lane = _iota((1, 2 * RW_LORA), 1)
    hid_w = jnp.tanh(_dot(mix(1), w1_ref[...]))
    hid_a = _dot(mix(4), a1_ref[...])
    g = _dot(_sigmoid(_dot(mix(5), g1_ref[...])).astype(BF16), g2_ref[...])
    kk = k * kk_ref[...]
    ss = _group_sum(kk * kk, bd_ref[...])
    kk = kk * jnp.minimum(lax.rsqrt(ss), 1e12)
    r_ref[0] = r.astype(r_ref.dtype)
    v_ref[0] = v.astype(v_ref.dtype)
    g_ref[0] = g.astype(g_ref.dtype)
    a_ref[0] = (-kk).astype(a_ref.dtype)
    for z, (k_out, b_out, lw_out) in enumerate(((k0_ref, b0_ref, lw0_ref), (k1_ref, b1_ref, lw1_ref))):
        in_dir = (lane >= z * RW_LORA) & (lane < (z + 1) * RW_LORA)
        w_logit = w0_ref[z:z + 1, :] + _dot(jnp.where(in_dir, hid_w, 0.0).astype(BF16), w2_ref[...])
        a = _sigmoid(a0_ref[z:z + 1, :] + _dot(jnp.where(in_dir, hid_a, 0.0).astype(BF16), a2_ref[...]))
        lw_out[0] = -RW_DECAY_SCALE * _sigmoid(w_logit)
        k_out[0] = (k * (1.0 + (a - 1.0) * ka_ref[...])).astype(k_out.dtype)
        b_out[0] = (kk * a).astype(b_out.dtype)


def _rw_prepare(x, shift, scale, prm):
    n_b, n_t, _ = x.shape
    tm = min(TOKEN_TILE, n_t)
    tok = pl.BlockSpec((1, tm, D_MODEL), lambda b, i: (b, i, 0))
    mod = pl.BlockSpec((1, 1, D_MODEL), lambda b, i: (b, 0, 0))
    n8 = n_t // 8
    prev_spec = pl.BlockSpec((1, 8, D_MODEL), lambda b, i: (b, jnp.maximum(i * (tm // 8) - 1, 0), 0))
    next_spec = pl.BlockSpec((1, 8, D_MODEL), lambda b, i: (b, jnp.minimum((i + 1) * (tm // 8), n8 - 1), 0))
    out = lambda dt: jax.ShapeDtypeStruct((n_b, n_t, D_MODEL), dt)
    return pl.pallas_call(
        _rw_prep_kernel,
        grid=(n_b, n_t // tm),
        in_specs=[tok, prev_spec, next_spec, mod, mod, _const_spec((8, D_MODEL)),
                  _const_spec((3, D_MODEL, D_MODEL)),
                  _const_spec((D_MODEL, 2 * RW_LORA)), _const_spec((2 * RW_LORA, D_MODEL)), _const_spec((2, D_MODEL)),
                  _const_spec((D_MODEL, 2 * RW_LORA)), _const_spec((2 * RW_LORA, D_MODEL)), _const_spec((2, D_MODEL)),
                  _const_spec((D_MODEL, 256)), _const_spec((256, D_MODEL)),
                  _const_spec((1, D_MODEL)), _const_spec((1, D_MODEL)), _const_spec((GROUP_W, GROUP_W))],
        out_specs=[tok] * 10,
        out_shape=[out(BF16)] * 8 + [out(F32)] * 2,
        compiler_params=_cparams(("parallel", "parallel")),
        name="rw_prepare",
    )(x, x, x, shift, scale, prm["mu"], prm["w_rkv"], prm["w1"], prm["w2"], prm["w0"],
      prm["a1"], prm["a2"], prm["a0"], prm["g1"], prm["g2"], prm["k_k"], prm["k_a"], prm["bd"])


def _rw_stage(refs, rows, stage, rev):
    r_ref, lw_ref, k_ref, v_ref, a_ref, b_ref = refs
    n_l = RW_CHUNK
    lw = lw_ref[0, rows, :]
    ti = _iota((n_l, n_l), 0)
    si = _iota((n_l, n_l), 1)
    incl = (si >= ti) if rev else (si <= ti)
    cum = _dot_exact_rhs(jnp.where(incl, 1.0, 0.0).astype(BF16), lw)
    tot = jnp.sum(lw, axis=0, keepdims=True)
    e_pos = jnp.exp(cum)
    e_neg = jnp.exp(-cum)
    e_tot = jnp.exp(tot)
    rt_s, at_s, bt_s, kt_s, bh_s, kh_s, etot_s = stage
    etot_s[...] = jnp.broadcast_to(e_tot, etot_s.shape)
    b = b_ref[0, rows, :].astype(F32)
    k = k_ref[0, rows, :].astype(F32)
    rt_s[...] = (r_ref[0, rows, :].astype(F32) * e_pos).astype(BF16)
    at_s[...] = (a_ref[0, rows, :].astype(F32) * jnp.exp(cum - lw)).astype(BF16)
    bt = b * e_neg
    kt = k * e_neg
    bt_s[...] = bt.astype(BF16)
    kt_s[...] = kt.astype(BF16)
    bh_s[...] = (bt * e_tot).astype(BF16)
    kh_s[...] = (kt * e_tot).astype(BF16)


def _rw_pair_fn(v_ref, rows, h_ref, y_ref, stage, rev):
    n_l = RW_CHUNK
    rt_s, at_s, bt_s, kt_s, bh_s, kh_s, etot_s = stage
    lane = _iota((1, LANES), 1)
    lo = lane < RW_HEAD_DIM
    hi = jnp.logical_not(lo)
    ti2 = _iota((2 * n_l, LANES), 0)
    si2 = _iota((2 * n_l, LANES), 1)
    tq = jnp.where(ti2 >= n_l, ti2 - n_l, ti2)
    sk = jnp.where(si2 >= n_l, si2 - n_l, si2)
    strict2 = (sk > tq) if rev else (sk < tq)
    mask_g = strict2 | ((ti2 >= n_l) & (sk == tq))
    eye2 = jnp.where(ti2 == si2, 1.0, 0.0)
    zeros_l = jnp.zeros((n_l, LANES), BF16)
    bd = (_iota((LANES, LANES), 0) < RW_HEAD_DIM) == (_iota((LANES, LANES), 1) < RW_HEAD_DIM)

    def pair(p):
        sl = pl.ds(p * LANES, LANES)
        rt = rt_s[:, sl]
        at = at_s[:, sl]
        bt_p = bt_s[:, sl]
        kt_p = kt_s[:, sl]
        vv = v_ref[0, rows, sl]
        h0 = h_ref[p]
        h0b = h0.astype(BF16)
        zb = jnp.zeros_like(at)
        ar = jnp.concatenate([at, rt], axis=0)
        g = _dot_nt(ar, jnp.concatenate([jnp.where(lo, bt_p, zb), jnp.where(lo, kt_p, zb),
                                         jnp.where(hi, kt_p, zb), jnp.where(hi, bt_p, zb)], axis=0))
        arh = _dot_nt(ar, h0b)
        yield
        g0 = jnp.where(mask_g, g[:, :LANES], 0.0)
        g1 = jnp.where(mask_g, g[:, LANES:], 0.0)
        a_blk = jnp.concatenate([jnp.where(lo, g0[:n_l], 0.0), jnp.where(hi, g1[:n_l], 0.0)], axis=0)
        v_lo = jnp.where(lo, vv, zb)
        v_hi = jnp.where(hi, vv, zb)
        ga = jnp.concatenate([g0[:n_l], g1[:n_l]], axis=1).astype(BF16)
        rhs = arh[:n_l] + _dot(ga, jnp.concatenate([zeros_l, v_lo, v_hi, zeros_l], axis=0))
        rhs = jnp.concatenate([jnp.where(lo, rhs, 0.0), jnp.where(hi, rhs, 0.0)], axis=0).astype(BF16)
        t_inv = eye2 + a_blk
        ab = a_blk.astype(BF16)
        a_pow = _dot(ab, ab)
        yield
        for _ in range(4):
            ab = a_pow.astype(BF16)
            both = _dot(ab, jnp.concatenate([ab, t_inv.astype(BF16)], axis=1))
            yield
            a_pow = both[:, :LANES]
            t_inv = t_inv + both[:, LANES:]
        t_inv = t_inv + _dot(a_pow.astype(BF16), t_inv.astype(BF16))
        yield
        u_st = _dot(t_inv.astype(BF16), rhs)
        yield
        u = (u_st[:n_l] + u_st[n_l:]).astype(BF16)
        if y_ref is not None:
            gr = jnp.concatenate([g0[n_l:], g1[n_l:]], axis=1).astype(BF16)
            uv_heads = jnp.concatenate([jnp.where(lo, u, zb), v_lo, v_hi, jnp.where(hi, u, zb)], axis=0)
            y_ref[0, rows, sl] = arh[n_l:] + _dot(gr, uv_heads)
        upd = _dot_tn(jnp.concatenate([u, vv], axis=0),
                      jnp.concatenate([bh_s[:, sl], kh_s[:, sl]], axis=0))
        yield
        h_ref[p] = h0 * etot_s[0:1, sl] + jnp.where(bd, upd, 0.0)

    return pair


def _rw_block(refs_f, refs_b, hf_ref, hb_ref, yf_ref, yb_ref, stage_f, stage_b):
    n_l = RW_CHUNK
    n_sub = refs_f[0].shape[1] // n_l
    n_pairs = D_MODEL // LANES

    def body(i, carry):
        rows_f = pl.ds(pl.multiple_of(i * n_l, n_l), n_l)
        rows_b = pl.ds(pl.multiple_of((n_sub - 1 - i) * n_l, n_l), n_l)
        _rw_stage(refs_f, rows_f, stage_f, False)
        _rw_stage(refs_b, rows_b, stage_b, True)
        pair_f = _rw_pair_fn(refs_f[3], rows_f, hf_ref, yf_ref, stage_f, False)
        pair_b = _rw_pair_fn(refs_b[3], rows_b, hb_ref, yb_ref, stage_b, True)
        for p0 in range(0, n_pairs, RW_PAIR_UNROLL):
            _round_robin([f(p) for p in range(p0, p0 + RW_PAIR_UNROLL) for f in (pair_f, pair_b)])
        return carry

    lax.fori_loop(0, n_sub, body, 0)


def _rw_scan_kernel(*refs, n_cc):
    ctx_f, lat_f, ctx_b, lat_b = refs[0:6], refs[6:12], refs[12:18], refs[18:24]
    yf_ref, yb_ref, hf_ref, hb_ref = refs[24:28]
    stage_f, stage_b = refs[28:35], refs[35:42]
    j = pl.program_id(1)

    @pl.when(j == 0)
    def _():
        hf_ref[...] = jnp.zeros_like(hf_ref)
        hb_ref[...] = jnp.zeros_like(hb_ref)

    @pl.when(j < n_cc)
    def _():
        _rw_block(ctx_f, ctx_b, hf_ref, hb_ref, None, None, stage_f, stage_b)

    @pl.when(j >= n_cc)
    def _():
        _rw_block(lat_f, lat_b, hf_ref, hb_ref, yf_ref, yb_ref, stage_f, stage_b)


def _rw_scan(cx, lat):
    n_b, n_t, _ = lat["r"].shape
    n_l = RW_CHUNK
    n_cc = cx["r"].shape[1] // RW_BLOCK
    n_lc = n_t // RW_BLOCK
    blk = (1, RW_BLOCK, D_MODEL)
    ctx_f = pl.BlockSpec(blk, lambda b, j: (b, jnp.minimum(j, n_cc - 1), 0))
    lat_f = pl.BlockSpec(blk, lambda b, j: (b, jnp.maximum(j - n_cc, 0), 0))
    ctx_b = pl.BlockSpec(blk, lambda b, j: (b, jnp.maximum(n_cc - 1 - j, 0), 0))
    lat_b = pl.BlockSpec(blk, lambda b, j: (b, n_lc - 1 - jnp.maximum(j - n_cc, 0), 0))
    fwd = ("r", "lw0", "k0", "v", "a", "b0")
    bwd = ("r", "lw1", "k1", "v", "a", "b1")
    args = ([cx[n] for n in fwd] + [lat[n] for n in fwd] + [cx[n] for n in bwd] + [lat[n] for n in bwd])
    stage = [pltpu.VMEM((n_l, D_MODEL), BF16)] * 6 + [pltpu.VMEM((8, D_MODEL), F32)]
    return pl.pallas_call(
        functools.partial(_rw_scan_kernel, n_cc=n_cc),
        grid=(n_b, n_cc + n_lc),
        in_specs=[ctx_f] * 6 + [lat_f] * 6 + [ctx_b] * 6 + [lat_b] * 6,
        out_specs=[lat_f, lat_b],
        out_shape=[jax.ShapeDtypeStruct((n_b, n_t, D_MODEL), F32)] * 2,
        scratch_shapes=[pltpu.VMEM((D_MODEL // LANES, LANES, LANES), F32)] * 2 + stage + stage,
        compiler_params=_cparams(("parallel", "arbitrary")),
        name="rw_scan",
    )(*args)


def _rw_out_kernel(x_ref, yf_ref, yb_ref, r_ref, k0_ref, k1_ref, v_ref, g_ref, ga_ref, shm_ref, scm_ref, gm_ref,
                   rk_ref, lnw_ref, lnb_ref, bd_ref, wo_ref, w1_ref, w2_ref, o_ref):
    bd = bd_ref[...]
    inv_n = 1.0 / RW_HEAD_DIM
    y = yf_ref[0] + yb_ref[0]
    dlt = y - _group_sum(y, bd) * inv_n
    var = _group_sum(dlt * dlt, bd) * inv_n
    yn = dlt * lax.rsqrt(var + RW_GN_EPS) * lnw_ref[...] + lnb_ref[...]
    r = r_ref[0].astype(F32)
    kd = k0_ref[0].astype(F32) + k1_ref[0].astype(F32)
    coef = _group_sum(r * kd * rk_ref[...], bd)
    o = _dot(((yn + coef * v_ref[0].astype(F32)) * g_ref[0].astype(F32)).astype(BF16), wo_ref[...])
    x1 = x_ref[0] + ga_ref[0] * o
    o_ref[0] = _mlp(x1, shm_ref[0], scm_ref[0], gm_ref[0], w1_ref, w2_ref)


def _rw_out_mlp(x, yf, yb, lat, mods, prm, w1, w2):
    n_b, n_t, _ = x.shape
    tm = min(MLP_TILE, n_t)
    tok = pl.BlockSpec((1, tm, D_MODEL), lambda b, i: (b, i, 0))
    mod = pl.BlockSpec((1, 1, D_MODEL), lambda b, i: (b, 0, 0))
    vec = _const_spec((1, D_MODEL))
    return pl.pallas_call(
        _rw_out_kernel,
        grid=(n_b, n_t // tm),
        in_specs=[tok] * 8 + [mod] * 4 + [vec, vec, vec, _const_spec((GROUP_W, GROUP_W)),
                                          _const_spec((D_MODEL, D_MODEL)),
                                          _const_spec((D_MODEL, D_FF)), _const_spec((D_FF, D_MODEL))],
        out_specs=tok,
        out_shape=jax.ShapeDtypeStruct(x.shape, F32),
        compiler_params=_cparams(("parallel", "parallel")),
        name="rw_out_mlp",
    )(x, yf, yb, lat["r"], lat["k0"], lat["k1"], lat["v"], lat["g"],
      mods["gate_a"], mods["shift_m"], mods["scale_m"], mods["gate_m"],
      prm["r_k"], prm["lnx_w"], prm["lnx_b"], prm["bd"], prm["w_o"], w1, w2)


def _ab_params(w_in, gate_b, q_norm, k_norm, n_tokens):
    n_proj = 3 * NA_WIDTH + 4 * ML_WIDTH
    head_major = np.array([4 * (j % 4) + j // 4 for j in range(4 * ML_HEADS)])
    wg = w_in[:, n_proj:][:, head_major]
    pos = np.arange(n_tokens)
    n_freq = ML_HEAD_DIM // 4
    inv_freq = (ROPE_BASE ** (-np.arange(n_freq, dtype=np.float32) / n_freq)).astype(np.float32)
    ang_r = ((pos // GRID_W).astype(np.float32)[:, None] * inv_freq).astype(np.float64)
    ang_c = ((pos % GRID_W).astype(np.float32)[:, None] * inv_freq).astype(np.float64)
    cos = np.concatenate([np.cos(ang_r)] * 2 + [np.cos(ang_c)] * 2, axis=1).astype(np.float32)
    sin = np.concatenate([-np.sin(ang_r), np.sin(ang_r), -np.sin(ang_c), np.sin(ang_c)], axis=1).astype(np.float32)
    gb = gate_b[head_major]
    return dict(w_all=w_in[:, :n_proj].astype(BF16), wg_t=wg.T.astype(BF16), bd=_group_ones(),
                q_norm=jnp.tile(q_norm, NA_HEADS)[None], k_norm=jnp.tile(k_norm, NA_HEADS)[None],
                cos=jnp.asarray(cos), sin=jnp.asarray(sin), gate_b_t=gb[:, None])


def _rw_params(mu, w_rkv, w0, w1, w2, a0, a1, a2, g1, g2, k_k, k_a, r_k, lnx_w, lnx_b, w_o):
    pad_g = 256 - RW_GATE_LORA
    return dict(
        mu=jnp.concatenate([mu, jnp.zeros((2, D_MODEL), F32)], axis=0),
        w_rkv=w_rkv.astype(BF16),
        w1=jnp.concatenate([w1[0], w1[1]], axis=1).astype(BF16),
        w2=jnp.concatenate([w2[0], w2[1]], axis=0).astype(BF16), w0=w0,
        a1=jnp.concatenate([a1[0], a1[1]], axis=1).astype(BF16),
        a2=jnp.concatenate([a2[0], a2[1]], axis=0).astype(BF16), a0=a0,
        g1=jnp.pad(g1, ((0, 0), (0, pad_g))).astype(BF16), g2=jnp.pad(g2, ((0, pad_g), (0, 0))).astype(BF16),
        k_k=k_k[None], k_a=k_a[None], r_k=r_k[None], lnx_w=lnx_w[None], lnx_b=lnx_b[None],
        bd=_group_ones(), w_o=w_o.astype(BF16))


def _mod_dict(m, n_b):
    names = ("shift_a", "scale_a", "gate_a", "shift_m", "scale_m", "gate_m")
    parts = jnp.split(m, 6, axis=-1)
    return {n: jnp.broadcast_to(p[:, None, :], (n_b, 1, D_MODEL)) for n, p in zip(names, parts)}


_PREP_NAMES = ("r", "v", "g", "a", "k0", "k1", "b0", "b1", "lw0", "lw1")


def kernel(x, c, ctx, c_ctx, ada_w, ada_b, ab_w_in, ab_gate_b, na_q_norm, na_k_norm, na_rpb, ml_head_norm, ab_w_out, rw_mu, rw_w_rkv, rw_w0, rw_w1, rw_w2, rw_a0, rw_a1, rw_a2, rw_g1, rw_g2, rw_k_k, rw_k_a, rw_r_k, rw_lnx_w, rw_lnx_b, rw_w_o, mlp_w1, mlp_w2):
    n_b, n_t, _ = x.shape
    assert ada_w.shape[0] == 2, "even (attention / mLSTM) layer followed by an odd (RWKV-7) layer"
    cond = jnp.concatenate([c, c_ctx[None], jnp.zeros((16 - n_b - 1, D_MODEL), F32)], axis=0)
    mods = _adaln(cond, ada_w, ada_b)
    w1 = mlp_w1.astype(BF16)
    w2 = mlp_w2.astype(BF16)

    m_l = _mod_dict(mods[0, :n_b], n_b)
    m_c = _mod_dict(mods[0, n_b:n_b + 1], n_b)
    prm = _ab_params(ab_w_in[0], ab_gate_b[0], na_q_norm[0], na_k_norm[0], n_t)
    qa_l, ka_l, va_l, qb_l, kb_l, vb_l, ob_l, gt_l = _ab_project(x, m_l["shift_a"], m_l["scale_a"], prm, True)
    qa_c, ka_c, va_c, qb_c, kb_c, vb_c, ob_c, gt_c = _ab_project(ctx, m_c["shift_a"], m_c["scale_a"], prm, False)
    na_l = _na_attention(qa_l, ka_l, va_l, ka_c, va_c, _na_col_tiles(na_rpb[0]))
    na_c = _ctx_attention(qa_c, ka_c, va_c)
    hlf, hlb, hcf, hcb = _mlstm(qb_l, kb_l, vb_l, gt_l, qb_c, kb_c, vb_c, gt_c)
    head_norm = ml_head_norm[0].reshape(1, ML_WIDTH)
    w_out = ab_w_out[0].astype(BF16)
    x = _ab_out_mlp(x, na_l, hlf, hlb, ob_l, m_l, head_norm, w_out, w1[0], w2[0])
    ctx = _ab_out_mlp(ctx, na_c, hcf, hcb, ob_c, m_c, head_norm, w_out, w1[0], w2[0])

    m_l = _mod_dict(mods[1, :n_b], n_b)
    m_c = _mod_dict(mods[1, n_b:n_b + 1], n_b)
    rprm = _rw_params(rw_mu[0], rw_w_rkv[0], rw_w0[0], rw_w1[0], rw_w2[0], rw_a0[0], rw_a1[0], rw_a2[0],
                      rw_g1[0], rw_g2[0], rw_k_k[0], rw_k_a[0], rw_r_k[0], rw_lnx_w[0], rw_lnx_b[0], rw_w_o[0])
    lat = dict(zip(_PREP_NAMES, _rw_prepare(x, m_l["shift_a"], m_l["scale_a"], rprm)))
    cxp = dict(zip(_PREP_NAMES, _rw_prepare(ctx, m_c["shift_a"], m_c["scale_a"], rprm)))
    yf, yb = _rw_scan(cxp, lat)
    return _rw_out_mlp(x, yf, yb, lat, m_l, rprm, w1[1], w2[1])
```

```python
import functools

import numpy as np
import jax
import jax.numpy as jnp
from jax import lax
from jax.experimental import pallas as pl
from jax.experimental.pallas import tpu as pltpu

F32 = jnp.float32
BF16 = jnp.bfloat16

D_MODEL = 1024
GRID_W = 64
NA_HEAD_DIM = 64
NA_HEADS = 8
NA_WIDTH = 512
NA_KH = 8
NA_KW = 16
ML_HEAD_DIM = 128
ML_HEADS = 4
ML_WIDTH = 512
ML_CHUNK = 128
ML_GATE_CAP = 15.0
ML_HEADS_PER_STEP = 4
RW_HEAD_DIM = 64
RW_HEADS = 16
RW_LORA = 64
RW_GATE_LORA = 160
RW_GN_EPS = 64e-5
D_FF = 4 * D_MODEL
ROPE_BASE = 10000.0
NORM_EPS = 1e-6

LANES = 128
GROUP_W = 256
VMEM_LIMIT = 56 * 1024 * 1024
TOKEN_TILE = 512
MLP_TILE = 512
NA_ROWS = 4
NA_KEY_ROWS = NA_ROWS + NA_KH - 1
NA_STRIP = 32
NA_UNROLL = 8
RW_CHUNK = 64
RW_BLOCK = 256
RW_PAIR_UNROLL = 8
NEG_BIG = -1e30
RW_DECAY_SCALE = float(np.exp(-0.5))


def _cparams(sem):
    return pltpu.CompilerParams(dimension_semantics=sem, vmem_limit_bytes=VMEM_LIMIT)


def _const_spec(shape):
    nd = len(shape)
    return pl.BlockSpec(shape, lambda *_: (0,) * nd, pipeline_mode=pl.Buffered(1))


SHIFT_A, SCALE_A, GATE_A, SHIFT_M, SCALE_M, GATE_M = range(6)


def _mod_spec(which, cond_row=None):
    if cond_row is None:
        return pl.BlockSpec((1, 1, D_MODEL), lambda b, i: (6 * b + which, 0, 0))
    return pl.BlockSpec((1, 1, D_MODEL), lambda b, i: (6 * cond_row + which, 0, 0))


def _dot(a, b):
    return jnp.dot(a, b, preferred_element_type=F32)


def _dot_nt(a, b):
    return lax.dot_general(a, b, (((1,), (1,)), ((), ())), preferred_element_type=F32)


def _dot_tn(a, b):
    return lax.dot_general(a, b, (((0,), (0,)), ((), ())), preferred_element_type=F32)


def _split3(x):
    hi = x.astype(BF16)
    r1 = x - hi.astype(F32)
    mid = r1.astype(BF16)
    lo = (r1 - mid.astype(F32)).astype(BF16)
    return hi, mid, lo


def _dot_exact_rhs(a_bf16, x):
    hi, mid, lo = _split3(x)
    return _dot(a_bf16, hi) + _dot(a_bf16, mid) + _dot(a_bf16, lo)


def _dot_exact_lhs(x, a_bf16):
    hi, mid, lo = _split3(x)
    return _dot(hi, a_bf16) + _dot(mid, a_bf16) + _dot(lo, a_bf16)


def _dot_f32(a, b):
    ah = a.astype(BF16)
    al = (a - ah.astype(F32)).astype(BF16)
    bh = b.astype(BF16)
    bl = (b - bh.astype(F32)).astype(BF16)
    return _dot(ah, bh) + _dot(ah, bl) + _dot(al, bh)


def _sigmoid(x):
    return 0.5 * jnp.tanh(0.5 * x) + 0.5


def _log_sigmoid(x):
    return jnp.minimum(x, 0.0) - jnp.log(1.0 + jnp.exp(-jnp.abs(x)))


def _rms(x):
    return x * lax.rsqrt(jnp.mean(x * x, axis=-1, keepdims=True) + NORM_EPS)


def _modulate(x, shift, scale):
    return _rms(x) * (1.0 + scale) + shift


def _iota(shape, dim):
    return lax.broadcasted_iota(jnp.int32, shape, dim)


def _group_sum(x, bd):
    parts = []
    for j in range(x.shape[1] // GROUP_W):
        parts.append(_dot(x[:, j * GROUP_W:(j + 1) * GROUP_W].astype(BF16), bd))
    return jnp.concatenate(parts, axis=1)


def _group_ones():
    grp = np.arange(GROUP_W) // 64
    return jnp.asarray(grp[:, None] == grp[None, :], BF16)


def _round_robin(chains, stagger=False):
    pending = list(chains)
    live = []
    while pending or live:
        if pending:
            live.extend(pending[:1] if stagger else pending)
            del pending[:1 if stagger else len(pending)]
        nxt = []
        for ch in (reversed(live) if stagger else live):
            try:
                next(ch)
                nxt.append(ch)
            except StopIteration:
                pass
        live = nxt[::-1] if stagger else nxt


def _adaln_kernel(c_ref, w_ref, b_ref, o_ref):
    c = c_ref[...]
    s = c * _sigmoid(c)
    o_ref[0] = _dot_f32(s, w_ref[0]) + b_ref[0]


def _adaln(cond, ada_w, ada_b):
    depth = ada_w.shape[0]
    tn = 512
    return pl.pallas_call(
        _adaln_kernel,
        grid=(depth, 6 * D_MODEL // tn),
        in_specs=[
            pl.BlockSpec((16, D_MODEL), lambda l, j: (0, 0)),
            pl.BlockSpec((1, D_MODEL, tn), lambda l, j: (l, 0, j)),
            pl.BlockSpec((1, 1, tn), lambda l, j: (l, 0, j)),
        ],
        out_specs=pl.BlockSpec((1, 16, tn), lambda l, j: (l, 0, j)),
        out_shape=jax.ShapeDtypeStruct((depth, 16, 6 * D_MODEL), F32),
        compiler_params=_cparams(("parallel", "parallel")),
        name="adaln",
    )(cond, ada_w, ada_b.reshape(depth, 1, 6 * D_MODEL))


_C_QA, _C_KA, _C_VA, _C_QB, _C_KB, _C_VB, _C_OB = (0, 512, 1024, 1536, 2048, 2560, 3072)
_AB_COLS = 3584


def _ab_proj_kernel(x_ref, sh_ref, sc_ref, w_ref, wgt_ref, bd_ref, qn_ref, kn_ref, cos_ref, sin_ref, gbt_ref,
                    qa_ref, ka_ref, va_ref, qb_ref, kb_ref, vb_ref, ob_ref, gt_ref, *, rope):
    h = _modulate(x_ref[0], sh_ref[0], sc_ref[0]).astype(BF16)

    def proj(c0, n=512):
        return _dot(h, w_ref[:, c0:c0 + n])

    bd = bd_ref[...]

    def head_norm(y, gain):
        ms = _group_sum(y * y, bd) * (1.0 / NA_HEAD_DIM)
        return y * lax.rsqrt(ms + NORM_EPS) * gain

    qa_ref[0] = (head_norm(proj(_C_QA), qn_ref[...]) * (NA_HEAD_DIM ** -0.5)).astype(BF16)
    ka_ref[0] = head_norm(proj(_C_KA), kn_ref[...]).astype(BF16)
    va_ref[0] = proj(_C_VA).astype(BF16)
    qb = proj(_C_QB)
    kb = proj(_C_KB)
    if rope:
        cos = jnp.concatenate([cos_ref[...]] * ML_HEADS, axis=1)
        sin = jnp.concatenate([sin_ref[...]] * ML_HEADS, axis=1)
        first_half = (_iota((1, ML_WIDTH), 1) & (ML_HEAD_DIM // 4)) == 0

        def partner(y):
            return jnp.where(first_half, pltpu.roll(y, ML_WIDTH - ML_HEAD_DIM // 4, 1),
                             pltpu.roll(y, ML_HEAD_DIM // 4, 1))

        qb = qb * cos + partner(qb) * sin
        kb = kb * cos + partner(kb) * sin
    qb_ref[0] = qb.astype(BF16)
    kb_ref[0] = (kb * (ML_HEAD_DIM ** -0.5)).astype(BF16)
    vb_ref[0] = proj(_C_VB).astype(BF16)
    ob_ref[0] = proj(_C_OB)
    gt_ref[0] = _dot_nt(wgt_ref[...], h) + gbt_ref[...]


def _ab_project(x, mods, cond_row, prm, rope):
    n_b, n_t, _ = x.shape
    tm = min(TOKEN_TILE, n_t)
    tok = lambda w, dt: jax.ShapeDtypeStruct((n_b, n_t, w), dt)
    tok_spec = lambda w: pl.BlockSpec((1, tm, w), lambda b, i: (b, i, 0))
    return pl.pallas_call(
        functools.partial(_ab_proj_kernel, rope=rope),
        grid=(n_b, n_t // tm),
        in_specs=[
            tok_spec(D_MODEL), _mod_spec(SHIFT_A, cond_row), _mod_spec(SCALE_A, cond_row),
            _const_spec((D_MODEL, _AB_COLS)), _const_spec((4 * ML_HEADS, D_MODEL)),
            _const_spec((GROUP_W, GROUP_W)), _const_spec((1, NA_WIDTH)), _const_spec((1, NA_WIDTH)),
            pl.BlockSpec((tm, LANES), lambda b, i: (i, 0)), pl.BlockSpec((tm, LANES), lambda b, i: (i, 0)),
            _const_spec((4 * ML_HEADS, 1)),
        ],
        out_specs=[tok_spec(512)] * 7 + [pl.BlockSpec((1, 4 * ML_HEADS, tm), lambda b, i: (b, 0, i))],
        out_shape=[tok(512, BF16)] * 6 + [tok(512, F32), jax.ShapeDtypeStruct((n_b, 4 * ML_HEADS, n_t), F32)],
        compiler_params=_cparams(("parallel", "parallel")),
        name="ab_proj_rope" if rope else "ab_proj",
    )(x, mods, mods, prm["w_all"], prm["wg_t"], prm["bd"], prm["q_norm"], prm["k_norm"],
      prm["cos"][:n_t], prm["sin"][:n_t], prm["gate_b_t"])


def _na_kernel(q_ref, k_ref, v_ref, kc_ref, vc_ref, tiles_ref, o_ref, bias_ref, *, row_offsets):
    n_t = q_ref.shape[1]
    n_rows = n_t // GRID_W
    rq = NA_ROWS * GRID_W
    n_blk = n_t // rq
    n_keys = NA_KEY_ROWS * GRID_W
    kc = kc_ref[0]
    vc = vc_ref[0]
    lane = _iota((1, LANES), 1)

    @pl.when(pl.program_id(1) == 0)
    def _():
        for case in range(3):
            for hh in range(2):
                for i in range(NA_ROWS):
                    for j in range(NA_KEY_ROWS):
                        bias_ref[case, hh, i * GRID_W:(i + 1) * GRID_W, j * GRID_W:(j + 1) * GRID_W] = (
                            tiles_ref[hh, int(row_offsets[case, i, j])])

    def chain(bi, hh, outs):
        kr0 = jnp.clip(bi * NA_ROWS - NA_KH // 2, 0, n_rows - NA_KEY_ROWS)
        k0 = pl.multiple_of(kr0 * GRID_W, GRID_W)
        case = jnp.where(bi == 0, 0, jnp.where(bi == n_blk - 1, 2, 1))
        q = q_ref[0, pl.ds(pl.multiple_of(bi * rq, rq), rq), :]
        in_head = (lane >= hh * NA_HEAD_DIM) & (lane < (hh + 1) * NA_HEAD_DIM)
        qm = jnp.where(in_head, q, jnp.zeros_like(q))
        s_nb = _dot_nt(qm, k_ref[0, pl.ds(k0, n_keys), :])
        s_cx = _dot_nt(qm, kc)
        yield
        p_nb, p_cx, den = [], [], []
        for r0 in range(0, rq, NA_STRIP):
            rows = slice(r0, r0 + NA_STRIP)
            sn = s_nb[rows] + bias_ref[case, hh, rows, :]
            sc = s_cx[rows]
            m = jnp.maximum(jnp.max(sn, axis=-1, keepdims=True), jnp.max(sc, axis=-1, keepdims=True))
            pn = jnp.exp(sn - m)
            pc = jnp.exp(sc - m)
            den.append(jnp.sum(pn, axis=-1, keepdims=True) + jnp.sum(pc, axis=-1, keepdims=True))
            p_nb.append(pn.astype(BF16))
            p_cx.append(pc.astype(BF16))
        o = (_dot(jnp.concatenate(p_nb, axis=0), v_ref[0, pl.ds(k0, n_keys), :])
             + _dot(jnp.concatenate(p_cx, axis=0), vc))
        yield
        outs[hh] = o / jnp.concatenate(den, axis=0)

    def body(i, carry):
        blocks = [(NA_UNROLL * i + u, [None, None]) for u in range(NA_UNROLL)]
        _round_robin([chain(bi, hh, outs) for bi, outs in blocks for hh in range(2)], stagger=True)
        for bi, outs in blocks:
            o_ref[0, pl.ds(pl.multiple_of(bi * rq, rq), rq), :] = jnp.where(
                lane < NA_HEAD_DIM, outs[0], outs[1]).astype(o_ref.dtype)
        return carry

    lax.fori_loop(0, n_blk // NA_UNROLL, body, 0)


def _na_attention(qa, ka, va, ka_c, va_c, col_tiles):
    n_b, n_t, _ = qa.shape
    n_ctx = ka_c.shape[1]
    seq = pl.BlockSpec((1, n_t, LANES), lambda hp, b: (b, 0, hp))
    cseq = pl.BlockSpec((1, n_ctx, LANES), lambda hp, b: (b, 0, hp))
    return pl.pallas_call(
        functools.partial(_na_kernel, row_offsets=_na_row_offsets(n_t // GRID_W)),
        grid=(NA_HEADS // 2, n_b),
        in_specs=[seq, seq, seq, cseq, cseq,
                  pl.BlockSpec((2,) + col_tiles.shape[1:], lambda hp, b: (hp, 0, 0, 0))],
        out_specs=seq,
        out_shape=jax.ShapeDtypeStruct((n_b, n_t, NA_WIDTH), BF16),
        scratch_shapes=[pltpu.VMEM((3, 2, NA_ROWS * GRID_W, NA_KEY_ROWS * GRID_W), F32)],
        compiler_params=_cparams(("parallel", "arbitrary")),
        name="na_attention",
    )(qa, ka, va, ka_c, va_c, col_tiles)


def _ctx_attn_kernel(q_ref, k_ref, v_ref, o_ref):
    q = q_ref[0]
    k = k_ref[0]
    v = v_ref[0]
    lane = _iota((1, LANES), 1)
    outs = []
    for hh in range(2):
        in_head = (lane >= hh * NA_HEAD_DIM) & (lane < (hh + 1) * NA_HEAD_DIM)
        qm = jnp.where(in_head, q, jnp.zeros_like(q))
        s = _dot_nt(qm, k)
        p = jnp.exp(s - jnp.max(s, axis=-1, keepdims=True))
        outs.append(_dot(p.astype(BF16), v) / jnp.sum(p, axis=-1, keepdims=True))
    o_ref[0] = jnp.where(lane < NA_HEAD_DIM, outs[0], outs[1]).astype(o_ref.dtype)


def _ctx_attention(qa, ka, va):
    n_b, n_ctx, _ = qa.shape
    spec = pl.BlockSpec((1, n_ctx, LANES), lambda b, hp: (b, 0, hp))
    return pl.pallas_call(
        _ctx_attn_kernel,
        grid=(n_b, NA_HEADS // 2),
        in_specs=[spec, spec, spec],
        out_specs=spec,
        out_shape=jax.ShapeDtypeStruct((n_b, n_ctx, NA_WIDTH), BF16),
        compiler_params=_cparams(("parallel", "parallel")),
        name="ctx_attention",
    )(qa, ka, va)


def _na_col_tiles(rpb):
    cq = np.arange(GRID_W)[:, None]
    ck = np.arange(GRID_W)[None, :]
    cs = np.clip(cq - NA_KW // 2, 0, GRID_W - NA_KW)
    col_ok = (ck >= cs) & (ck < cs + NA_KW)
    pad = jnp.pad(rpb, ((0, 0), (0, 0), (GRID_W, GRID_W)))
    lo = GRID_W + NA_KW - 1
    toep = jnp.stack([pad[:, :, lo - c:lo - c + GRID_W] for c in range(GRID_W)], axis=2)
    toep = jnp.where(col_ok[None, None], toep, NEG_BIG)
    return jnp.concatenate([toep, jnp.full_like(toep[:, :1], NEG_BIG)], axis=1)


def _na_row_offsets(n_rows):
    n_blk = n_rows // NA_ROWS
    out = []
    for bi in (0, 1, n_blk - 1):
        r0 = bi * NA_ROWS
        kr0 = int(np.clip(r0 - NA_KH // 2, 0, n_rows - NA_KEY_ROWS))
        rq = r0 + np.arange(NA_ROWS)[:, None]
        rk = kr0 + np.arange(NA_KEY_ROWS)[None, :]
        rs = np.clip(rq - NA_KH // 2, 0, n_rows - NA_KH)
        row_ok = (rk >= rs) & (rk < rs + NA_KH)
        out.append(np.where(row_ok, rk - rq + NA_KH - 1, 2 * NA_KH - 1))
    return np.stack(out)


def _mlstm_prologue(grow_ref, rows_s, cols_s):
    n_l = ML_CHUNK
    n_c = grow_ref.shape[3]
    ti = _iota((n_l, n_l), 0)
    si = _iota((n_l, n_l), 1)
    lower = jnp.where(si <= ti, 1.0, 0.0).astype(BF16)
    upper = jnp.where(si >= ti, 1.0, 0.0).astype(BF16)
    cap = lambda g: ML_GATE_CAP * jnp.tanh(g / ML_GATE_CAP)
    fill = jnp.zeros((LANES - n_c, n_l), F32)
    for hh in range(grow_ref.shape[1]):
        for d in range(2):
            ch = 2 * hh + d
            b_rows = _dot_exact_lhs(_log_sigmoid(cap(grow_ref[0, hh, 2 * d + 1])), lower if d else upper)
            rows_s[ch] = b_rows
            ib_rows = cap(grow_ref[0, hh, 2 * d]) - b_rows
            cols_s[ch] = jnp.concatenate([ib_rows, fill], axis=0).T


def _mlstm_chain(ch, c, q_ref, k_ref, v_ref, rows_s, cols_s, c_s, n_s, m_s, out_ref, accumulate):
    n_l = ML_CHUNK
    hh, rev = ch // 2, ch % 2 == 1
    t0 = pl.multiple_of(c * n_l, n_l)
    hs = pl.ds(hh * ML_HEAD_DIM, ML_HEAD_DIM)
    q = q_ref[0, pl.ds(t0, n_l), hs]
    k = k_ref[0, pl.ds(t0, n_l), hs]
    v = v_ref[0, pl.ds(t0, n_l), hs]
    c_st = c_s[ch]
    n_st = n_s[ch]
    m_st = m_s[ch, 0:1, 0:1]
    b_row = rows_s[ch, pl.ds(c, 1), :]
    lane = _iota((1, LANES), 1)
    ib_col = jnp.sum(jnp.where(lane == c, cols_s[ch], 0.0), axis=1, keepdims=True)
    is_last = lane == (0 if rev else n_l - 1)
    b_last = jnp.sum(jnp.where(is_last, b_row, 0.0), axis=1, keepdims=True)
    kcn = _dot_nt(jnp.concatenate([k, c_st.astype(BF16), n_st.astype(BF16)], axis=0), q)
    yield
    si = _iota((n_l, n_l), 0)
    ti = _iota((n_l, n_l), 1)
    mask = (si >= ti) if rev else (si <= ti)
    log_d = jnp.where(mask, b_row + ib_col, -jnp.inf)
    log_inter = b_row + m_st
    m_t = jnp.maximum(log_inter, jnp.max(log_d, axis=0, keepdims=True))
    m_new = jnp.sum(jnp.where(is_last, m_t, 0.0), axis=1, keepdims=True)
    w_s = jnp.exp(ib_col + (b_last - m_new))
    decay = jnp.exp(b_last + m_st - m_new)
    w_inter = jnp.exp(log_inter - m_t)
    s_w = kcn[:n_l] * jnp.exp(log_d - m_t)
    kw = k.astype(F32) * w_s
    both = _dot_tn(v, jnp.concatenate([s_w.astype(BF16), kw.astype(BF16)], axis=1))
    c_s[ch] = decay * c_st + both[:, n_l:]
    n_s[ch] = jnp.broadcast_to(decay * n_st[0:1] + jnp.sum(kw, axis=0, keepdims=True), n_s.shape[1:])
    m_s[ch] = jnp.broadcast_to(m_new, m_s.shape[1:])
    yield
    num_t = both[:, :n_l] + w_inter * kcn[n_l:2 * n_l]
    den = jnp.sum(s_w, axis=0, keepdims=True) + w_inter * kcn[2 * n_l:2 * n_l + 1]
    h_t = num_t / jnp.maximum(jnp.abs(den), jnp.exp(-m_t))
    if accumulate:
        out_ref[0, pl.ds(t0, n_l), hs] += h_t.T
    else:
        out_ref[0, pl.ds(t0, n_l), hs] = h_t.T


def _mlstm_kernel(ql_ref, kl_ref, vl_ref, gl_ref, qc_ref, kc_ref, vc_ref, gc_ref, hl_ref, hc_ref,
                  rows_l, cols_l, rows_c, cols_c, c_s, n_s, m_s):
    n_l = ML_CHUNK
    n_chains = c_s.shape[0]
    _mlstm_prologue(gc_ref, rows_c, cols_c)
    _mlstm_prologue(gl_ref, rows_l, cols_l)
    c_s[...] = jnp.zeros_like(c_s)
    n_s[...] = jnp.zeros_like(n_s)
    m_s[...] = jnp.zeros_like(m_s)

    def sweep(q_ref, k_ref, v_ref, rows_s, cols_s, out_ref):
        n_c = q_ref.shape[1] // n_l

        def body(accumulate, j, carry):
            _round_robin([_mlstm_chain(ch, (n_c - 1 - j) if ch % 2 else j, q_ref, k_ref, v_ref, rows_s, cols_s,
                                       c_s, n_s, m_s, out_ref, accumulate) for ch in range(n_chains)])
            return carry

        lax.fori_loop(0, n_c // 2, functools.partial(body, False), 0)
        lax.fori_loop(n_c // 2, n_c, functools.partial(body, True), 0)

    sweep(qc_ref, kc_ref, vc_ref, rows_c, cols_c, hc_ref)
    sweep(ql_ref, kl_ref, vl_ref, rows_l, cols_l, hl_ref)


def _mlstm(ql, kl, vl, gl_t, qc, kc, vc, gc_t):
    n_b, n_t, _ = ql.shape
    n_ctx = qc.shape[1]
    n_l = ML_CHUNK
    n_h = ML_HEADS_PER_STEP
    assert (n_t // n_l) % 2 == 0 and (n_ctx // n_l) % 2 == 0

    def gate_rows(g_t, n):
        n_c = n // n_l
        rows = g_t.reshape(n_b, ML_HEADS, 4, n_c, n_l)
        return jnp.pad(rows, ((0, 0),) * 3 + ((0, (-n_c) % 8), (0, 0)))

    glr = gate_rows(gl_t, n_t)
    gcr = gate_rows(gc_t, n_ctx)
    wide = n_h * ML_HEAD_DIM
    n_ch = 2 * n_h

    def seq(n):
        return pl.BlockSpec((1, n, wide), lambda b, hp: (b, 0, hp))

    def gspec(a):
        return pl.BlockSpec((1, n_h) + a.shape[2:], lambda b, hp: (b, hp, 0, 0, 0))

    vm = lambda *s: pltpu.VMEM(s, F32)
    return pl.pallas_call(
        _mlstm_kernel,
        grid=(n_b, ML_HEADS // n_h),
        in_specs=[seq(n_t), seq(n_t), seq(n_t), gspec(glr), seq(n_ctx), seq(n_ctx), seq(n_ctx), gspec(gcr)],
        out_specs=[seq(n_t), seq(n_ctx)],
        out_shape=[jax.ShapeDtypeStruct((n_b, n_t, ML_WIDTH), F32),
                   jax.ShapeDtypeStruct((n_b, n_ctx, ML_WIDTH), F32)],
        scratch_shapes=[vm(n_ch, glr.shape[3], n_l), vm(n_ch, n_l, LANES),
                        vm(n_ch, gcr.shape[3], n_l), vm(n_ch, n_l, LANES),
                        vm(n_ch, ML_HEAD_DIM, ML_HEAD_DIM), vm(n_ch, 16, ML_HEAD_DIM), vm(n_ch, 8, LANES)],
        compiler_params=_cparams(("parallel", "parallel")),
        name="mlstm",
    )(ql, kl, vl, glr, qc, kc, vc, gcr)


def _mlp(x1, sh, sc, gt, w1_ref, w2_ref):
    h = _modulate(x1, sh, sc).astype(BF16)
    acc = jnp.zeros_like(x1)
    n_chunk = 1024
    for c in range(D_FF // n_chunk):
        a = jnp.maximum(_dot(h, w1_ref[:, c * n_chunk:(c + 1) * n_chunk]), 0.0)
        acc = acc + _dot((a * a).astype(BF16), w2_ref[c * n_chunk:(c + 1) * n_chunk, :])
    return x1 + gt * acc


def _ab_out_kernel(x_ref, na_ref, ml_ref, ob_ref, ga_ref, shm_ref, scm_ref, gm_ref, hn_ref,
                   wo_ref, w1_ref, w2_ref, o_ref):
    ml = ml_ref[0]
    parts = []
    for hh in range(ML_HEADS):
        parts.append(_rms(ml[:, hh * ML_HEAD_DIM:(hh + 1) * ML_HEAD_DIM]))
    ml = jnp.concatenate(parts, axis=1) * hn_ref[...] * _sigmoid(ob_ref[0])
    o = _dot(na_ref[0], wo_ref[0:NA_WIDTH, :]) + _dot(ml.astype(BF16), wo_ref[NA_WIDTH:, :])
    x1 = x_ref[0] + ga_ref[0] * o
    o_ref[0] = _mlp(x1, shm_ref[0], scm_ref[0], gm_ref[0], w1_ref, w2_ref)


def _ab_out_mlp(x, na, ml, ob, mods, cond_row, head_norm, w_out, w1, w2):
    n_b, n_t, _ = x.shape
    tm = min(MLP_TILE, n_t)
    tok = lambda w: pl.BlockSpec((1, tm, w), lambda b, i: (b, i, 0))
    return pl.pallas_call(
        _ab_out_kernel,
        grid=(n_b, n_t // tm),
        in_specs=[tok(D_MODEL), tok(512), tok(512), tok(512)]
        + [_mod_spec(j, cond_row) for j in (GATE_A, SHIFT_M, SCALE_M, GATE_M)]
        + [_const_spec((1, ML_WIDTH)), _const_spec((D_MODEL, D_MODEL)),
           _const_spec((D_MODEL, D_FF)), _const_spec((D_FF, D_MODEL))],
        out_specs=tok(D_MODEL),
        out_shape=jax.ShapeDtypeStruct(x.shape, F32),
        compiler_params=_cparams(("parallel", "parallel")),
        name="ab_out_mlp",
    )(x, na, ml, ob, mods, mods, mods, mods, head_norm, w_out, w1, w2)


def _rw_prep_kernel(x_ref, xp_ref, xn_ref, sh_ref, sc_ref, mu_ref, wrkv_ref, w1_ref, w2_ref, w0_ref,
                    a1_ref, a2_ref, a0_ref, g1_ref, g2_ref, kk_ref, ka_ref, bd_ref,
                    r_ref, v_ref, g_ref, a_ref, k0_ref, k1_ref, b0_ref, b1_ref, lw0_ref, lw1_ref):
    i = pl.program_id(1)
    n_i = pl.num_programs(1)
    sh = sh_ref[0]
    sc = sc_ref[0]
    h = _modulate(x_ref[0], sh, sc)
    tm = h.shape[0]
    h_before = _modulate(xp_ref[0, 7:8, :], sh, sc) * jnp.where(i > 0, 1.0, 0.0)
    h_after = _modulate(xn_ref[0, 0:1, :], sh, sc) * jnp.where(i < n_i - 1, 1.0, 0.0)
    row = _iota((tm, 1), 0)
    h_prev = jnp.where(row == 0, h_before, pltpu.roll(h, 1, 0))
    h_next = jnp.where(row == tm - 1, h_after, pltpu.roll(h, tm - 1, 0))
    xx = 0.5 * (h_prev + h_next) - h

    def mix(s):
        return (h + xx * mu_ref[s:s + 1, :]).astype(BF16)

    r = _dot(mix(0), wrkv_ref[0])
    k = _dot(mix(2), wrkv_ref[1])
    v = _dot(mix(3), wrkv_ref[2])
    lane = _iota((1, 2 * RW_LORA), 1)
    hid_w = jnp.tanh(_dot(mix(1), w1_ref[...]))
    hid_a = _dot(mix(4), a1_ref[...])
    g = _dot(_sigmoid(_dot(mix(5), g1_ref[...])).astype(BF16), g2_ref[...])
    kk = k * kk_ref[...]
    ss = _group_sum(kk * kk, bd_ref[...])
    kk = kk * jnp.minimum(lax.rsqrt(ss), 1e12)
    r_ref[0] = r.astype(r_ref.dtype)
    v_ref[0] = v.astype(v_ref.dtype)
    g_ref[0] = g.astype(g_ref.dtype)
    a_ref[0] = (-kk).astype(a_ref.dtype)
    for z, (k_out, b_out, lw_out) in enumerate(((k0_ref, b0_ref, lw0_ref), (k1_ref, b1_ref, lw1_ref))):
        in_dir = (lane >= z * RW_LORA) & (lane < (z + 1) * RW_LORA)
        w_logit = w0_ref[z:z + 1, :] + _dot(jnp.where(in_dir, hid_w, 0.0).astype(BF16), w2_ref[...])
        a = _sigmoid(a0_ref[z:z + 1, :] + _dot(jnp.where(in_dir, hid_a, 0.0).astype(BF16), a2_ref[...]))
        lw_out[0] = -RW_DECAY_SCALE * _sigmoid(w_logit)
        k_out[0] = (k * (1.0 + (a - 1.0) * ka_ref[...])).astype(k_out.dtype)
        b_out[0] = (kk * a).astype(b_out.dtype)


def _rw_prepare(x, mods, cond_row, prm):
    n_b, n_t, _ = x.shape
    tm = min(TOKEN_TILE, n_t)
    tok = pl.BlockSpec((1, tm, D_MODEL), lambda b, i: (b, i, 0))
    n8 = n_t // 8
    prev_spec = pl.BlockSpec((1, 8, D_MODEL), lambda b, i: (b, jnp.maximum(i * (tm // 8) - 1, 0), 0))
    next_spec = pl.BlockSpec((1, 8, D_MODEL), lambda b, i: (b, jnp.minimum((i + 1) * (tm // 8), n8 - 1), 0))
    out = lambda dt: jax.ShapeDtypeStruct((n_b, n_t, D_MODEL), dt)
    return pl.pallas_call(
        _rw_prep_kernel,
        grid=(n_b, n_t // tm),
        in_specs=[tok, prev_spec, next_spec, _mod_spec(SHIFT_A, cond_row), _mod_spec(SCALE_A, cond_row),
                  _const_spec((8, D_MODEL)),
                  _const_spec((3, D_MODEL, D_MODEL)),
                  _const_spec((D_MODEL, 2 * RW_LORA)), _const_spec((2 * RW_LORA, D_MODEL)), _const_spec((2, D_MODEL)),
                  _const_spec((D_MODEL, 2 * RW_LORA)), _const_spec((2 * RW_LORA, D_MODEL)), _const_spec((2, D_MODEL)),
                  _const_spec((D_MODEL, 256)), _const_spec((256, D_MODEL)),
                  _const_spec((1, D_MODEL)), _const_spec((1, D_MODEL)), _const_spec((GROUP_W, GROUP_W))],
        out_specs=[tok] * 10,
        out_shape=[out(BF16)] * 8 + [out(F32)] * 2,
        compiler_params=_cparams(("parallel", "parallel")),
        name="rw_prepare",
    )(x, x, x, mods, mods, prm["mu"], prm["w_rkv"], prm["w1"], prm["w2"], prm["w0"],
      prm["a1"], prm["a2"], prm["a0"], prm["g1"], prm["g2"], prm["k_k"], prm["k_a"], prm["bd"])


def _rw_stage(refs, rows, stage, rev):
    r_ref, lw_ref, k_ref, v_ref, a_ref, b_ref = refs
    n_l = RW_CHUNK
    lw = lw_ref[0, rows, :]
    ti = _iota((n_l, n_l), 0)
    si = _iota((n_l, n_l), 1)
    incl = (si >= ti) if rev else (si <= ti)
    cum = _dot_exact_rhs(jnp.where(incl, 1.0, 0.0).astype(BF16), lw)
    tot = jnp.sum(lw, axis=0, keepdims=True)
    e_pos = jnp.exp(cum)
    e_neg = jnp.exp(-cum)
    e_tot = jnp.exp(tot)
    rt_s, at_s, bt_s, kt_s, bh_s, kh_s, etot_s = stage
    etot_s[...] = jnp.broadcast_to(e_tot, etot_s.shape)
    b = b_ref[0, rows, :].astype(F32)
    k = k_ref[0, rows, :].astype(F32)
    rt_s[...] = (r_ref[0, rows, :].astype(F32) * e_pos).astype(BF16)
    at_s[...] = (a_ref[0, rows, :].astype(F32) * jnp.exp(cum - lw)).astype(BF16)
    bt = b * e_neg
    kt = k * e_neg
    bt_s[...] = bt.astype(BF16)
    kt_s[...] = kt.astype(BF16)
    bh_s[...] = (bt * e_tot).astype(BF16)
    kh_s[...] = (kt * e_tot).astype(BF16)


def _rw_pair_fn(v_ref, rows, h_ref, y_ref, stage, rev):
    n_l = RW_CHUNK
    rt_s, at_s, bt_s, kt_s, bh_s, kh_s, etot_s = stage
    lane = _iota((1, LANES), 1)
    lo = lane < RW_HEAD_DIM
    hi = jnp.logical_not(lo)
    ti2 = _iota((2 * n_l, LANES), 0)
    si2 = _iota((2 * n_l, LANES), 1)
    tq = jnp.where(ti2 >= n_l, ti2 - n_l, ti2)
    sk = jnp.where(si2 >= n_l, si2 - n_l, si2)
    strict2 = (sk > tq) if rev else (sk < tq)
    mask_g = strict2 | ((ti2 >= n_l) & (sk == tq))
    eye2 = jnp.where(ti2 == si2, 1.0, 0.0)
    zeros_l = jnp.zeros((n_l, LANES), BF16)
    bd = (_iota((LANES, LANES), 0) < RW_HEAD_DIM) == (_iota((LANES, LANES), 1) < RW_HEAD_DIM)

    def pair(p):
        sl = pl.ds(p * LANES, LANES)
        rt = rt_s[:, sl]
        at = at_s[:, sl]
        bt_p = bt_s[:, sl]
        kt_p = kt_s[:, sl]
        vv = v_ref[0, rows, sl]
        h0 = h_ref[p]
        h0b = h0.astype(BF16)
        zb = jnp.zeros_like(at)
        ar = jnp.concatenate([at, rt], axis=0)
        g = _dot_nt(ar, jnp.concatenate([jnp.where(lo, bt_p, zb), jnp.where(lo, kt_p, zb),
                                         jnp.where(hi, kt_p, zb), jnp.where(hi, bt_p, zb)], axis=0))
        arh = _dot_nt(ar, h0b)
        yield
        g0 = jnp.where(mask_g, g[:, :LANES], 0.0)
        g1 = jnp.where(mask_g, g[:, LANES:], 0.0)
        a_blk = jnp.concatenate([jnp.where(lo, g0[:n_l], 0.0), jnp.where(hi, g1[:n_l], 0.0)], axis=0)
        v_lo = jnp.where(lo, vv, zb)
        v_hi = jnp.where(hi, vv, zb)
        ga = jnp.concatenate([g0[:n_l], g1[:n_l]], axis=1).astype(BF16)
        rhs = arh[:n_l] + _dot(ga, jnp.concatenate([zeros_l, v_lo, v_hi, zeros_l], axis=0))
        rhs = jnp.concatenate([jnp.where(lo, rhs, 0.0), jnp.where(hi, rhs, 0.0)], axis=0).astype(BF16)
        t_inv = eye2 + a_blk
        ab = a_blk.astype(BF16)
        a_pow = _dot(ab, ab)
        yield
        for _ in range(4):
            ab = a_pow.astype(BF16)
            both = _dot(ab, jnp.concatenate([ab, t_inv.astype(BF16)], axis=1))
            yield
            a_pow = both[:, :LANES]
            t_inv = t_inv + both[:, LANES:]
        t_inv = t_inv + _dot(a_pow.astype(BF16), t_inv.astype(BF16))
        yield
        u_st = _dot(t_inv.astype(BF16), rhs)
        yield
        u = (u_st[:n_l] + u_st[n_l:]).astype(BF16)
        if y_ref is not None:
            gr = jnp.concatenate([g0[n_l:], g1[n_l:]], axis=1).astype(BF16)
            uv_heads = jnp.concatenate([jnp.where(lo, u, zb), v_lo, v_hi, jnp.where(hi, u, zb)], axis=0)
            y_ref[0, rows, sl] = arh[n_l:] + _dot(gr, uv_heads)
        upd = _dot_tn(jnp.concatenate([u, vv], axis=0),
                      jnp.concatenate([bh_s[:, sl], kh_s[:, sl]], axis=0))
        yield
        h_ref[p] = h0 * etot_s[0:1, sl] + jnp.where(bd, upd, 0.0)

    return pair


def _rw_block(refs_f, refs_b, hf_ref, hb_ref, yf_ref, yb_ref, stage_f, stage_b):
    n_l = RW_CHUNK
    n_sub = refs_f[0].shape[1] // n_l
    n_pairs = D_MODEL // LANES

    def body(i, carry):
        rows_f = pl.ds(pl.multiple_of(i * n_l, n_l), n_l)
        rows_b = pl.ds(pl.multiple_of((n_sub - 1 - i) * n_l, n_l), n_l)
        _rw_stage(refs_f, rows_f, stage_f, False)
        _rw_stage(refs_b, rows_b, stage_b, True)
        pair_f = _rw_pair_fn(refs_f[3], rows_f, hf_ref, yf_ref, stage_f, False)
        pair_b = _rw_pair_fn(refs_b[3], rows_b, hb_ref, yb_ref, stage_b, True)
        for p0 in range(0, n_pairs, RW_PAIR_UNROLL):
            _round_robin([f(p) for p in range(p0, p0 + RW_PAIR_UNROLL) for f in (pair_f, pair_b)])
        return carry

    lax.fori_loop(0, n_sub, body, 0)


def _rw_scan_kernel(*refs, n_cc):
    ctx_f, lat_f, ctx_b, lat_b = refs[0:6], refs[6:12], refs[12:18], refs[18:24]
    yf_ref, yb_ref, hf_ref, hb_ref = refs[24:28]
    stage_f, stage_b = refs[28:35], refs[35:42]
    j = pl.program_id(1)

    @pl.when(j == 0)
    def _():
        hf_ref[...] = jnp.zeros_like(hf_ref)
        hb_ref[...] = jnp.zeros_like(hb_ref)

    @pl.when(j < n_cc)
    def _():
        _rw_block(ctx_f, ctx_b, hf_ref, hb_ref, None, None, stage_f, stage_b)

    @pl.when(j >= n_cc)
    def _():
        _rw_block(lat_f, lat_b, hf_ref, hb_ref, yf_ref, yb_ref, stage_f, stage_b)


def _rw_scan(cx, lat):
    n_b, n_t, _ = lat["r"].shape
    n_l = RW_CHUNK
    n_cc = cx["r"].shape[1] // RW_BLOCK
    n_lc = n_t // RW_BLOCK
    blk = (1, RW_BLOCK, D_MODEL)
    ctx_f = pl.BlockSpec(blk, lambda b, j: (b, jnp.minimum(j, n_cc - 1), 0))
    lat_f = pl.BlockSpec(blk, lambda b, j: (b, jnp.maximum(j - n_cc, 0), 0))
    ctx_b = pl.BlockSpec(blk, lambda b, j: (b, jnp.maximum(n_cc - 1 - j, 0), 0))
    lat_b = pl.BlockSpec(blk, lambda b, j: (b, n_lc - 1 - jnp.maximum(j - n_cc, 0), 0))
    fwd = ("r", "lw0", "k0", "v", "a", "b0")
    bwd = ("r", "lw1", "k1", "v", "a", "b1")
    args = ([cx[n] for n in fwd] + [lat[n] for n in fwd] + [cx[n] for n in bwd] + [lat[n] for n in bwd])
    stage = [pltpu.VMEM((n_l, D_MODEL), BF16)] * 6 + [pltpu.VMEM((8, D_MODEL), F32)]
    return pl.pallas_call(
        functools.partial(_rw_scan_kernel, n_cc=n_cc),
        grid=(n_b, n_cc + n_lc),
        in_specs=[ctx_f] * 6 + [lat_f] * 6 + [ctx_b] * 6 + [lat_b] * 6,
        out_specs=[lat_f, lat_b],
        out_shape=[jax.ShapeDtypeStruct((n_b, n_t, D_MODEL), F32)] * 2,
        scratch_shapes=[pltpu.VMEM((D_MODEL // LANES, LANES, LANES), F32)] * 2 + stage + stage,
        compiler_params=_cparams(("parallel", "arbitrary")),
        name="rw_scan",
    )(*args)


def _rw_out_kernel(x_ref, yf_ref, yb_ref, r_ref, k0_ref, k1_ref, v_ref, g_ref, ga_ref, shm_ref, scm_ref, gm_ref,
                   rk_ref, lnw_ref, lnb_ref, bd_ref, wo_ref, w1_ref, w2_ref, o_ref):
    bd = bd_ref[...]
    inv_n = 1.0 / RW_HEAD_DIM
    y = yf_ref[0] + yb_ref[0]
    dlt = y - _group_sum(y, bd) * inv_n
    var = _group_sum(dlt * dlt, bd) * inv_n
    yn = dlt * lax.rsqrt(var + RW_GN_EPS) * lnw_ref[...] + lnb_ref[...]
    r = r_ref[0].astype(F32)
    kd = k0_ref[0].astype(F32) + k1_ref[0].astype(F32)
    coef = _group_sum(r * kd * rk_ref[...], bd)
    o = _dot(((yn + coef * v_ref[0].astype(F32)) * g_ref[0].astype(F32)).astype(BF16), wo_ref[...])
    x1 = x_ref[0] + ga_ref[0] * o
    o_ref[0] = _mlp(x1, shm_ref[0], scm_ref[0], gm_ref[0], w1_ref, w2_ref)


def _rw_out_mlp(x, yf, yb, lat, mods, prm, w1, w2):
    n_b, n_t, _ = x.shape
    tm = min(MLP_TILE, n_t)
    tok = pl.BlockSpec((1, tm, D_MODEL), lambda b, i: (b, i, 0))
    mod = [_mod_spec(j) for j in (GATE_A, SHIFT_M, SCALE_M, GATE_M)]
    vec = _const_spec((1, D_MODEL))
    return pl.pallas_call(
        _rw_out_kernel,
        grid=(n_b, n_t // tm),
        in_specs=[tok] * 8 + mod + [vec, vec, vec, _const_spec((GROUP_W, GROUP_W)),
                                          _const_spec((D_MODEL, D_MODEL)),
                                          _const_spec((D_MODEL, D_FF)), _const_spec((D_FF, D_MODEL))],
        out_specs=tok,
        out_shape=jax.ShapeDtypeStruct(x.shape, F32),
        compiler_params=_cparams(("parallel", "parallel")),
        name="rw_out_mlp",
    )(x, yf, yb, lat["r"], lat["k0"], lat["k1"], lat["v"], lat["g"],
      mods, mods, mods, mods, prm["r_k"], prm["lnx_w"], prm["lnx_b"], prm["bd"], prm["w_o"], w1, w2)


def _ab_params(w_in, gate_b, q_norm, k_norm, n_tokens):
    n_proj = 3 * NA_WIDTH + 4 * ML_WIDTH
    head_major = np.array([4 * (j % 4) + j // 4 for j in range(4 * ML_HEADS)])
    wg = w_in[:, n_proj:][:, head_major]
    pos = np.arange(n_tokens)
    n_freq = ML_HEAD_DIM // 4
    inv_freq = (ROPE_BASE ** (-np.arange(n_freq, dtype=np.float32) / n_freq)).astype(np.float32)
    ang_r = ((pos // GRID_W).astype(np.float32)[:, None] * inv_freq).astype(np.float64)
    ang_c = ((pos % GRID_W).astype(np.float32)[:, None] * inv_freq).astype(np.float64)
    cos = np.concatenate([np.cos(ang_r)] * 2 + [np.cos(ang_c)] * 2, axis=1).astype(np.float32)
    sin = np.concatenate([-np.sin(ang_r), np.sin(ang_r), -np.sin(ang_c), np.sin(ang_c)], axis=1).astype(np.float32)
    gb = gate_b[head_major]
    return dict(w_all=w_in[:, :n_proj].astype(BF16), wg_t=wg.T.astype(BF16), bd=_group_ones(),
                q_norm=jnp.tile(q_norm, NA_HEADS)[None], k_norm=jnp.tile(k_norm, NA_HEADS)[None],
                cos=jnp.asarray(cos), sin=jnp.asarray(sin), gate_b_t=gb[:, None])


def _rw_params(mu, w_rkv, w0, w1, w2, a0, a1, a2, g1, g2, k_k, k_a, r_k, lnx_w, lnx_b, w_o):
    pad_g = 256 - RW_GATE_LORA
    return dict(
        mu=jnp.concatenate([mu, jnp.zeros((2, D_MODEL), F32)], axis=0),
        w_rkv=w_rkv.astype(BF16),
        w1=jnp.concatenate([w1[0], w1[1]], axis=1).astype(BF16),
        w2=jnp.concatenate([w2[0], w2[1]], axis=0).astype(BF16), w0=w0,
        a1=jnp.concatenate([a1[0], a1[1]], axis=1).astype(BF16),
        a2=jnp.concatenate([a2[0], a2[1]], axis=0).astype(BF16), a0=a0,
        g1=jnp.pad(g1, ((0, 0), (0, pad_g))).astype(BF16), g2=jnp.pad(g2, ((0, pad_g), (0, 0))).astype(BF16),
        k_k=k_k[None], k_a=k_a[None], r_k=r_k[None], lnx_w=lnx_w[None], lnx_b=lnx_b[None],
        bd=_group_ones(), w_o=w_o.astype(BF16))


_PREP_NAMES = ("r", "v", "g", "a", "k0", "k1", "b0", "b1", "lw0", "lw1")


def kernel(x, c, ctx, c_ctx, ada_w, ada_b, ab_w_in, ab_gate_b, na_q_norm, na_k_norm, na_rpb, ml_head_norm, ab_w_out, rw_mu, rw_w_rkv, rw_w0, rw_w1, rw_w2, rw_a0, rw_a1, rw_a2, rw_g1, rw_g2, rw_k_k, rw_k_a, rw_r_k, rw_lnx_w, rw_lnx_b, rw_w_o, mlp_w1, mlp_w2):
    n_b, n_t, _ = x.shape
    assert ada_w.shape[0] == 2, "even (attention / mLSTM) layer followed by an odd (RWKV-7) layer"
    cond = jnp.concatenate([c, c_ctx[None], jnp.zeros((16 - n_b - 1, D_MODEL), F32)], axis=0)
    mods = _adaln(cond, ada_w, ada_b).reshape(2, 16 * 6, 1, D_MODEL)
    ctx_row = n_b

    prm = _ab_params(ab_w_in[0], ab_gate_b[0], na_q_norm[0], na_k_norm[0], n_t)
    qa_l, ka_l, va_l, qb_l, kb_l, vb_l, ob_l, gt_l = _ab_project(x, mods[0], None, prm, True)
    qa_c, ka_c, va_c, qb_c, kb_c, vb_c, ob_c, gt_c = _ab_project(ctx, mods[0], ctx_row, prm, False)
    na_l = _na_attention(qa_l, ka_l, va_l, ka_c, va_c, _na_col_tiles(na_rpb[0]))
    na_c = _ctx_attention(qa_c, ka_c, va_c)
    ml_l, ml_c = _mlstm(qb_l, kb_l, vb_l, gt_l, qb_c, kb_c, vb_c, gt_c)
    head_norm = ml_head_norm[0].reshape(1, ML_WIDTH)
    w_out = ab_w_out[0].astype(BF16)
    w1 = mlp_w1[0].astype(BF16)
    w2 = mlp_w2[0].astype(BF16)
    x = _ab_out_mlp(x, na_l, ml_l, ob_l, mods[0], None, head_norm, w_out, w1, w2)
    ctx = _ab_out_mlp(ctx, na_c, ml_c, ob_c, mods[0], ctx_row, head_norm, w_out, w1, w2)

    rprm = _rw_params(rw_mu[0], rw_w_rkv[0], rw_w0[0], rw_w1[0], rw_w2[0], rw_a0[0], rw_a1[0], rw_a2[0],
                      rw_g1[0], rw_g2[0], rw_k_k[0], rw_k_a[0], rw_r_k[0], rw_lnx_w[0], rw_lnx_b[0], rw_w_o[0])
    lat = dict(zip(_PREP_NAMES, _rw_prepare(x, mods[1], None, rprm)))
    cxp = dict(zip(_PREP_NAMES, _rw_prepare(ctx, mods[1], ctx_row, rprm)))
    yf, yb = _rw_scan(cxp, lat)
    return _rw_out_mlp(x, yf, yb, lat, mods[1], rprm, mlp_w1[1].astype(BF16), mlp_w2[1].astype(BF16))
```

```python
import functools

import numpy as np
import jax
import jax.numpy as jnp
from jax import lax
from jax.experimental import pallas as pl
from jax.experimental.pallas import tpu as pltpu

F32 = jnp.float32
BF16 = jnp.bfloat16

D_MODEL = 1024
GRID_W = 64
NA_HEAD_DIM = 64
NA_HEADS = 8
NA_WIDTH = 512
NA_KH = 8
NA_KW = 16
ML_HEAD_DIM = 128
ML_HEADS = 4
ML_WIDTH = 512
ML_CHUNK = 128
ML_GATE_CAP = 15.0
ML_HEADS_PER_STEP = 4
RW_HEAD_DIM = 64
RW_HEADS = 16
RW_LORA = 64
RW_GATE_LORA = 160
RW_GN_EPS = 64e-5
D_FF = 4 * D_MODEL
ROPE_BASE = 10000.0
NORM_EPS = 1e-6

LANES = 128
GROUP_W = 256
VMEM_LIMIT = 56 * 1024 * 1024
TOKEN_TILE = 512
MLP_TILE = 512
NA_ROWS = 4
NA_KEY_ROWS = NA_ROWS + NA_KH - 1
NA_STRIP = 32
NA_UNROLL = 8
RW_CHUNK = 64
RW_BLOCK = 256
RW_CHUNK_DELAY = 6
NEG_BIG = -1e30
RW_DECAY_SCALE = float(np.exp(-0.5))


def _cparams(sem):
    return pltpu.CompilerParams(dimension_semantics=sem, vmem_limit_bytes=VMEM_LIMIT)


def _const_spec(shape):
    nd = len(shape)
    return pl.BlockSpec(shape, lambda *_: (0,) * nd, pipeline_mode=pl.Buffered(1))


SHIFT_A, SCALE_A, GATE_A, SHIFT_M, SCALE_M, GATE_M = range(6)


def _mod_spec(which, cond_row=None):
    if cond_row is None:
        return pl.BlockSpec((1, 1, D_MODEL), lambda b, i: (6 * b + which, 0, 0))
    return pl.BlockSpec((1, 1, D_MODEL), lambda b, i: (6 * cond_row + which, 0, 0))


def _dot(a, b):
    return jnp.dot(a, b, preferred_element_type=F32)


def _dot_nt(a, b):
    return lax.dot_general(a, b, (((1,), (1,)), ((), ())), preferred_element_type=F32)


def _dot_tn(a, b):
    return lax.dot_general(a, b, (((0,), (0,)), ((), ())), preferred_element_type=F32)


def _split3(x):
    hi = x.astype(BF16)
    r1 = x - hi.astype(F32)
    mid = r1.astype(BF16)
    lo = (r1 - mid.astype(F32)).astype(BF16)
    return hi, mid, lo


def _dot_exact_rhs(a_bf16, x):
    hi, mid, lo = _split3(x)
    return _dot(a_bf16, hi) + _dot(a_bf16, mid) + _dot(a_bf16, lo)


def _dot_exact_lhs(x, a_bf16):
    hi, mid, lo = _split3(x)
    return _dot(hi, a_bf16) + _dot(mid, a_bf16) + _dot(lo, a_bf16)


def _dot_f32(a, b):
    ah = a.astype(BF16)
    al = (a - ah.astype(F32)).astype(BF16)
    bh = b.astype(BF16)
    bl = (b - bh.astype(F32)).astype(BF16)
    return _dot(ah, bh) + _dot(ah, bl) + _dot(al, bh)


def _sigmoid(x):
    return 0.5 * jnp.tanh(0.5 * x) + 0.5


def _log_sigmoid(x):
    return jnp.minimum(x, 0.0) - jnp.log(1.0 + jnp.exp(-jnp.abs(x)))


def _rms(x):
    return x * lax.rsqrt(jnp.mean(x * x, axis=-1, keepdims=True) + NORM_EPS)


def _modulate(x, shift, scale):
    return _rms(x) * (1.0 + scale) + shift


def _iota(shape, dim):
    return lax.broadcasted_iota(jnp.int32, shape, dim)


def _run_timeline(timeline):
    live = list(timeline)
    rnd = 0
    while live:
        nxt = []
        for start, gen in live:
            if start > rnd:
                nxt.append((start, gen))
                continue
            try:
                next(gen)
                nxt.append((start, gen))
            except StopIteration:
                pass
        live = nxt
        rnd += 1


def _group_sum(x, bd):
    parts = []
    for j in range(x.shape[1] // GROUP_W):
        parts.append(_dot(x[:, j * GROUP_W:(j + 1) * GROUP_W].astype(BF16), bd))
    return jnp.concatenate(parts, axis=1)


def _group_ones():
    grp = np.arange(GROUP_W) // 64
    return jnp.asarray(grp[:, None] == grp[None, :], BF16)


def _round_robin(chains, stagger=False):
    pending = list(chains)
    live = []
    while pending or live:
        if pending:
            live.extend(pending[:1] if stagger else pending)
            del pending[:1 if stagger else len(pending)]
        nxt = []
        for ch in (reversed(live) if stagger else live):
            try:
                next(ch)
                nxt.append(ch)
            except StopIteration:
                pass
        live = nxt[::-1] if stagger else nxt


def _adaln_kernel(c_ref, w_ref, b_ref, o_ref):
    c = c_ref[...]
    s = c * _sigmoid(c)
    o_ref[0] = _dot_f32(s, w_ref[0]) + b_ref[0]


def _adaln(cond, ada_w, ada_b):
    depth = ada_w.shape[0]
    tn = 512
    return pl.pallas_call(
        _adaln_kernel,
        grid=(depth, 6 * D_MODEL // tn),
        in_specs=[
            pl.BlockSpec((16, D_MODEL), lambda l, j: (0, 0)),
            pl.BlockSpec((1, D_MODEL, tn), lambda l, j: (l, 0, j)),
            pl.BlockSpec((1, 1, tn), lambda l, j: (l, 0, j)),
        ],
        out_specs=pl.BlockSpec((1, 16, tn), lambda l, j: (l, 0, j)),
        out_shape=jax.ShapeDtypeStruct((depth, 16, 6 * D_MODEL), F32),
        compiler_params=_cparams(("parallel", "parallel")),
        name="adaln",
    )(cond, ada_w, ada_b.reshape(depth, 1, 6 * D_MODEL))


_C_QA, _C_KA, _C_VA, _C_QB, _C_KB, _C_VB, _C_OB = (0, 512, 1024, 1536, 2048, 2560, 3072)
_AB_COLS = 3584


def _ab_proj_kernel(x_ref, sh_ref, sc_ref, w_ref, wgt_ref, bd_ref, qn_ref, kn_ref, cos_ref, sin_ref, gbt_ref,
                    qa_ref, ka_ref, va_ref, qb_ref, kb_ref, vb_ref, ob_ref, gt_ref, *, rope):
    h = _modulate(x_ref[0], sh_ref[0], sc_ref[0]).astype(BF16)

    def proj(c0, n=512):
        return _dot(h, w_ref[:, c0:c0 + n])

    bd = bd_ref[...]

    def head_norm(y, gain):
        ms = _group_sum(y * y, bd) * (1.0 / NA_HEAD_DIM)
        return y * lax.rsqrt(ms + NORM_EPS) * gain

    qa_ref[0] = (head_norm(proj(_C_QA), qn_ref[...]) * (NA_HEAD_DIM ** -0.5)).astype(BF16)
    ka_ref[0] = head_norm(proj(_C_KA), kn_ref[...]).astype(BF16)
    va_ref[0] = proj(_C_VA).astype(BF16)
    qb = proj(_C_QB)
    kb = proj(_C_KB)
    if rope:
        cos = jnp.concatenate([cos_ref[...]] * ML_HEADS, axis=1)
        sin = jnp.concatenate([sin_ref[...]] * ML_HEADS, axis=1)
        first_half = (_iota((1, ML_WIDTH), 1) & (ML_HEAD_DIM // 4)) == 0

        def partner(y):
            return jnp.where(first_half, pltpu.roll(y, ML_WIDTH - ML_HEAD_DIM // 4, 1),
                             pltpu.roll(y, ML_HEAD_DIM // 4, 1))

        qb = qb * cos + partner(qb) * sin
        kb = kb * cos + partner(kb) * sin
    qb_ref[0] = qb.astype(BF16)
    kb_ref[0] = (kb * (ML_HEAD_DIM ** -0.5)).astype(BF16)
    vb_ref[0] = proj(_C_VB).astype(BF16)
    ob_ref[0] = proj(_C_OB)
    gt_ref[0] = _dot_nt(wgt_ref[...], h) + gbt_ref[...]


def _ab_project(x, mods, cond_row, prm, rope):
    n_b, n_t, _ = x.shape
    tm = min(TOKEN_TILE, n_t)
    tok = lambda w, dt: jax.ShapeDtypeStruct((n_b, n_t, w), dt)
    tok_spec = lambda w: pl.BlockSpec((1, tm, w), lambda b, i: (b, i, 0))
    return pl.pallas_call(
        functools.partial(_ab_proj_kernel, rope=rope),
        grid=(n_b, n_t // tm),
        in_specs=[
            tok_spec(D_MODEL), _mod_spec(SHIFT_A, cond_row), _mod_spec(SCALE_A, cond_row),
            _const_spec((D_MODEL, _AB_COLS)), _const_spec((4 * ML_HEADS, D_MODEL)),
            _const_spec((GROUP_W, GROUP_W)), _const_spec((1, NA_WIDTH)), _const_spec((1, NA_WIDTH)),
            pl.BlockSpec((tm, LANES), lambda b, i: (i, 0)), pl.BlockSpec((tm, LANES), lambda b, i: (i, 0)),
            _const_spec((4 * ML_HEADS, 1)),
        ],
        out_specs=[tok_spec(512)] * 7 + [pl.BlockSpec((1, 4 * ML_HEADS, tm), lambda b, i: (b, 0, i))],
        out_shape=[tok(512, BF16)] * 6 + [tok(512, F32), jax.ShapeDtypeStruct((n_b, 4 * ML_HEADS, n_t), F32)],
        compiler_params=_cparams(("parallel", "parallel")),
        name="ab_proj_rope" if rope else "ab_proj",
    )(x, mods, mods, prm["w_all"], prm["wg_t"], prm["bd"], prm["q_norm"], prm["k_norm"],
      prm["cos"][:n_t], prm["sin"][:n_t], prm["gate_b_t"])


def _na_kernel(q_ref, k_ref, v_ref, kc_ref, vc_ref, tiles_ref, o_ref, bias_ref, *, row_offsets):
    n_t = q_ref.shape[1]
    n_rows = n_t // GRID_W
    rq = NA_ROWS * GRID_W
    n_blk = n_t // rq
    n_keys = NA_KEY_ROWS * GRID_W
    kc = kc_ref[0]
    vc = vc_ref[0]
    lane = _iota((1, LANES), 1)

    @pl.when(pl.program_id(1) == 0)
    def _():
        for case in range(3):
            for hh in range(2):
                for i in range(NA_ROWS):
                    for j in range(NA_KEY_ROWS):
                        bias_ref[case, hh, i * GRID_W:(i + 1) * GRID_W, j * GRID_W:(j + 1) * GRID_W] = (
                            tiles_ref[hh, int(row_offsets[case, i, j])])

    def chain(bi, hh, outs):
        kr0 = jnp.clip(bi * NA_ROWS - NA_KH // 2, 0, n_rows - NA_KEY_ROWS)
        k0 = pl.multiple_of(kr0 * GRID_W, GRID_W)
        case = jnp.where(bi == 0, 0, jnp.where(bi == n_blk - 1, 2, 1))
        q = q_ref[0, pl.ds(pl.multiple_of(bi * rq, rq), rq), :]
        in_head = (lane >= hh * NA_HEAD_DIM) & (lane < (hh + 1) * NA_HEAD_DIM)
        qm = jnp.where(in_head, q, jnp.zeros_like(q))
        s_nb = _dot_nt(qm, k_ref[0, pl.ds(k0, n_keys), :])
        s_cx = _dot_nt(qm, kc)
        yield
        p_nb, p_cx, den = [], [], []
        for r0 in range(0, rq, NA_STRIP):
            rows = slice(r0, r0 + NA_STRIP)
            sn = s_nb[rows] + bias_ref[case, hh, rows, :]
            sc = s_cx[rows]
            m = jnp.maximum(jnp.max(sn, axis=-1, keepdims=True), jnp.max(sc, axis=-1, keepdims=True))
            pn = jnp.exp(sn - m)
            pc = jnp.exp(sc - m)
            den.append(jnp.sum(pn, axis=-1, keepdims=True) + jnp.sum(pc, axis=-1, keepdims=True))
            p_nb.append(pn.astype(BF16))
            p_cx.append(pc.astype(BF16))
        o = (_dot(jnp.concatenate(p_nb, axis=0), v_ref[0, pl.ds(k0, n_keys), :])
             + _dot(jnp.concatenate(p_cx, axis=0), vc))
        yield
        outs[hh] = o / jnp.concatenate(den, axis=0)

    def body(i, carry):
        blocks = [(NA_UNROLL * i + u, [None, None]) for u in range(NA_UNROLL)]
        _round_robin([chain(bi, hh, outs) for bi, outs in blocks for hh in range(2)], stagger=True)
        for bi, outs in blocks:
            o_ref[0, pl.ds(pl.multiple_of(bi * rq, rq), rq), :] = jnp.where(
                lane < NA_HEAD_DIM, outs[0], outs[1]).astype(o_ref.dtype)
        return carry

    lax.fori_loop(0, n_blk // NA_UNROLL, body, 0)


def _na_attention(qa, ka, va, ka_c, va_c, col_tiles):
    n_b, n_t, _ = qa.shape
    n_ctx = ka_c.shape[1]
    seq = pl.BlockSpec((1, n_t, LANES), lambda hp, b: (b, 0, hp))
    cseq = pl.BlockSpec((1, n_ctx, LANES), lambda hp, b: (b, 0, hp))
    return pl.pallas_call(
        functools.partial(_na_kernel, row_offsets=_na_row_offsets(n_t // GRID_W)),
        grid=(NA_HEADS // 2, n_b),
        in_specs=[seq, seq, seq, cseq, cseq,
                  pl.BlockSpec((2,) + col_tiles.shape[1:], lambda hp, b: (hp, 0, 0, 0))],
        out_specs=seq,
        out_shape=jax.ShapeDtypeStruct((n_b, n_t, NA_WIDTH), BF16),
        scratch_shapes=[pltpu.VMEM((3, 2, NA_ROWS * GRID_W, NA_KEY_ROWS * GRID_W), F32)],
        compiler_params=_cparams(("parallel", "arbitrary")),
        name="na_attention",
    )(qa, ka, va, ka_c, va_c, col_tiles)


def _ctx_attn_kernel(q_ref, k_ref, v_ref, o_ref):
    q = q_ref[0]
    k = k_ref[0]
    v = v_ref[0]
    lane = _iota((1, LANES), 1)
    outs = []
    for hh in range(2):
        in_head = (lane >= hh * NA_HEAD_DIM) & (lane < (hh + 1) * NA_HEAD_DIM)
        qm = jnp.where(in_head, q, jnp.zeros_like(q))
        s = _dot_nt(qm, k)
        p = jnp.exp(s - jnp.max(s, axis=-1, keepdims=True))
        outs.append(_dot(p.astype(BF16), v) / jnp.sum(p, axis=-1, keepdims=True))
    o_ref[0] = jnp.where(lane < NA_HEAD_DIM, outs[0], outs[1]).astype(o_ref.dtype)


def _ctx_attention(qa, ka, va):
    n_b, n_ctx, _ = qa.shape
    spec = pl.BlockSpec((1, n_ctx, LANES), lambda b, hp: (b, 0, hp))
    return pl.pallas_call(
        _ctx_attn_kernel,
        grid=(n_b, NA_HEADS // 2),
        in_specs=[spec, spec, spec],
        out_specs=spec,
        out_shape=jax.ShapeDtypeStruct((n_b, n_ctx, NA_WIDTH), BF16),
        compiler_params=_cparams(("parallel", "parallel")),
        name="ctx_attention",
    )(qa, ka, va)


def _na_col_tiles(rpb):
    cq = np.arange(GRID_W)[:, None]
    ck = np.arange(GRID_W)[None, :]
    cs = np.clip(cq - NA_KW // 2, 0, GRID_W - NA_KW)
    col_ok = (ck >= cs) & (ck < cs + NA_KW)
    pad = jnp.pad(rpb, ((0, 0), (0, 0), (GRID_W, GRID_W)))
    lo = GRID_W + NA_KW - 1
    toep = jnp.stack([pad[:, :, lo - c:lo - c + GRID_W] for c in range(GRID_W)], axis=2)
    toep = jnp.where(col_ok[None, None], toep, NEG_BIG)
    return jnp.concatenate([toep, jnp.full_like(toep[:, :1], NEG_BIG)], axis=1)


def _na_row_offsets(n_rows):
    n_blk = n_rows // NA_ROWS
    out = []
    for bi in (0, 1, n_blk - 1):
        r0 = bi * NA_ROWS
        kr0 = int(np.clip(r0 - NA_KH // 2, 0, n_rows - NA_KEY_ROWS))
        rq = r0 + np.arange(NA_ROWS)[:, None]
        rk = kr0 + np.arange(NA_KEY_ROWS)[None, :]
        rs = np.clip(rq - NA_KH // 2, 0, n_rows - NA_KH)
        row_ok = (rk >= rs) & (rk < rs + NA_KH)
        out.append(np.where(row_ok, rk - rq + NA_KH - 1, 2 * NA_KH - 1))
    return np.stack(out)


def _mlstm_prologue(grow_ref, rows_s, cols_s):
    n_l = ML_CHUNK
    n_c = grow_ref.shape[3]
    ti = _iota((n_l, n_l), 0)
    si = _iota((n_l, n_l), 1)
    lower = jnp.where(si <= ti, 1.0, 0.0).astype(BF16)
    upper = jnp.where(si >= ti, 1.0, 0.0).astype(BF16)
    cap = lambda g: ML_GATE_CAP * jnp.tanh(g / ML_GATE_CAP)
    fill = jnp.zeros((LANES - n_c, n_l), F32)
    for hh in range(grow_ref.shape[1]):
        for d in range(2):
            ch = 2 * hh + d
            b_rows = _dot_exact_lhs(_log_sigmoid(cap(grow_ref[0, hh, 2 * d + 1])), lower if d else upper)
            rows_s[ch] = b_rows
            ib_rows = cap(grow_ref[0, hh, 2 * d]) - b_rows
            cols_s[ch] = jnp.concatenate([ib_rows, fill], axis=0).T


def _mlstm_chain(ch, c, q_ref, k_ref, v_ref, rows_s, cols_s, c_s, n_s, m_s, out_ref, accumulate):
    n_l = ML_CHUNK
    hh, rev = ch // 2, ch % 2 == 1
    t0 = pl.multiple_of(c * n_l, n_l)
    hs = pl.ds(hh * ML_HEAD_DIM, ML_HEAD_DIM)
    q = q_ref[0, pl.ds(t0, n_l), hs]
    k = k_ref[0, pl.ds(t0, n_l), hs]
    v = v_ref[0, pl.ds(t0, n_l), hs]
    c_st = c_s[ch]
    n_st = n_s[ch]
    m_st = m_s[ch, 0:1, 0:1]
    b_row = rows_s[ch, pl.ds(c, 1), :]
    lane = _iota((1, LANES), 1)
    ib_col = jnp.sum(jnp.where(lane == c, cols_s[ch], 0.0), axis=1, keepdims=True)
    is_last = lane == (0 if rev else n_l - 1)
    b_last = jnp.sum(jnp.where(is_last, b_row, 0.0), axis=1, keepdims=True)
    kcn = _dot_nt(jnp.concatenate([k, c_st.astype(BF16), n_st.astype(BF16)], axis=0), q)
    yield
    si = _iota((n_l, n_l), 0)
    ti = _iota((n_l, n_l), 1)
    mask = (si >= ti) if rev else (si <= ti)
    log_d = jnp.where(mask, b_row + ib_col, -jnp.inf)
    log_inter = b_row + m_st
    m_t = jnp.maximum(log_inter, jnp.max(log_d, axis=0, keepdims=True))
    m_new = jnp.sum(jnp.where(is_last, m_t, 0.0), axis=1, keepdims=True)
    w_s = jnp.exp(ib_col + (b_last - m_new))
    decay = jnp.exp(b_last + m_st - m_new)
    w_inter = jnp.exp(log_inter - m_t)
    s_w = kcn[:n_l] * jnp.exp(log_d - m_t)
    kw = k.astype(F32) * w_s
    both = _dot_tn(v, jnp.concatenate([s_w.astype(BF16), kw.astype(BF16)], axis=1))
    c_s[ch] = decay * c_st + both[:, n_l:]
    n_s[ch] = jnp.broadcast_to(decay * n_st[0:1] + jnp.sum(kw, axis=0, keepdims=True), n_s.shape[1:])
    m_s[ch] = jnp.broadcast_to(m_new, m_s.shape[1:])
    yield
    num_t = both[:, :n_l] + w_inter * kcn[n_l:2 * n_l]
    den = jnp.sum(s_w, axis=0, keepdims=True) + w_inter * kcn[2 * n_l:2 * n_l + 1]
    h_t = num_t / jnp.maximum(jnp.abs(den), jnp.exp(-m_t))
    if accumulate:
        out_ref[0, pl.ds(t0, n_l), hs] += h_t.T
    else:
        out_ref[0, pl.ds(t0, n_l), hs] = h_t.T


def _mlstm_kernel(ql_ref, kl_ref, vl_ref, gl_ref, qc_ref, kc_ref, vc_ref, gc_ref, hl_ref, hc_ref,
                  rows_l, cols_l, rows_c, cols_c, c_s, n_s, m_s):
    n_l = ML_CHUNK
    n_chains = c_s.shape[0]
    _mlstm_prologue(gc_ref, rows_c, cols_c)
    _mlstm_prologue(gl_ref, rows_l, cols_l)
    c_s[...] = jnp.zeros_like(c_s)
    n_s[...] = jnp.zeros_like(n_s)
    m_s[...] = jnp.zeros_like(m_s)

    def sweep(q_ref, k_ref, v_ref, rows_s, cols_s, out_ref):
        n_c = q_ref.shape[1] // n_l

        def body(accumulate, j, carry):
            _round_robin([_mlstm_chain(ch, (n_c - 1 - j) if ch % 2 else j, q_ref, k_ref, v_ref, rows_s, cols_s,
                                       c_s, n_s, m_s, out_ref, accumulate) for ch in range(n_chains)])
            return carry

        lax.fori_loop(0, n_c // 2, functools.partial(body, False), 0)
        lax.fori_loop(n_c // 2, n_c, functools.partial(body, True), 0)

    sweep(qc_ref, kc_ref, vc_ref, rows_c, cols_c, hc_ref)
    sweep(ql_ref, kl_ref, vl_ref, rows_l, cols_l, hl_ref)


def _mlstm(ql, kl, vl, gl_t, qc, kc, vc, gc_t):
    n_b, n_t, _ = ql.shape
    n_ctx = qc.shape[1]
    n_l = ML_CHUNK
    n_h = ML_HEADS_PER_STEP
    assert (n_t // n_l) % 2 == 0 and (n_ctx // n_l) % 2 == 0

    def gate_rows(g_t, n):
        n_c = n // n_l
        rows = g_t.reshape(n_b, ML_HEADS, 4, n_c, n_l)
        return jnp.pad(rows, ((0, 0),) * 3 + ((0, (-n_c) % 8), (0, 0)))

    glr = gate_rows(gl_t, n_t)
    gcr = gate_rows(gc_t, n_ctx)
    wide = n_h * ML_HEAD_DIM
    n_ch = 2 * n_h

    def seq(n):
        return pl.BlockSpec((1, n, wide), lambda b, hp: (b, 0, hp))

    def gspec(a):
        return pl.BlockSpec((1, n_h) + a.shape[2:], lambda b, hp: (b, hp, 0, 0, 0))

    vm = lambda *s: pltpu.VMEM(s, F32)
    return pl.pallas_call(
        _mlstm_kernel,
        grid=(n_b, ML_HEADS // n_h),
        in_specs=[seq(n_t), seq(n_t), seq(n_t), gspec(glr), seq(n_ctx), seq(n_ctx), seq(n_ctx), gspec(gcr)],
        out_specs=[seq(n_t), seq(n_ctx)],
        out_shape=[jax.ShapeDtypeStruct((n_b, n_t, ML_WIDTH), F32),
                   jax.ShapeDtypeStruct((n_b, n_ctx, ML_WIDTH), F32)],
        scratch_shapes=[vm(n_ch, glr.shape[3], n_l), vm(n_ch, n_l, LANES),
                        vm(n_ch, gcr.shape[3], n_l), vm(n_ch, n_l, LANES),
                        vm(n_ch, ML_HEAD_DIM, ML_HEAD_DIM), vm(n_ch, 16, ML_HEAD_DIM), vm(n_ch, 8, LANES)],
        compiler_params=_cparams(("parallel", "parallel")),
        name="mlstm",
    )(ql, kl, vl, glr, qc, kc, vc, gcr)


def _mlp(x1, sh, sc, gt, w1_ref, w2_ref):
    h = _modulate(x1, sh, sc).astype(BF16)
    acc = jnp.zeros_like(x1)
    n_chunk = 1024
    for c in range(D_FF // n_chunk):
        a = jnp.maximum(_dot(h, w1_ref[:, c * n_chunk:(c + 1) * n_chunk]), 0.0)
        acc = acc + _dot((a * a).astype(BF16), w2_ref[c * n_chunk:(c + 1) * n_chunk, :])
    return x1 + gt * acc


def _ab_out_kernel(x_ref, na_ref, ml_ref, ob_ref, ga_ref, shm_ref, scm_ref, gm_ref, hn_ref,
                   wo_ref, w1_ref, w2_ref, o_ref):
    ml = ml_ref[0]
    parts = []
    for hh in range(ML_HEADS):
        parts.append(_rms(ml[:, hh * ML_HEAD_DIM:(hh + 1) * ML_HEAD_DIM]))
    ml = jnp.concatenate(parts, axis=1) * hn_ref[...] * _sigmoid(ob_ref[0])
    o = _dot(na_ref[0], wo_ref[0:NA_WIDTH, :]) + _dot(ml.astype(BF16), wo_ref[NA_WIDTH:, :])
    x1 = x_ref[0] + ga_ref[0] * o
    o_ref[0] = _mlp(x1, shm_ref[0], scm_ref[0], gm_ref[0], w1_ref, w2_ref)


def _ab_out_mlp(x, na, ml, ob, mods, cond_row, head_norm, w_out, w1, w2):
    n_b, n_t, _ = x.shape
    tm = min(MLP_TILE, n_t)
    tok = lambda w: pl.BlockSpec((1, tm, w), lambda b, i: (b, i, 0))
    return pl.pallas_call(
        _ab_out_kernel,
        grid=(n_b, n_t // tm),
        in_specs=[tok(D_MODEL), tok(512), tok(512), tok(512)]
        + [_mod_spec(j, cond_row) for j in (GATE_A, SHIFT_M, SCALE_M, GATE_M)]
        + [_const_spec((1, ML_WIDTH)), _const_spec((D_MODEL, D_MODEL)),
           _const_spec((D_MODEL, D_FF)), _const_spec((D_FF, D_MODEL))],
        out_specs=tok(D_MODEL),
        out_shape=jax.ShapeDtypeStruct(x.shape, F32),
        compiler_params=_cparams(("parallel", "parallel")),
        name="ab_out_mlp",
    )(x, na, ml, ob, mods, mods, mods, mods, head_norm, w_out, w1, w2)


def _rw_prep_kernel(x_ref, xp_ref, xn_ref, sh_ref, sc_ref, mu_ref, wrkv_ref, w1_ref, w2_ref, w0_ref,
                    a1_ref, a2_ref, a0_ref, g1_ref, g2_ref, kk_ref, ka_ref, bd_ref,
                    r_ref, v_ref, g_ref, a_ref, k0_ref, k1_ref, b0_ref, b1_ref, lw0_ref, lw1_ref):
    i = pl.program_id(1)
    n_i = pl.num_programs(1)
    sh = sh_ref[0]
    sc = sc_ref[0]
    h = _modulate(x_ref[0], sh, sc)
    tm = h.shape[0]
    h_before = _modulate(xp_ref[0, 7:8, :], sh, sc) * jnp.where(i > 0, 1.0, 0.0)
    h_after = _modulate(xn_ref[0, 0:1, :], sh, sc) * jnp.where(i < n_i - 1, 1.0, 0.0)
    row = _iota((tm, 1), 0)
    h_prev = jnp.where(row == 0, h_before, pltpu.roll(h, 1, 0))
    h_next = jnp.where(row == tm - 1, h_after, pltpu.roll(h, tm - 1, 0))
    xx = 0.5 * (h_prev + h_next) - h

    def mix(s):
        return (h + xx * mu_ref[s:s + 1, :]).astype(BF16)

    r = _dot(mix(0), wrkv_ref[0])
    k = _dot(mix(2), wrkv_ref[1])
    v = _dot(mix(3), wrkv_ref[2])
    lane = _iota((1, 2 * RW_LORA), 1)
    hid_w = jnp.tanh(_dot(mix(1), w1_ref[...]))
    hid_a = _dot(mix(4), a1_ref[...])
    g = _dot(_sigmoid(_dot(mix(5), g1_ref[...])).astype(BF16), g2_ref[...])
    kk = k * kk_ref[...]
    ss = _group_sum(kk * kk, bd_ref[...])
    kk = kk * jnp.minimum(lax.rsqrt(ss), 1e12)
    r_ref[0] = r.astype(r_ref.dtype)
    v_ref[0] = v.astype(v_ref.dtype)
    g_ref[0] = g.astype(g_ref.dtype)
    a_ref[0] = (-kk).astype(a_ref.dtype)
    for z, (k_out, b_out, lw_out) in enumerate(((k0_ref, b0_ref, lw0_ref), (k1_ref, b1_ref, lw1_ref))):
        in_dir = (lane >= z * RW_LORA) & (lane < (z + 1) * RW_LORA)
        w_logit = w0_ref[z:z + 1, :] + _dot(jnp.where(in_dir, hid_w, 0.0).astype(BF16), w2_ref[...])
        a = _sigmoid(a0_ref[z:z + 1, :] + _dot(jnp.where(in_dir, hid_a, 0.0).astype(BF16), a2_ref[...]))
        lw_out[0] = -RW_DECAY_SCALE * _sigmoid(w_logit)
        k_out[0] = (k * (1.0 + (a - 1.0) * ka_ref[...])).astype(k_out.dtype)
        b_out[0] = (kk * a).astype(b_out.dtype)


def _rw_prepare(x, mods, cond_row, prm):
    n_b, n_t, _ = x.shape
    tm = min(TOKEN_TILE, n_t)
    tok = pl.BlockSpec((1, tm, D_MODEL), lambda b, i: (b, i, 0))
    n8 = n_t // 8
    prev_spec = pl.BlockSpec((1, 8, D_MODEL), lambda b, i: (b, jnp.maximum(i * (tm // 8) - 1, 0), 0))
    next_spec = pl.BlockSpec((1, 8, D_MODEL), lambda b, i: (b, jnp.minimum((i + 1) * (tm // 8), n8 - 1), 0))
    out = lambda dt: jax.ShapeDtypeStruct((n_b, n_t, D_MODEL), dt)
    return pl.pallas_call(
        _rw_prep_kernel,
        grid=(n_b, n_t // tm),
        in_specs=[tok, prev_spec, next_spec, _mod_spec(SHIFT_A, cond_row), _mod_spec(SCALE_A, cond_row),
                  _const_spec((8, D_MODEL)),
                  _const_spec((3, D_MODEL, D_MODEL)),
                  _const_spec((D_MODEL, 2 * RW_LORA)), _const_spec((2 * RW_LORA, D_MODEL)), _const_spec((2, D_MODEL)),
                  _const_spec((D_MODEL, 2 * RW_LORA)), _const_spec((2 * RW_LORA, D_MODEL)), _const_spec((2, D_MODEL)),
                  _const_spec((D_MODEL, 256)), _const_spec((256, D_MODEL)),
                  _const_spec((1, D_MODEL)), _const_spec((1, D_MODEL)), _const_spec((GROUP_W, GROUP_W))],
        out_specs=[tok] * 10,
        out_shape=[out(BF16)] * 8 + [out(F32)] * 2,
        compiler_params=_cparams(("parallel", "parallel")),
        name="rw_prepare",
    )(x, x, x, mods, mods, prm["mu"], prm["w_rkv"], prm["w1"], prm["w2"], prm["w0"],
      prm["a1"], prm["a2"], prm["a0"], prm["g1"], prm["g2"], prm["k_k"], prm["k_a"], prm["bd"])


def _rw_stage(refs, rows, stage, slot, rev):
    r_ref, lw_ref, k_ref, v_ref, a_ref, b_ref = refs
    n_l = RW_CHUNK
    lw = lw_ref[0, rows, :]
    ti = _iota((n_l, n_l), 0)
    si = _iota((n_l, n_l), 1)
    incl = (si >= ti) if rev else (si <= ti)
    cum = _dot_exact_rhs(jnp.where(incl, 1.0, 0.0).astype(BF16), lw)
    yield
    tot = jnp.sum(lw, axis=0, keepdims=True)
    e_pos = jnp.exp(cum)
    e_neg = jnp.exp(-cum)
    e_tot = jnp.exp(tot)
    rt_s, at_s, bt_s, kt_s, bh_s, kh_s, etot_s = stage
    etot_s[slot] = jnp.broadcast_to(e_tot, etot_s.shape[1:])
    b = b_ref[0, rows, :].astype(F32)
    k = k_ref[0, rows, :].astype(F32)
    rt_s[slot] = (r_ref[0, rows, :].astype(F32) * e_pos).astype(BF16)
    at_s[slot] = (a_ref[0, rows, :].astype(F32) * jnp.exp(cum - lw)).astype(BF16)
    bt = b * e_neg
    kt = k * e_neg
    bt_s[slot] = bt.astype(BF16)
    kt_s[slot] = kt.astype(BF16)
    bh_s[slot] = (bt * e_tot).astype(BF16)
    kh_s[slot] = (kt * e_tot).astype(BF16)


def _rw_pair_fn(v_ref, rows, h_ref, y_ref, stage, slot, rev):
    n_l = RW_CHUNK
    rt_s, at_s, bt_s, kt_s, bh_s, kh_s, etot_s = stage
    lane = _iota((1, LANES), 1)
    lo = lane < RW_HEAD_DIM
    hi = jnp.logical_not(lo)
    ti2 = _iota((2 * n_l, LANES), 0)
    si2 = _iota((2 * n_l, LANES), 1)
    tq = jnp.where(ti2 >= n_l, ti2 - n_l, ti2)
    sk = jnp.where(si2 >= n_l, si2 - n_l, si2)
    strict2 = (sk > tq) if rev else (sk < tq)
    mask_g = strict2 | ((ti2 >= n_l) & (sk == tq))
    eye2 = jnp.where(ti2 == si2, 1.0, 0.0)
    zeros_l = jnp.zeros((n_l, LANES), BF16)
    bd = (_iota((LANES, LANES), 0) < RW_HEAD_DIM) == (_iota((LANES, LANES), 1) < RW_HEAD_DIM)

    def pair(p):
        sl = pl.ds(p * LANES, LANES)
        rt = rt_s[slot, :, sl]
        at = at_s[slot, :, sl]
        bt_p = bt_s[slot, :, sl]
        kt_p = kt_s[slot, :, sl]
        vv = v_ref[0, rows, sl]
        zb = jnp.zeros_like(at)
        ar = jnp.concatenate([at, rt], axis=0)
        g = _dot_nt(ar, jnp.concatenate([jnp.where(lo, bt_p, zb), jnp.where(lo, kt_p, zb),
                                         jnp.where(hi, kt_p, zb), jnp.where(hi, bt_p, zb)], axis=0))
        yield
        g0 = jnp.where(mask_g, g[:, :LANES], 0.0)
        g1 = jnp.where(mask_g, g[:, LANES:], 0.0)
        a_blk = jnp.concatenate([jnp.where(lo, g0[:n_l], 0.0), jnp.where(hi, g1[:n_l], 0.0)], axis=0)
        v_lo = jnp.where(lo, vv, zb)
        v_hi = jnp.where(hi, vv, zb)
        ga = jnp.concatenate([g0[:n_l], g1[:n_l]], axis=1).astype(BF16)
        gav = _dot(ga, jnp.concatenate([zeros_l, v_lo, v_hi, zeros_l], axis=0))
        t_inv = eye2 + a_blk
        ab = a_blk.astype(BF16)
        a_pow = _dot(ab, ab)
        yield
        for _ in range(4):
            ab = a_pow.astype(BF16)
            both = _dot(ab, jnp.concatenate([ab, t_inv.astype(BF16)], axis=1))
            yield
            a_pow = both[:, :LANES]
            t_inv = t_inv + both[:, LANES:]
        t_inv = t_inv + _dot(a_pow.astype(BF16), t_inv.astype(BF16))
        h0 = h_ref[p]
        arh = _dot_nt(ar, h0.astype(BF16))
        yield
        rhs = arh[:n_l] + gav
        rhs = jnp.concatenate([jnp.where(lo, rhs, 0.0), jnp.where(hi, rhs, 0.0)], axis=0).astype(BF16)
        u_st = _dot(t_inv.astype(BF16), rhs)
        yield
        u = (u_st[:n_l] + u_st[n_l:]).astype(BF16)
        if y_ref is not None:
            gr = jnp.concatenate([g0[n_l:], g1[n_l:]], axis=1).astype(BF16)
            uv_heads = jnp.concatenate([jnp.where(lo, u, zb), v_lo, v_hi, jnp.where(hi, u, zb)], axis=0)
            y_ref[0, rows, sl] = arh[n_l:] + _dot(gr, uv_heads)
        upd = _dot_tn(jnp.concatenate([u, vv], axis=0),
                      jnp.concatenate([bh_s[slot, :, sl], kh_s[slot, :, sl]], axis=0))
        yield
        h_ref[p] = h0 * etot_s[slot, 0:1, sl] + jnp.where(bd, upd, 0.0)

    return pair


def _rw_block(refs_f, refs_b, hf_ref, hb_ref, yf_ref, yb_ref, stage_f, stage_b):
    n_l = RW_CHUNK
    n_sub = refs_f[0].shape[1] // n_l
    n_pairs = D_MODEL // LANES

    timeline = []
    for c in range(n_sub):
        rows_f = pl.ds(c * n_l, n_l)
        rows_b = pl.ds((n_sub - 1 - c) * n_l, n_l)
        start = RW_CHUNK_DELAY * c
        timeline.append((start, _rw_stage(refs_f, rows_f, stage_f, c, False)))
        timeline.append((start, _rw_stage(refs_b, rows_b, stage_b, c, True)))
        pair_f = _rw_pair_fn(refs_f[3], rows_f, hf_ref, yf_ref, stage_f, c, False)
        pair_b = _rw_pair_fn(refs_b[3], rows_b, hb_ref, yb_ref, stage_b, c, True)
        timeline.extend((start + 2, f(p)) for p in range(n_pairs) for f in (pair_f, pair_b))
    _run_timeline(timeline)


def _rw_scan_kernel(*refs, n_cc):
    ctx_f, lat_f, ctx_b, lat_b = refs[0:6], refs[6:12], refs[12:18], refs[18:24]
    yf_ref, yb_ref, hf_ref, hb_ref = refs[24:28]
    stage_f, stage_b = refs[28:35], refs[35:42]
    j = pl.program_id(1)

    @pl.when(j == 0)
    def _():
        hf_ref[...] = jnp.zeros_like(hf_ref)
        hb_ref[...] = jnp.zeros_like(hb_ref)

    @pl.when(j < n_cc)
    def _():
        _rw_block(ctx_f, ctx_b, hf_ref, hb_ref, None, None, stage_f, stage_b)

    @pl.when(j >= n_cc)
    def _():
        _rw_block(lat_f, lat_b, hf_ref, hb_ref, yf_ref, yb_ref, stage_f, stage_b)


def _rw_scan(cx, lat):
    n_b, n_t, _ = lat["r"].shape
    n_l = RW_CHUNK
    n_cc = cx["r"].shape[1] // RW_BLOCK
    n_lc = n_t // RW_BLOCK
    blk = (1, RW_BLOCK, D_MODEL)
    ctx_f = pl.BlockSpec(blk, lambda b, j: (b, jnp.minimum(j, n_cc - 1), 0))
    lat_f = pl.BlockSpec(blk, lambda b, j: (b, jnp.maximum(j - n_cc, 0), 0))
    ctx_b = pl.BlockSpec(blk, lambda b, j: (b, jnp.maximum(n_cc - 1 - j, 0), 0))
    lat_b = pl.BlockSpec(blk, lambda b, j: (b, n_lc - 1 - jnp.maximum(j - n_cc, 0), 0))
    fwd = ("r", "lw0", "k0", "v", "a", "b0")
    bwd = ("r", "lw1", "k1", "v", "a", "b1")
    args = ([cx[n] for n in fwd] + [lat[n] for n in fwd] + [cx[n] for n in bwd] + [lat[n] for n in bwd])
    n_sub = RW_BLOCK // n_l
    stage = [pltpu.VMEM((n_sub, n_l, D_MODEL), BF16)] * 6 + [pltpu.VMEM((n_sub, 8, D_MODEL), F32)]
    return pl.pallas_call(
        functools.partial(_rw_scan_kernel, n_cc=n_cc),
        grid=(n_b, n_cc + n_lc),
        in_specs=[ctx_f] * 6 + [lat_f] * 6 + [ctx_b] * 6 + [lat_b] * 6,
        out_specs=[lat_f, lat_b],
        out_shape=[jax.ShapeDtypeStruct((n_b, n_t, D_MODEL), F32)] * 2,
        scratch_shapes=[pltpu.VMEM((D_MODEL // LANES, LANES, LANES), F32)] * 2 + stage + stage,
        compiler_params=_cparams(("parallel", "arbitrary")),
        name="rw_scan",
    )(*args)


def _rw_out_kernel(x_ref, yf_ref, yb_ref, r_ref, k0_ref, k1_ref, v_ref, g_ref, ga_ref, shm_ref, scm_ref, gm_ref,
                   rk_ref, lnw_ref, lnb_ref, bd_ref, wo_ref, w1_ref, w2_ref, o_ref):
    bd = bd_ref[...]
    inv_n = 1.0 / RW_HEAD_DIM
    y = yf_ref[0] + yb_ref[0]
    dlt = y - _group_sum(y, bd) * inv_n
    var = _group_sum(dlt * dlt, bd) * inv_n
    yn = dlt * lax.rsqrt(var + RW_GN_EPS) * lnw_ref[...] + lnb_ref[...]
    r = r_ref[0].astype(F32)
    kd = k0_ref[0].astype(F32) + k1_ref[0].astype(F32)
    coef = _group_sum(r * kd * rk_ref[...], bd)
    o = _dot(((yn + coef * v_ref[0].astype(F32)) * g_ref[0].astype(F32)).astype(BF16), wo_ref[...])
    x1 = x_ref[0] + ga_ref[0] * o
    o_ref[0] = _mlp(x1, shm_ref[0], scm_ref[0], gm_ref[0], w1_ref, w2_ref)


def _rw_out_mlp(x, yf, yb, lat, mods, prm, w1, w2):
    n_b, n_t, _ = x.shape
    tm = min(MLP_TILE, n_t)
    tok = pl.BlockSpec((1, tm, D_MODEL), lambda b, i: (b, i, 0))
    mod = [_mod_spec(j) for j in (GATE_A, SHIFT_M, SCALE_M, GATE_M)]
    vec = _const_spec((1, D_MODEL))
    return pl.pallas_call(
        _rw_out_kernel,
        grid=(n_b, n_t // tm),
        in_specs=[tok] * 8 + mod + [vec, vec, vec, _const_spec((GROUP_W, GROUP_W)),
                                          _const_spec((D_MODEL, D_MODEL)),
                                          _const_spec((D_MODEL, D_FF)), _const_spec((D_FF, D_MODEL))],
        out_specs=tok,
        out_shape=jax.ShapeDtypeStruct(x.shape, F32),
        compiler_params=_cparams(("parallel", "parallel")),
        name="rw_out_mlp",
    )(x, yf, yb, lat["r"], lat["k0"], lat["k1"], lat["v"], lat["g"],
      mods, mods, mods, mods, prm["r_k"], prm["lnx_w"], prm["lnx_b"], prm["bd"], prm["w_o"], w1, w2)


def _ab_params(w_in, gate_b, q_norm, k_norm, n_tokens):
    n_proj = 3 * NA_WIDTH + 4 * ML_WIDTH
    head_major = np.array([4 * (j % 4) + j // 4 for j in range(4 * ML_HEADS)])
    wg = w_in[:, n_proj:][:, head_major]
    pos = np.arange(n_tokens)
    n_freq = ML_HEAD_DIM // 4
    inv_freq = (ROPE_BASE ** (-np.arange(n_freq, dtype=np.float32) / n_freq)).astype(np.float32)
    ang_r = ((pos // GRID_W).astype(np.float32)[:, None] * inv_freq).astype(np.float64)
    ang_c = ((pos % GRID_W).astype(np.float32)[:, None] * inv_freq).astype(np.float64)
    cos = np.concatenate([np.cos(ang_r)] * 2 + [np.cos(ang_c)] * 2, axis=1).astype(np.float32)
    sin = np.concatenate([-np.sin(ang_r), np.sin(ang_r), -np.sin(ang_c), np.sin(ang_c)], axis=1).astype(np.float32)
    gb = gate_b[head_major]
    return dict(w_all=w_in[:, :n_proj].astype(BF16), wg_t=wg.T.astype(BF16), bd=_group_ones(),
                q_norm=jnp.tile(q_norm, NA_HEADS)[None], k_norm=jnp.tile(k_norm, NA_HEADS)[None],
                cos=jnp.asarray(cos), sin=jnp.asarray(sin), gate_b_t=gb[:, None])


def _rw_params(mu, w_rkv, w0, w1, w2, a0, a1, a2, g1, g2, k_k, k_a, r_k, lnx_w, lnx_b, w_o):
    pad_g = 256 - RW_GATE_LORA
    return dict(
        mu=jnp.concatenate([mu, jnp.zeros((2, D_MODEL), F32)], axis=0),
        w_rkv=w_rkv.astype(BF16),
        w1=jnp.concatenate([w1[0], w1[1]], axis=1).astype(BF16),
        w2=jnp.concatenate([w2[0], w2[1]], axis=0).astype(BF16), w0=w0,
        a1=jnp.concatenate([a1[0], a1[1]], axis=1).astype(BF16),
        a2=jnp.concatenate([a2[0], a2[1]], axis=0).astype(BF16), a0=a0,
        g1=jnp.pad(g1, ((0, 0), (0, pad_g))).astype(BF16), g2=jnp.pad(g2, ((0, pad_g), (0, 0))).astype(BF16),
        k_k=k_k[None], k_a=k_a[None], r_k=r_k[None], lnx_w=lnx_w[None], lnx_b=lnx_b[None],
        bd=_group_ones(), w_o=w_o.astype(BF16))


_PREP_NAMES = ("r", "v", "g", "a", "k0", "k1", "b0", "b1", "lw0", "lw1")


def kernel(x, c, ctx, c_ctx, ada_w, ada_b, ab_w_in, ab_gate_b, na_q_norm, na_k_norm, na_rpb, ml_head_norm, ab_w_out, rw_mu, rw_w_rkv, rw_w0, rw_w1, rw_w2, rw_a0, rw_a1, rw_a2, rw_g1, rw_g2, rw_k_k, rw_k_a, rw_r_k, rw_lnx_w, rw_lnx_b, rw_w_o, mlp_w1, mlp_w2):
    n_b, n_t, _ = x.shape
    assert ada_w.shape[0] == 2, "even (attention / mLSTM) layer followed by an odd (RWKV-7) layer"
    cond = jnp.concatenate([c, c_ctx[None], jnp.zeros((16 - n_b - 1, D_MODEL), F32)], axis=0)
    mods = _adaln(cond, ada_w, ada_b).reshape(2, 16 * 6, 1, D_MODEL)
    ctx_row = n_b

    prm = _ab_params(ab_w_in[0], ab_gate_b[0], na_q_norm[0], na_k_norm[0], n_t)
    qa_l, ka_l, va_l, qb_l, kb_l, vb_l, ob_l, gt_l = _ab_project(x, mods[0], None, prm, True)
    qa_c, ka_c, va_c, qb_c, kb_c, vb_c, ob_c, gt_c = _ab_project(ctx, mods[0], ctx_row, prm, False)
    na_l = _na_attention(qa_l, ka_l, va_l, ka_c, va_c, _na_col_tiles(na_rpb[0]))
    na_c = _ctx_attention(qa_c, ka_c, va_c)
    ml_l, ml_c = _mlstm(qb_l, kb_l, vb_l, gt_l, qb_c, kb_c, vb_c, gt_c)
    head_norm = ml_head_norm[0].reshape(1, ML_WIDTH)
    w_out = ab_w_out[0].astype(BF16)
    w1 = mlp_w1[0].astype(BF16)
    w2 = mlp_w2[0].astype(BF16)
    x = _ab_out_mlp(x, na_l, ml_l, ob_l, mods[0], None, head_norm, w_out, w1, w2)
    ctx = _ab_out_mlp(ctx, na_c, ml_c, ob_c, mods[0], ctx_row, head_norm, w_out, w1, w2)

    rprm = _rw_params(rw_mu[0], rw_w_rkv[0], rw_w0[0], rw_w1[0], rw_w2[0], rw_a0[0], rw_a1[0], rw_a2[0],
                      rw_g1[0], rw_g2[0], rw_k_k[0], rw_k_a[0], rw_r_k[0], rw_lnx_w[0], rw_lnx_b[0], rw_w_o[0])
    lat = dict(zip(_PREP_NAMES, _rw_prepare(x, mods[1], None, rprm)))
    cxp = dict(zip(_PREP_NAMES, _rw_prepare(ctx, mods[1], ctx_row, rprm)))
    yf, yb = _rw_scan(cxp, lat)
    return _rw_out_mlp(x, yf, yb, lat, mods[1], rprm, mlp_w1[1].astype(BF16), mlp_w2[1].astype(BF16))
```
